```python
import jax
import jax.numpy as jnp
from jax import lax
import numpy as np

D_MODEL = 1024
BATCH = 8
SEQ = 2048
DEPTH = 4

D_MIX = D_MODEL
HEAD_DIM = 64
FOX_DIM = 3 * D_MIX // 8
HGRN_DIM = 3 * D_MIX // 8
POOL_DIM = D_MIX - FOX_DIM - HGRN_DIM
FOX_HEADS = FOX_DIM // HEAD_DIM
HGRN_HEADS = HGRN_DIM // HEAD_DIM
POOL_WINDOWS = (2, 4, 8, 16)
POOL_GROUPS = len(POOL_WINDOWS)
POOL_GROUP_DIM = POOL_DIM // POOL_GROUPS
Q_BLOCK = 128
HGRN_CHUNK = 64
N_EXPERT_GROUPS = 4
EXPERTS_PER_GROUP = 4
N_EXPERTS = N_EXPERT_GROUPS * EXPERTS_PER_GROUP
TOP_K = 2
D_FF_EXPERT = D_MODEL // 2
N_MOD = 6
EPS = 1e-6
PROJ_SPLITS = (FOX_DIM, FOX_DIM, FOX_DIM, FOX_HEADS,
               HGRN_DIM, HGRN_DIM, HGRN_DIM, HGRN_DIM, POOL_DIM)
PROJ_DIM = sum(PROJ_SPLITS)

kernel_name = 'hybrid_fox_hgrn2_pool_hmoe_adaln'


def rms_norm(x, gain):
    xf = x.astype(jnp.float32)
    y = xf * lax.rsqrt(jnp.mean(xf * xf, axis=-1, keepdims=True) + EPS)
    return (y * gain.astype(jnp.float32)).astype(x.dtype)


def head_rms_norm(x, gain):
    H, Dh = x.shape[-2], x.shape[-1]
    return rms_norm(x, gain.reshape(H, Dh))


def modulate(h, shift, scale):
    return h * (1 + scale) + shift


def split_heads(t, n_heads):
    B, S, _ = t.shape
    return t.reshape(B, S, n_heads, -1).transpose(0, 2, 1, 3)


def fox_attention(q, k, v, log_f):
    B, H, S, Dh = q.shape
    F = jnp.cumsum(log_f, axis=-1)
    scale = Dh ** -0.5
    neg = jnp.finfo(jnp.float32).min
    outs = []
    for blk in range(S // Q_BLOCK):
        lo, hi = blk * Q_BLOCK, (blk + 1) * Q_BLOCK
        qb = q[:, :, lo:hi]
        kb = k[:, :, :hi]
        vb = v[:, :, :hi]
        logits = jnp.einsum('bhqd,bhkd->bhqk', qb, kb).astype(jnp.float32) * scale
        logits = logits + F[:, :, lo:hi, None] - F[:, :, None, :hi]
        mask = (lo + jnp.arange(Q_BLOCK))[:, None] >= jnp.arange(hi)[None, :]
        logits = jnp.where(mask, logits, neg)
        p = jax.nn.softmax(logits, axis=-1)
        outs.append(jnp.einsum('bhqk,bhkd->bhqd', p.astype(v.dtype), vb))
    return jnp.concatenate(outs, axis=2)


def hgrn2_chunkwise(q, k, v, log_f):
    B, H, S, Dk = q.shape
    Dv = v.shape[-1]
    C = HGRN_CHUNK
    N = S // C

    def to_chunks(t):
        return t.astype(jnp.float32).reshape(B, H, N, C, t.shape[-1]).transpose(2, 0, 1, 3, 4)

    qc, kc, vc, gc = to_chunks(q), to_chunks(k), to_chunks(v), to_chunks(log_f)
    causal = jnp.tril(jnp.ones((C, C), dtype=bool))[:, :, None]

    def step(state, inp):
        qi, ki, vi, gi = inp
        b = jnp.cumsum(gi, axis=2)
        rel = b[:, :, :, None, :] - b[:, :, None, :, :]
        decay = jnp.exp(jnp.where(causal, rel, -jnp.inf))
        scores = jnp.einsum('bhtd,bhsd,bhtsd->bhts', qi, ki, decay)
        o_intra = jnp.einsum('bhts,bhse->bhte', scores, vi)
        o_inter = jnp.einsum('bhtd,bhde->bhte', qi * jnp.exp(b), state)
        b_last = b[:, :, -1:, :]
        k_dec = ki * jnp.exp(b_last - b)
        new_state = jnp.exp(b_last[:, :, 0, :])[..., None] * state + jnp.einsum('bhsd,bhse->bhde', k_dec, vi)
        return new_state, o_intra + o_inter

    state0 = jnp.zeros((B, H, Dk, Dv), jnp.float32)
    _, oc = lax.scan(step, state0, (qc, kc, vc, gc))
    return oc.transpose(1, 2, 0, 3, 4).reshape(B, H, S, Dv).astype(v.dtype)


def causal_multiscale_pool(u):
    B, S, _ = u.shape
    uf = u.astype(jnp.float32).reshape(B, S, POOL_GROUPS, POOL_GROUP_DIM)
    cs = jnp.concatenate([jnp.zeros((B, 1, POOL_GROUPS, POOL_GROUP_DIM), jnp.float32),
                          jnp.cumsum(uf, axis=1)], axis=1)
    pos = jnp.arange(S)
    means = []
    for g, w in enumerate(POOL_WINDOWS):
        start = jnp.maximum(pos + 1 - w, 0)
        window_sum = cs[:, 1:, g] - cs[:, start, g]
        count = (pos + 1 - start).astype(jnp.float32)
        means.append(window_sum / count[None, :, None])
    mean = jnp.stack(means, axis=2)
    return (mean - uf).astype(u.dtype)


def parallel_mixer(h, w_in, fox_f_bias, fox_norm_g, lower_bound, hgrn_norm_g, pool_w, pool_scale, w_out):
    B, S, _ = h.shape
    proj = h @ w_in
    offsets = []
    acc = 0
    for size in PROJ_SPLITS[:-1]:
        acc += size
        offsets.append(acc)
    fq, fk, fv, ff, hq, hf, hi, hg, pu = jnp.split(proj, offsets, axis=-1)

    log_fox = jax.nn.log_sigmoid(ff.astype(jnp.float32) + fox_f_bias.astype(jnp.float32))
    o_fox = fox_attention(split_heads(fq, FOX_HEADS), split_heads(fk, FOX_HEADS),
                          split_heads(fv, FOX_HEADS), log_fox.transpose(0, 2, 1))
    o_fox = head_rms_norm(o_fox.transpose(0, 2, 1, 3), fox_norm_g).reshape(B, S, FOX_DIM)

    f = lower_bound + (1 - lower_bound) * jax.nn.sigmoid(hf.astype(jnp.float32))
    o_hgrn = hgrn2_chunkwise(split_heads(jax.nn.silu(hq), HGRN_HEADS), split_heads(1 - f, HGRN_HEADS),
                             split_heads(hi, HGRN_HEADS), split_heads(jnp.log(f), HGRN_HEADS))
    o_hgrn = head_rms_norm(o_hgrn.transpose(0, 2, 1, 3), hgrn_norm_g).reshape(B, S, HGRN_DIM)
    o_hgrn = o_hgrn * jax.nn.silu(hg)

    pooled = causal_multiscale_pool(pu)
    o_pool = jnp.einsum('bsgc,gcd->bsgd', pooled, pool_w).reshape(B, S, POOL_DIM) * pool_scale

    merged = jnp.concatenate([o_fox.astype(h.dtype), o_hgrn.astype(h.dtype), o_pool.astype(h.dtype)], axis=-1)
    return merged @ w_out


def hierarchical_moe(h, w_rg, b_rg, w_re, b_re, w_gate, w_up, w_down):
    B, S, D = h.shape
    t = h.reshape(B * S, D)
    T = t.shape[0]
    group_logits = (t @ w_rg + b_rg).astype(jnp.float32)
    group_prob = jax.nn.softmax(group_logits, axis=-1)
    g_sel = jnp.argmax(group_logits, axis=-1)
    p_group = jnp.take_along_axis(group_prob, g_sel[:, None], axis=-1)
    expert_logits = (t @ w_re + b_re).astype(jnp.float32).reshape(T, N_EXPERT_GROUPS, EXPERTS_PER_GROUP)
    local_logits = jnp.take_along_axis(expert_logits, g_sel[:, None, None], axis=1)[:, 0]
    local_prob = jax.nn.softmax(local_logits, axis=-1)
    top_p, top_i = lax.top_k(local_prob, TOP_K)
    top_w = top_p / jnp.sum(top_p, axis=-1, keepdims=True) * p_group
    expert_id = g_sel[:, None] * EXPERTS_PER_GROUP + top_i
    combine = jnp.sum(jax.nn.one_hot(expert_id, N_EXPERTS, dtype=jnp.float32) * top_w[..., None], axis=1)

    def expert_step(acc, inp):
        wg, wu, wd, cw = inp
        hidden = jax.nn.silu(t @ wg) * (t @ wu)
        return acc + cw[:, None].astype(t.dtype) * (hidden @ wd), None

    out, _ = lax.scan(expert_step, jnp.zeros_like(t), (w_gate, w_up, w_down, combine.T))
    return out.reshape(B, S, D)


def setup_inputs(seed: int = 0) -> dict:
    key = jax.random.key(seed)
    ks = jax.random.split(key, 24)

    def nrm(k, shape, scale):
        return jax.random.normal(k, shape, jnp.float32) * scale

    D, E, F = D_MODEL, N_EXPERTS, D_FF_EXPERT
    return {
        'x': nrm(ks[0], (BATCH, SEQ, D), 1.0),
        'c': nrm(ks[1], (BATCH, D), 1.0),
        'w_ada': nrm(ks[2], (DEPTH, D, N_MOD * D), 0.1 * D ** -0.5),
        'b_ada': nrm(ks[3], (DEPTH, N_MOD * D), 0.02),
        'norm1_g': 1.0 + nrm(ks[4], (DEPTH, D), 0.05),
        'w_in': nrm(ks[5], (DEPTH, D, PROJ_DIM), D ** -0.5),
        'fox_f_bias': jax.random.uniform(ks[6], (DEPTH, FOX_HEADS), jnp.float32, 1.0, 5.0),
        'fox_norm_g': 1.0 + nrm(ks[7], (DEPTH, FOX_DIM), 0.05),
        'hgrn_lb_logits': nrm(ks[8], (DEPTH, HGRN_DIM), 0.1),
        'hgrn_norm_g': 1.0 + nrm(ks[9], (DEPTH, HGRN_DIM), 0.05),
        'pool_w': nrm(ks[10], (DEPTH, POOL_GROUPS, POOL_GROUP_DIM, POOL_GROUP_DIM), POOL_GROUP_DIM ** -0.5),
        'pool_scale': 1.0 + nrm(ks[11], (DEPTH, POOL_DIM), 0.05),
        'w_out': nrm(ks[12], (DEPTH, D_MIX, D), D_MIX ** -0.5),
        'norm2_g': 1.0 + nrm(ks[13], (DEPTH, D), 0.05),
        'router_group_w': nrm(ks[14], (DEPTH, D, N_EXPERT_GROUPS), D ** -0.5),
        'router_group_b': nrm(ks[15], (DEPTH, N_EXPERT_GROUPS), 0.01),
        'router_expert_w': nrm(ks[16], (DEPTH, D, E), D ** -0.5),
        'router_expert_b': nrm(ks[17], (DEPTH, E), 0.01),
        'expert_w_gate': nrm(ks[18], (DEPTH, E, D, F), D ** -0.5),
        'expert_w_up': nrm(ks[19], (DEPTH, E, D, F), D ** -0.5),
        'expert_w_down': nrm(ks[20], (DEPTH, E, F, D), F ** -0.5),
        'final_norm_g': 1.0 + nrm(ks[21], (D,), 0.05),
    }


def reference(x, c, w_ada, b_ada, norm1_g, w_in, fox_f_bias, fox_norm_g, hgrn_lb_logits, hgrn_norm_g,
              pool_w, pool_scale, w_out, norm2_g, router_group_w, router_group_b, router_expert_w,
              router_expert_b, expert_w_gate, expert_w_up, expert_w_down, final_norm_g):
    mod_all = jnp.einsum('bd,lde->lbe', jax.nn.silu(c), w_ada) + b_ada[:, None, :]
    lb_cum = jnp.cumsum(jax.nn.softmax(hgrn_lb_logits.astype(jnp.float32), axis=0), axis=0)
    lower_bounds = lb_cum - lb_cum[0:1]
    for l in range(DEPTH):
        shift1, scale1, gate1, shift2, scale2, gate2 = jnp.split(mod_all[l][:, None, :], N_MOD, axis=-1)
        h = modulate(rms_norm(x, norm1_g[l]), shift1, scale1)
        y = parallel_mixer(h, w_in[l], fox_f_bias[l], fox_norm_g[l], lower_bounds[l], hgrn_norm_g[l],
                           pool_w[l], pool_scale[l], w_out[l])
        x = x + (1 + gate1) * y
        h = modulate(rms_norm(x, norm2_g[l]), shift2, scale2)
        y = hierarchical_moe(h, router_group_w[l], router_group_b[l], router_expert_w[l], router_expert_b[l],
                             expert_w_gate[l], expert_w_up[l], expert_w_down[l])
        x = x + (1 + gate2) * y
    return rms_norm(x, final_norm_g)
```

```python
import functools

import jax
import jax.numpy as jnp
from jax import lax
from jax.experimental import pallas as pl
from jax.experimental.pallas import tpu as pltpu

F32 = jnp.float32
BF16 = jnp.bfloat16
HIGHEST = lax.Precision.HIGHEST

HEAD_DIM = 64
LANES = 128
EPS = 1e-6
N_MOD = 6
POOL_WINDOWS = (2, 4, 8, 16)
N_GROUPS = 4
EXPERTS_PER_GROUP = 4
N_EXPERTS = N_GROUPS * EXPERTS_PER_GROUP
ROUTER_EXPERT_LANE0 = N_GROUPS
HGRN_CHUNK = 128
HGRN_LEVELS = (64, 32, 16, 8)
HGRN_DIAG = 8
VMEM_LIMIT = 48 * 1024 * 1024


def _cparams(sem):
    return pltpu.CompilerParams(dimension_semantics=sem, vmem_limit_bytes=VMEM_LIMIT)


def _silu(x):
    return x * jax.nn.sigmoid(x)


def _iota(shape, dim):
    return lax.broadcasted_iota(jnp.int32, shape, dim)


def _ada_kernel(c_ref, w_ref, b_ref, o_ref):
    sc = _silu(c_ref[...]).astype(BF16)
    o_ref[0] = jnp.dot(sc, w_ref[0].astype(BF16), preferred_element_type=F32) + b_ref[0]


def _ada_call(c, w_ada, b_ada):
    depth, d, n = w_ada.shape
    bsz = c.shape[0]
    tn = 1536
    return pl.pallas_call(
        _ada_kernel,
        out_shape=jax.ShapeDtypeStruct((depth, bsz, n), F32),
        grid=(depth, n // tn),
        in_specs=[
            pl.BlockSpec((bsz, d), lambda l, j: (0, 0)),
            pl.BlockSpec((1, d, tn), lambda l, j: (l, 0, j)),
            pl.BlockSpec((1, 1, tn), lambda l, j: (l, 0, j)),
        ],
        out_specs=pl.BlockSpec((1, bsz, tn), lambda l, j: (l, 0, j)),
        compiler_params=_cparams(("arbitrary", "arbitrary")),
        name="ada_mod",
    )(c, w_ada, b_ada.reshape(depth, 1, n))


def _lb_kernel(x_ref, o_ref):
    x = x_ref[...]
    depth = x.shape[0]
    e = jnp.exp(x - jnp.max(x, axis=0, keepdims=True))
    p = e / jnp.sum(e, axis=0, keepdims=True)
    acc = jnp.zeros_like(p[0:1])
    o_ref[0:1, :] = acc
    for l in range(1, depth):
        acc = acc + p[l:l + 1]
        o_ref[l:l + 1, :] = acc


def _lb_call(lb_logits):
    return pl.pallas_call(
        _lb_kernel,
        out_shape=jax.ShapeDtypeStruct(lb_logits.shape, F32),
        name="hgrn_lower_bounds",
    )(lb_logits)


def _norm_mod(x, g, shift, scale):
    ms = jnp.mean(x * x, axis=-1, keepdims=True)
    return (x * lax.rsqrt(ms + EPS) * g) * (1.0 + scale) + shift


def _inproj_kernel(x_ref, g_ref, mod_ref, w_ref, qkv_ref, hg_ref, pu_ref, ff_ref, *, n_qkv, n_hg, n_pu):
    h = _norm_mod(x_ref[0], g_ref[...], mod_ref[0, 0:1, :], mod_ref[0, 1:2, :]).astype(BF16)
    o0, o1, o2 = n_qkv, n_qkv + n_hg, n_qkv + n_hg + n_pu
    qkv_ref[0] = jnp.dot(h, w_ref[:, 0:o0], preferred_element_type=F32).astype(BF16)
    hg_ref[0] = jnp.dot(h, w_ref[:, o0:o1], preferred_element_type=F32)
    pu_ref[0] = jnp.dot(h, w_ref[:, o1:o2], preferred_element_type=F32)
    ff_ref[0] = jnp.dot(h, w_ref[:, o2:], preferred_element_type=F32)


def _inproj_call(x, g, mod, w, n_qkv, n_hg, n_pu, tm):
    bsz, seq, d = x.shape
    n_all = w.shape[1]
    n_ff = n_all - n_qkv - n_hg - n_pu
    return pl.pallas_call(
        functools.partial(_inproj_kernel, n_qkv=n_qkv, n_hg=n_hg, n_pu=n_pu),
        out_shape=(
            jax.ShapeDtypeStruct((bsz, seq, n_qkv), BF16),
            jax.ShapeDtypeStruct((bsz, seq, n_hg), F32),
            jax.ShapeDtypeStruct((bsz, seq, n_pu), F32),
            jax.ShapeDtypeStruct((bsz, seq, n_ff), F32),
        ),
        grid=(bsz, seq // tm),
        in_specs=[
            pl.BlockSpec((1, tm, d), lambda b, i: (b, i, 0)),
            pl.BlockSpec((1, d), lambda b, i: (0, 0)),
            pl.BlockSpec((1, N_MOD, d), lambda b, i: (b, 0, 0)),
            pl.BlockSpec((d, n_all), lambda b, i: (0, 0)),
        ],
        out_specs=(
            pl.BlockSpec((1, tm, n_qkv), lambda b, i: (b, i, 0)),
            pl.BlockSpec((1, tm, n_hg), lambda b, i: (b, i, 0)),
            pl.BlockSpec((1, tm, n_pu), lambda b, i: (b, i, 0)),
            pl.BlockSpec((1, tm, n_ff), lambda b, i: (b, i, 0)),
        ),
        compiler_params=_cparams(("arbitrary", "arbitrary")),
        name="norm_inproj",
    )(x, g, mod, w)


def _fbias_kernel(ff_ref, bias_ref, fcol_ref, frow_ref, *, cb):
    seq = ff_ref.shape[1]
    tri = (_iota((cb, cb), 0) >= _iota((cb, cb), 1)).astype(F32)
    carry = jnp.zeros((1, LANES), F32)
    for blk in range(seq // cb):
        x = ff_ref[0, blk * cb:(blk + 1) * cb, :] + bias_ref[...]
        log_f = jnp.minimum(x, 0.0) - jnp.log1p(jnp.exp(-jnp.abs(x)))
        cs = jnp.dot(tri, log_f, precision=HIGHEST, preferred_element_type=F32) + carry
        fcol_ref[0, blk * cb:(blk + 1) * cb, :] = cs
        frow_ref[0, :, blk * cb:(blk + 1) * cb] = cs.T[0:8, :]
        carry = cs[cb - 1:cb, :]


def _fbias_call(ff, bias):
    bsz, seq, _ = ff.shape
    cb = min(256, seq)
    return pl.pallas_call(
        functools.partial(_fbias_kernel, cb=cb),
        out_shape=(jax.ShapeDtypeStruct((bsz, seq, LANES), F32),
                   jax.ShapeDtypeStruct((bsz, 8, seq), F32)),
        grid=(bsz,),
        in_specs=[pl.BlockSpec((1, seq, LANES), lambda b: (b, 0, 0)),
                  pl.BlockSpec((1, LANES), lambda b: (0, 0))],
        out_specs=(pl.BlockSpec((1, seq, LANES), lambda b: (b, 0, 0)),
                   pl.BlockSpec((1, 8, seq), lambda b: (b, 0, 0))),
        compiler_params=_cparams(("arbitrary",)),
        name="fox_forget_bias",
    )(ff, bias)


def _fox_kernel(q_ref, k_ref, v_ref, fc_ref, fr_ref, gn_ref, o_ref, *, tq):
    p = pl.program_id(1)
    i = pl.program_id(2)
    lane = _iota((1, LANES), 1)
    h0 = lane < HEAD_DIM
    q = q_ref[0] * jnp.asarray(HEAD_DIM ** -0.5, BF16)
    zero = jnp.zeros_like(q)
    q0 = jnp.where(h0, q, zero)
    q1 = jnp.where(h0, zero, q)
    fc = fc_ref[0]
    f0 = jnp.sum(jnp.where(lane == 2 * p, fc, 0.0), axis=-1, keepdims=True)
    f1 = jnp.sum(jnp.where(lane == 2 * p + 1, fc, 0.0), axis=-1, keepdims=True)
    nt = (((1,), (1,)), ((), ()))

    def step(j, carry, masked):
        m0, l0, m1, l1, acc = carry
        start = pl.multiple_of(j * tq, tq)
        ks = k_ref[0, pl.ds(start, tq), :]
        vs = v_ref[0, pl.ds(start, tq), :]
        fr0 = fr_ref[0, pl.ds(2 * p, 1), pl.ds(start, tq)]
        fr1 = fr_ref[0, pl.ds(2 * p + 1, 1), pl.ds(start, tq)]
        s0 = lax.dot_general(q0, ks, nt, preferred_element_type=F32) + (f0 - fr0)
        s1 = lax.dot_general(q1, ks, nt, preferred_element_type=F32) + (f1 - fr1)
        if masked:
            causal = _iota((tq, tq), 0) >= _iota((tq, tq), 1)
            s0 = jnp.where(causal, s0, -jnp.inf)
            s1 = jnp.where(causal, s1, -jnp.inf)
        n0 = jnp.maximum(m0, jnp.max(s0, axis=-1, keepdims=True))
        n1 = jnp.maximum(m1, jnp.max(s1, axis=-1, keepdims=True))
        p0 = jnp.exp(s0 - n0)
        p1 = jnp.exp(s1 - n1)
        a0 = jnp.exp(m0 - n0)
        a1 = jnp.exp(m1 - n1)
        l0 = a0 * l0 + jnp.sum(p0, axis=-1, keepdims=True)
        l1 = a1 * l1 + jnp.sum(p1, axis=-1, keepdims=True)
        pv0 = jnp.dot(p0.astype(BF16), vs, preferred_element_type=F32)
        pv1 = jnp.dot(p1.astype(BF16), vs, preferred_element_type=F32)
        acc = jnp.where(h0, a0, a1) * acc + jnp.where(h0, pv0, pv1)
        return n0, l0, n1, l1, acc

    neg = jnp.full((tq, 1), -1e30, F32)
    zcol = jnp.zeros((tq, 1), F32)
    init = (neg, zcol, neg, zcol, jnp.zeros((tq, LANES), F32))
    carry = lax.fori_loop(0, i, functools.partial(step, masked=False), init)
    _, l0, _, l1, acc = step(i, carry, True)
    o = acc / jnp.where(h0, l0, l1)
    o2 = o * o
    ms0 = jnp.sum(jnp.where(h0, o2, 0.0), axis=-1, keepdims=True)
    ms1 = jnp.sum(jnp.where(h0, 0.0, o2), axis=-1, keepdims=True)
    ms = jnp.where(h0, ms0, ms1) * (1.0 / HEAD_DIM)
    o_ref[0] = (o * lax.rsqrt(ms + EPS) * gn_ref[0]).astype(BF16)


def _fox_call(qkv, fcol, frow, gn, n_pairs, tq):
    bsz, seq, _ = qkv.shape
    return pl.pallas_call(
        functools.partial(_fox_kernel, tq=tq),
        out_shape=jax.ShapeDtypeStruct((bsz, seq, n_pairs * LANES), BF16),
        grid=(bsz, n_pairs, seq // tq),
        in_specs=[
            pl.BlockSpec((1, tq, LANES), lambda b, p, i: (b, i, p)),
            pl.BlockSpec((1, seq, LANES), lambda b, p, i: (b, 0, n_pairs + p)),
            pl.BlockSpec((1, seq, LANES), lambda b, p, i: (b, 0, 2 * n_pairs + p)),
            pl.BlockSpec((1, tq, LANES), lambda b, p, i: (b, i, 0)),
            pl.BlockSpec((1, 8, seq), lambda b, p, i: (b, 0, 0)),
            pl.BlockSpec((1, 1, LANES), lambda b, p, i: (p, 0, 0)),
        ],
        out_specs=pl.BlockSpec((1, tq, LANES), lambda b, p, i: (b, i, p)),
        compiler_params=_cparams(("arbitrary", "arbitrary", "arbitrary")),
        name="fox_attention",
    )(qkv, qkv, qkv, fcol, frow, gn)


def _hgrn_kernel(hq_ref, hf_ref, hi_ref, hg_ref, lb_ref, gn_ref, o_ref, p_sc):
    ch = HGRN_CHUNK
    seq = hq_ref.shape[1]
    lane = _iota((1, LANES), 1)
    h0 = lane < HEAD_DIM
    r = _iota((ch, ch), 0)
    c = _iota((ch, ch), 1)
    tri = (r >= c).astype(F32)
    same_head = (r < HEAD_DIM) == (c < HEAD_DIM)
    seg = same_head.astype(BF16)
    ones = jnp.ones((ch, LANES), F32)
    gathers = []
    level_masks = []
    for m in HGRN_LEVELS:
        blk_r = r // (2 * m)
        gathers.append((c == blk_r * (2 * m) + (m - 1)).astype(F32))
        level_masks.append((blk_r == c // (2 * m)) & (r % (2 * m) >= m) & (c % (2 * m) < m))
    gather = jnp.concatenate(gathers, axis=0)
    row8 = _iota((HGRN_DIAG, LANES), 0)
    lb = lb_ref[0]
    nt = (((1,), (1,)), ((), ()))

    def chunk(ci, state):
        sl = pl.ds(pl.multiple_of(ci * ch, ch), ch)
        f = lb + (1.0 - lb) * jax.nn.sigmoid(hf_ref[0, sl, :])
        g = jnp.log(f)
        k = 1.0 - f
        q = _silu(hq_ref[0, sl, :])
        v = hi_ref[0, sl, :]
        vb = v.astype(BF16)
        b = jnp.dot(tri, g, precision=HIGHEST, preferred_element_type=F32)
        b_last = b[ch - 1:ch, :]
        refs = jnp.dot(gather, b, precision=HIGHEST, preferred_element_type=F32)

        sc0 = jnp.zeros((ch, ch), F32)
        sc1 = jnp.zeros((ch, ch), F32)
        for li in range(len(HGRN_LEVELS)):
            ref_l = refs[li * ch:(li + 1) * ch]
            qt = q * jnp.exp(jnp.minimum(b - ref_l, 0.0))
            kt = (k * jnp.exp(jnp.minimum(ref_l - b, 0.0))).astype(BF16)
            s0 = lax.dot_general(jnp.where(h0, qt, 0.0).astype(BF16), kt, nt, preferred_element_type=F32)
            s1 = lax.dot_general(jnp.where(h0, 0.0, qt).astype(BF16), kt, nt, preferred_element_type=F32)
            sc0 = sc0 + jnp.where(level_masks[li], s0, 0.0)
            sc1 = sc1 + jnp.where(level_masks[li], s1, 0.0)
        o = jnp.where(h0,
                      jnp.dot(sc0.astype(BF16), vb, preferred_element_type=F32),
                      jnp.dot(sc1.astype(BF16), vb, preferred_element_type=F32))

        nd = ch // HGRN_DIAG
        for blk in range(nd):
            t0 = blk * HGRN_DIAG
            qb = q[t0:t0 + HGRN_DIAG]
            bb = b[t0:t0 + HGRN_DIAG]
            for j in range(HGRN_DIAG):
                kj = k[t0 + j:t0 + j + 1]
                bj = b[t0 + j:t0 + j + 1]
                pj = jnp.where(row8 >= j, qb * kj * jnp.exp(jnp.minimum(bb - bj, 0.0)), 0.0)
                p_sc[(t0 + j) * HGRN_DIAG:(t0 + j + 1) * HGRN_DIAG, :] = pj
        w = jnp.dot(p_sc[...].astype(BF16), seg, preferred_element_type=F32)
        diag_rows = []
        for blk in range(nd):
            t0 = blk * HGRN_DIAG
            od = jnp.zeros((HGRN_DIAG, LANES), F32)
            for j in range(HGRN_DIAG):
                od = od + w[(t0 + j) * HGRN_DIAG:(t0 + j + 1) * HGRN_DIAG] * v[t0 + j:t0 + j + 1]
            diag_rows.append(od)
        o = o + jnp.concatenate(diag_rows, axis=0)

        o = o + jnp.dot((q * jnp.exp(b)).astype(BF16), state.astype(BF16), preferred_element_type=F32)

        k_dec = (k * jnp.exp(b_last - b)).T.astype(BF16)
        upd = jnp.dot(k_dec, vb, preferred_element_type=F32)
        total = jnp.dot(g.T, ones, precision=HIGHEST, preferred_element_type=F32)
        state = jnp.exp(total) * state + jnp.where(same_head, upd, 0.0)

        o2 = o * o
        ms0 = jnp.sum(jnp.where(h0, o2, 0.0), axis=-1, keepdims=True)
        ms1 = jnp.sum(jnp.where(h0, 0.0, o2), axis=-1, keepdims=True)
        ms = jnp.where(h0, ms0, ms1) * (1.0 / HEAD_DIM)
        out = o * lax.rsqrt(ms + EPS) * gn_ref[0] * _silu(hg_ref[0, sl, :])
        o_ref[0, sl, :] = out.astype(BF16)
        return state

    lax.fori_loop(0, seq // ch, chunk, jnp.zeros((LANES, LANES), F32))


def _hgrn_call(hg4, lb, gn, n_pairs):
    bsz, seq, _ = hg4.shape

    def spec(off):
        return pl.BlockSpec((1, seq, LANES), lambda b, p: (b, 0, off + p))

    return pl.pallas_call(
        _hgrn_kernel,
        out_shape=jax.ShapeDtypeStruct((bsz, seq, n_pairs * LANES), BF16),
        grid=(bsz, n_pairs),
        in_specs=[spec(0), spec(n_pairs), spec(2 * n_pairs), spec(3 * n_pairs),
                  pl.BlockSpec((1, 1, LANES), lambda b, p: (p, 0, 0)),
                  pl.BlockSpec((1, 1, LANES), lambda b, p: (p, 0, 0))],
        out_specs=pl.BlockSpec((1, seq, LANES), lambda b, p: (b, 0, p)),
        scratch_shapes=[pltpu.VMEM((HGRN_CHUNK * HGRN_DIAG, LANES), F32)],
        compiler_params=_cparams(("arbitrary", "arbitrary")),
        name="hgrn2",
    )(hg4, hg4, hg4, hg4, lb, gn)


def _pool_kernel(u_ref, w_ref, s_ref, o_ref):
    u = u_ref[0]
    seq, n = u.shape
    t = _iota((seq, 1), 0)
    lane = _iota((1, n), 1)

    def shifted(x, k):
        return jnp.where(t >= k, pltpu.roll(x, k, axis=0), 0.0)

    sums = []
    s = u
    for w in POOL_WINDOWS:
        s = s + shifted(s, w // 2)
        sums.append(s)
    pos1 = (t + 1).astype(F32)
    group = len(POOL_WINDOWS) - 1
    mean = sums[group] / jnp.minimum(pos1, float(POOL_WINDOWS[group]))
    group_dim = n // len(POOL_WINDOWS)
    for gi in range(group - 1, -1, -1):
        mean = jnp.where(lane < (gi + 1) * group_dim,
                         sums[gi] / jnp.minimum(pos1, float(POOL_WINDOWS[gi])), mean)
    pooled = (mean - u).astype(BF16)
    o_ref[0] = (jnp.dot(pooled, w_ref[...], preferred_element_type=F32) * s_ref[...]).astype(BF16)


def _pool_call(pu, w_bd, scale):
    bsz, seq, n = pu.shape
    return pl.pallas_call(
        _pool_kernel,
        out_shape=jax.ShapeDtypeStruct((bsz, seq, n), BF16),
        grid=(bsz,),
        in_specs=[pl.BlockSpec((1, seq, n), lambda b: (b, 0, 0)),
                  pl.BlockSpec((n, n), lambda b: (0, 0)),
                  pl.BlockSpec((1, n), lambda b: (0, 0))],
        out_specs=pl.BlockSpec((1, seq, n), lambda b: (b, 0, 0)),
        compiler_params=_cparams(("arbitrary",)),
        name="multiscale_pool",
    )(pu, w_bd, scale)


def _route(logits):
    lane = _iota(logits.shape, 1)
    lane_f = lane.astype(F32)
    big = float(LANES)
    is_g = lane < N_GROUPS
    gl = jnp.where(is_g, logits, -jnp.inf)
    gmax = jnp.max(gl, axis=-1, keepdims=True)
    gsum = jnp.sum(jnp.where(is_g, jnp.exp(gl - gmax), 0.0), axis=-1, keepdims=True)
    p_group = 1.0 / gsum
    g_sel = jnp.min(jnp.where(gl == gmax, lane_f, big), axis=-1, keepdims=True)
    lo = ROUTER_EXPERT_LANE0 + EXPERTS_PER_GROUP * g_sel
    in_group = (lane_f >= lo) & (lane_f < lo + EXPERTS_PER_GROUP)
    el = jnp.where(in_group, logits, -jnp.inf)
    m1 = jnp.max(el, axis=-1, keepdims=True)
    i1 = jnp.min(jnp.where(el == m1, lane_f, big), axis=-1, keepdims=True)
    el2 = jnp.where(lane_f == i1, -jnp.inf, el)
    m2 = jnp.max(el2, axis=-1, keepdims=True)
    i2 = jnp.min(jnp.where(el2 == m2, lane_f, big), axis=-1, keepdims=True)
    e2 = jnp.exp(m2 - m1)
    w1 = p_group / (1.0 + e2)
    w2 = p_group * e2 / (1.0 + e2)
    return jnp.where(lane_f == i1, w1, 0.0) + jnp.where(lane_f == i2, w2, 0.0)


def _outproj_kernel(of_ref, oh_ref, op_ref, x_ref, w_ref, mod_ref, g_ref, wr_ref, br_ref,
                    xn_ref, h2_ref, cw_ref, *, n_fox, n_hgrn):
    y = jnp.dot(of_ref[0], w_ref[0:n_fox, :], preferred_element_type=F32)
    y = y + jnp.dot(oh_ref[0], w_ref[n_fox:n_fox + n_hgrn, :], preferred_element_type=F32)
    y = y + jnp.dot(op_ref[0], w_ref[n_fox + n_hgrn:, :], preferred_element_type=F32)
    xn = x_ref[0] + (1.0 + mod_ref[0, 2:3, :]) * y
    xn_ref[0] = xn
    h2 = _norm_mod(xn, g_ref[...], mod_ref[0, 3:4, :], mod_ref[0, 4:5, :])
    h2_ref[0] = h2.astype(BF16)
    logits = jnp.dot(h2, wr_ref[...], precision=HIGHEST, preferred_element_type=F32) + br_ref[...]
    cw_ref[0] = _route(logits)


def _outproj_call(o_fox, o_hgrn, o_pool, x, w_out, mod, g2, w_router, b_router, tm):
    bsz, seq, d = x.shape
    n_fox, n_hgrn, n_pool = o_fox.shape[-1], o_hgrn.shape[-1], o_pool.shape[-1]
    return pl.pallas_call(
        functools.partial(_outproj_kernel, n_fox=n_fox, n_hgrn=n_hgrn),
        out_shape=(jax.ShapeDtypeStruct((bsz, seq, d), F32),
                   jax.ShapeDtypeStruct((bsz, seq, d), BF16),
                   jax.ShapeDtypeStruct((bsz, seq, LANES), F32)),
        grid=(bsz, seq // tm),
        in_specs=[
            pl.BlockSpec((1, tm, n_fox), lambda b, i: (b, i, 0)),
            pl.BlockSpec((1, tm, n_hgrn), lambda b, i: (b, i, 0)),
            pl.BlockSpec((1, tm, n_pool), lambda b, i: (b, i, 0)),
            pl.BlockSpec((1, tm, d), lambda b, i: (b, i, 0)),
            pl.BlockSpec((d, d), lambda b, i: (0, 0)),
            pl.BlockSpec((1, N_MOD, d), lambda b, i: (b, 0, 0)),
            pl.BlockSpec((1, d), lambda b, i: (0, 0)),
            pl.BlockSpec((d, LANES), lambda b, i: (0, 0)),
            pl.BlockSpec((1, LANES), lambda b, i: (0, 0)),
        ],
        out_specs=(pl.BlockSpec((1, tm, d), lambda b, i: (b, i, 0)),
                   pl.BlockSpec((1, tm, d), lambda b, i: (b, i, 0)),
                   pl.BlockSpec((1, tm, LANES), lambda b, i: (b, i, 0))),
        compiler_params=_cparams(("arbitrary", "arbitrary")),
        name="outproj_router",
    )(o_fox, o_hgrn, o_pool, x, w_out, mod, g2, w_router, b_router)


def _moe_kernel(h_ref, cw_ref, x_ref, mod_ref, wg_ref, wu_ref, wd_ref, o_ref, acc_ref):
    e = pl.program_id(2)

    @pl.when(e == 0)
    def _():
        acc_ref[...] = jnp.zeros_like(acc_ref)

    h = h_ref[0]
    hidden = _silu(jnp.dot(h, wg_ref[0], preferred_element_type=F32)) * jnp.dot(
        h, wu_ref[0], preferred_element_type=F32)
    y = jnp.dot(hidden.astype(BF16), wd_ref[0], preferred_element_type=F32)
    lane = _iota((1, LANES), 1)
    cw = jnp.sum(jnp.where(lane == ROUTER_EXPERT_LANE0 + e, cw_ref[0], 0.0), axis=-1, keepdims=True)
    acc_ref[...] += cw * y

    @pl.when(e == pl.num_programs(2) - 1)
    def _():
        o_ref[0] = x_ref[0] + (1.0 + mod_ref[0, 5:6, :]) * acc_ref[...]


def _moe_call(h2, cw, x, mod, wg, wu, wd, tm):
    bsz, seq, d = x.shape
    n_e, _, f = wg.shape
    return pl.pallas_call(
        _moe_kernel,
        out_shape=jax.ShapeDtypeStruct((bsz, seq, d), F32),
        grid=(bsz, seq // tm, n_e),
        in_specs=[
            pl.BlockSpec((1, tm, d), lambda b, i, e: (b, i, 0)),
            pl.BlockSpec((1, tm, LANES), lambda b, i, e: (b, i, 0)),
            pl.BlockSpec((1, tm, d), lambda b, i, e: (b, i, 0)),
            pl.BlockSpec((1, N_MOD, d), lambda b, i, e: (b, 0, 0)),
            pl.BlockSpec((1, d, f), lambda b, i, e: (e, 0, 0)),
            pl.BlockSpec((1, d, f), lambda b, i, e: (e, 0, 0)),
            pl.BlockSpec((1, f, d), lambda b, i, e: (e, 0, 0)),
        ],
        out_specs=pl.BlockSpec((1, tm, d), lambda b, i, e: (b, i, 0)),
        scratch_shapes=[pltpu.VMEM((tm, d), F32)],
        compiler_params=_cparams(("arbitrary", "arbitrary", "arbitrary")),
        name="moe_experts",
    )(h2, cw, x, mod, wg, wu, wd)


def _final_kernel(x_ref, g_ref, o_ref):
    x = x_ref[0]
    ms = jnp.mean(x * x, axis=-1, keepdims=True)
    o_ref[0] = x * lax.rsqrt(ms + EPS) * g_ref[...]


def _final_call(x, g, tm):
    bsz, seq, d = x.shape
    return pl.pallas_call(
        _final_kernel,
        out_shape=jax.ShapeDtypeStruct(x.shape, F32),
        grid=(bsz, seq // tm),
        in_specs=[pl.BlockSpec((1, tm, d), lambda b, i: (b, i, 0)),
                  pl.BlockSpec((1, d), lambda b, i: (0, 0))],
        out_specs=pl.BlockSpec((1, tm, d), lambda b, i: (b, i, 0)),
        compiler_params=_cparams(("arbitrary", "arbitrary")),
        name="final_norm",
    )(x, g)


def _pad_lanes(a, n=LANES):
    return jnp.pad(a, [(0, 0)] * (a.ndim - 1) + [(0, n - a.shape[-1])])


def kernel(x, c, w_ada, b_ada, norm1_g, w_in, fox_f_bias, fox_norm_g, hgrn_lb_logits, hgrn_norm_g, pool_w, pool_scale, w_out, norm2_g, router_group_w, router_group_b, router_expert_w, router_expert_b, expert_w_gate, expert_w_up, expert_w_down, final_norm_g):
    bsz, seq, d = x.shape
    depth = w_ada.shape[0]
    fox_heads = fox_f_bias.shape[1]
    fox_dim = fox_heads * HEAD_DIM
    hgrn_dim = hgrn_lb_logits.shape[1]
    pool_dim = pool_scale.shape[1]
    n_fox_pairs = fox_dim // LANES
    n_hgrn_pairs = hgrn_dim // LANES
    tm = min(512, seq)
    tq = min(256, seq)

    o_ff = 3 * fox_dim
    w_in_r = jnp.concatenate(
        [w_in[:, :, :o_ff], w_in[:, :, o_ff + fox_heads:], _pad_lanes(w_in[:, :, o_ff:o_ff + fox_heads])],
        axis=-1).astype(BF16)
    f_bias = _pad_lanes(fox_f_bias)
    groups, gd, _ = pool_w.shape[1:]
    pool_bd = jnp.einsum('lgcd,gh->lgchd', pool_w, jnp.eye(groups, dtype=pool_w.dtype)).reshape(
        depth, pool_dim, pool_dim).astype(BF16)
    w_router = _pad_lanes(jnp.concatenate([router_group_w, router_expert_w], axis=-1))
    b_router = _pad_lanes(jnp.concatenate([router_group_b, router_expert_b], axis=-1))
    w_out_b = w_out.astype(BF16)
    wg_b = expert_w_gate.astype(BF16)
    wu_b = expert_w_up.astype(BF16)
    wd_b = expert_w_down.astype(BF16)

    mod_all = _ada_call(c, w_ada, b_ada).reshape(depth, bsz, N_MOD, d)
    lower = _lb_call(hgrn_lb_logits)

    for l in range(depth):
        mod = mod_all[l]
        qkv, hg4, pu, ff = _inproj_call(x, norm1_g[l:l + 1], mod, w_in_r[l], 3 * fox_dim, 4 * hgrn_dim,
                                        pool_dim, tm)
        fcol, frow = _fbias_call(ff, f_bias[l:l + 1])
        o_fox = _fox_call(qkv, fcol, frow, fox_norm_g[l].reshape(n_fox_pairs, 1, LANES), n_fox_pairs, tq)
        o_hgrn = _hgrn_call(hg4, lower[l].reshape(n_hgrn_pairs, 1, LANES),
                            hgrn_norm_g[l].reshape(n_hgrn_pairs, 1, LANES), n_hgrn_pairs)
        o_pool = _pool_call(pu, pool_bd[l], pool_scale[l:l + 1])
        x, h2, cw = _outproj_call(o_fox, o_hgrn, o_pool, x, w_out_b[l], mod, norm2_g[l:l + 1],
                                  w_router[l], b_router[l:l + 1], tm)
        x = _moe_call(h2, cw, x, mod, wg_b[l], wu_b[l], wd_b[l], tm)
    return _final_call(x, final_norm_g.reshape(1, d), tm)
```

```python
import functools

import jax
import jax.numpy as jnp
from jax import lax
from jax.experimental import pallas as pl
from jax.experimental.pallas import tpu as pltpu

F32 = jnp.float32
BF16 = jnp.bfloat16
HIGHEST = lax.Precision.HIGHEST

HEAD_DIM = 64
LANES = 128
EPS = 1e-6
N_MOD = 6
POOL_WINDOWS = (2, 4, 8, 16)
N_GROUPS = 4
EXPERTS_PER_GROUP = 4
N_EXPERTS = N_GROUPS * EXPERTS_PER_GROUP
ROUTER_EXPERT_LANE0 = N_GROUPS
N_PAIRS = 6
N_CLASSES = N_GROUPS * N_PAIRS
ROUTE_CLASS_LANE0 = 32
MOE_TILE_ROWS = 256
HGRN_CHUNK = 128
HGRN_LEVELS = (64, 32, 16, 8)
HGRN_DIAG = 8
VMEM_LIMIT = 48 * 1024 * 1024


def _cparams(sem):
    return pltpu.CompilerParams(dimension_semantics=sem, vmem_limit_bytes=VMEM_LIMIT)


def _silu(x):
    return x * jax.nn.sigmoid(x)


def _iota(shape, dim):
    return lax.broadcasted_iota(jnp.int32, shape, dim)


def _ada_kernel(c_ref, w_ref, b_ref, o_ref):
    sc = _silu(c_ref[...]).astype(BF16)
    o_ref[0] = jnp.dot(sc, w_ref[0].astype(BF16), preferred_element_type=F32) + b_ref[0]


def _ada_call(c, w_ada, b_ada):
    depth, d, n = w_ada.shape
    bsz = c.shape[0]
    tn = 1536
    return pl.pallas_call(
        _ada_kernel,
        out_shape=jax.ShapeDtypeStruct((depth, bsz, n), F32),
        grid=(depth, n // tn),
        in_specs=[
            pl.BlockSpec((bsz, d), lambda l, j: (0, 0)),
            pl.BlockSpec((1, d, tn), lambda l, j: (l, 0, j)),
            pl.BlockSpec((1, 1, tn), lambda l, j: (l, 0, j)),
        ],
        out_specs=pl.BlockSpec((1, bsz, tn), lambda l, j: (l, 0, j)),
        compiler_params=_cparams(("arbitrary", "arbitrary")),
        name="ada_mod",
    )(c, w_ada, b_ada.reshape(depth, 1, n))


def _lb_kernel(x_ref, o_ref):
    x = x_ref[...]
    depth = x.shape[0]
    e = jnp.exp(x - jnp.max(x, axis=0, keepdims=True))
    p = e / jnp.sum(e, axis=0, keepdims=True)
    acc = jnp.zeros_like(p[0:1])
    o_ref[0:1, :] = acc
    for l in range(1, depth):
        acc = acc + p[l:l + 1]
        o_ref[l:l + 1, :] = acc


def _lb_call(lb_logits):
    return pl.pallas_call(
        _lb_kernel,
        out_shape=jax.ShapeDtypeStruct(lb_logits.shape, F32),
        name="hgrn_lower_bounds",
    )(lb_logits)


def _norm_mod(x, g, shift, scale):
    ms = jnp.mean(x * x, axis=-1, keepdims=True)
    return (x * lax.rsqrt(ms + EPS) * g) * (1.0 + scale) + shift


def _inproj_kernel(x_ref, g_ref, mod_ref, w_ref, qkv_ref, hg_ref, pu_ref, ff_ref, *, n_qkv, n_hg, n_pu):
    h = _norm_mod(x_ref[0], g_ref[...], mod_ref[0, 0:1, :], mod_ref[0, 1:2, :]).astype(BF16)
    o0, o1, o2 = n_qkv, n_qkv + n_hg, n_qkv + n_hg + n_pu
    qkv_ref[0] = jnp.dot(h, w_ref[:, 0:o0], preferred_element_type=F32).astype(BF16)
    hg_ref[0] = jnp.dot(h, w_ref[:, o0:o1], preferred_element_type=F32)
    pu_ref[0] = jnp.dot(h, w_ref[:, o1:o2], preferred_element_type=F32)
    ff_ref[0] = jnp.dot(h, w_ref[:, o2:], preferred_element_type=F32)


def _inproj_call(x, g, mod, w, n_qkv, n_hg, n_pu, tm):
    bsz, seq, d = x.shape
    n_all = w.shape[1]
    n_ff = n_all - n_qkv - n_hg - n_pu
    return pl.pallas_call(
        functools.partial(_inproj_kernel, n_qkv=n_qkv, n_hg=n_hg, n_pu=n_pu),
        out_shape=(
            jax.ShapeDtypeStruct((bsz, seq, n_qkv), BF16),
            jax.ShapeDtypeStruct((bsz, seq, n_hg), F32),
            jax.ShapeDtypeStruct((bsz, seq, n_pu), F32),
            jax.ShapeDtypeStruct((bsz, seq, n_ff), F32),
        ),
        grid=(bsz, seq // tm),
        in_specs=[
            pl.BlockSpec((1, tm, d), lambda b, i: (b, i, 0)),
            pl.BlockSpec((1, d), lambda b, i: (0, 0)),
            pl.BlockSpec((1, N_MOD, d), lambda b, i: (b, 0, 0)),
            pl.BlockSpec((d, n_all), lambda b, i: (0, 0)),
        ],
        out_specs=(
            pl.BlockSpec((1, tm, n_qkv), lambda b, i: (b, i, 0)),
            pl.BlockSpec((1, tm, n_hg), lambda b, i: (b, i, 0)),
            pl.BlockSpec((1, tm, n_pu), lambda b, i: (b, i, 0)),
            pl.BlockSpec((1, tm, n_ff), lambda b, i: (b, i, 0)),
        ),
        compiler_params=_cparams(("arbitrary", "arbitrary")),
        name="norm_inproj",
    )(x, g, mod, w)


def _fbias_kernel(ff_ref, bias_ref, fcol_ref, frow_ref, *, cb):
    seq = ff_ref.shape[1]
    tri = (_iota((cb, cb), 0) >= _iota((cb, cb), 1)).astype(F32)
    carry = jnp.zeros((1, LANES), F32)
    for blk in range(seq // cb):
        x = ff_ref[0, blk * cb:(blk + 1) * cb, :] + bias_ref[...]
        log_f = jnp.minimum(x, 0.0) - jnp.log1p(jnp.exp(-jnp.abs(x)))
        cs = jnp.dot(tri, log_f, precision=HIGHEST, preferred_element_type=F32) + carry
        fcol_ref[0, blk * cb:(blk + 1) * cb, :] = cs
        frow_ref[0, :, blk * cb:(blk + 1) * cb] = cs.T[0:8, :]
        carry = cs[cb - 1:cb, :]


def _fbias_call(ff, bias):
    bsz, seq, _ = ff.shape
    cb = min(256, seq)
    return pl.pallas_call(
        functools.partial(_fbias_kernel, cb=cb),
        out_shape=(jax.ShapeDtypeStruct((bsz, seq, LANES), F32),
                   jax.ShapeDtypeStruct((bsz, 8, seq), F32)),
        grid=(bsz,),
        in_specs=[pl.BlockSpec((1, seq, LANES), lambda b: (b, 0, 0)),
                  pl.BlockSpec((1, LANES), lambda b: (0, 0))],
        out_specs=(pl.BlockSpec((1, seq, LANES), lambda b: (b, 0, 0)),
                   pl.BlockSpec((1, 8, seq), lambda b: (b, 0, 0))),
        compiler_params=_cparams(("arbitrary",)),
        name="fox_forget_bias",
    )(ff, bias)


def _fox_kernel(q_ref, k_ref, v_ref, fc_ref, fr_ref, gn_ref, o_ref, *, tq):
    p = pl.program_id(1)
    i = pl.program_id(2)
    lane = _iota((1, LANES), 1)
    h0 = lane < HEAD_DIM
    q = q_ref[0] * jnp.asarray(HEAD_DIM ** -0.5, BF16)
    zero = jnp.zeros_like(q)
    q0 = jnp.where(h0, q, zero)
    q1 = jnp.where(h0, zero, q)
    fc = fc_ref[0]
    f0 = jnp.sum(jnp.where(lane == 2 * p, fc, 0.0), axis=-1, keepdims=True)
    f1 = jnp.sum(jnp.where(lane == 2 * p + 1, fc, 0.0), axis=-1, keepdims=True)
    nt = (((1,), (1,)), ((), ()))

    def step(j, carry, masked):
        m0, l0, m1, l1, acc = carry
        start = pl.multiple_of(j * tq, tq)
        ks = k_ref[0, pl.ds(start, tq), :]
        vs = v_ref[0, pl.ds(start, tq), :]
        fr0 = fr_ref[0, pl.ds(2 * p, 1), pl.ds(start, tq)]
        fr1 = fr_ref[0, pl.ds(2 * p + 1, 1), pl.ds(start, tq)]
        s0 = lax.dot_general(q0, ks, nt, preferred_element_type=F32) + (f0 - fr0)
        s1 = lax.dot_general(q1, ks, nt, preferred_element_type=F32) + (f1 - fr1)
        if masked:
            causal = _iota((tq, tq), 0) >= _iota((tq, tq), 1)
            s0 = jnp.where(causal, s0, -jnp.inf)
            s1 = jnp.where(causal, s1, -jnp.inf)
        n0 = jnp.maximum(m0, jnp.max(s0, axis=-1, keepdims=True))
        n1 = jnp.maximum(m1, jnp.max(s1, axis=-1, keepdims=True))
        p0 = jnp.exp(s0 - n0)
        p1 = jnp.exp(s1 - n1)
        a0 = jnp.exp(m0 - n0)
        a1 = jnp.exp(m1 - n1)
        l0 = a0 * l0 + jnp.sum(p0, axis=-1, keepdims=True)
        l1 = a1 * l1 + jnp.sum(p1, axis=-1, keepdims=True)
        pv0 = jnp.dot(p0.astype(BF16), vs, preferred_element_type=F32)
        pv1 = jnp.dot(p1.astype(BF16), vs, preferred_element_type=F32)
        acc = jnp.where(h0, a0, a1) * acc + jnp.where(h0, pv0, pv1)
        return n0, l0, n1, l1, acc

    neg = jnp.full((tq, 1), -1e30, F32)
    zcol = jnp.zeros((tq, 1), F32)
    init = (neg, zcol, neg, zcol, jnp.zeros((tq, LANES), F32))
    carry = lax.fori_loop(0, i, functools.partial(step, masked=False), init)
    _, l0, _, l1, acc = step(i, carry, True)
    o = acc / jnp.where(h0, l0, l1)
    o2 = o * o
    ms0 = jnp.sum(jnp.where(h0, o2, 0.0), axis=-1, keepdims=True)
    ms1 = jnp.sum(jnp.where(h0, 0.0, o2), axis=-1, keepdims=True)
    ms = jnp.where(h0, ms0, ms1) * (1.0 / HEAD_DIM)
    o_ref[0] = (o * lax.rsqrt(ms + EPS) * gn_ref[0]).astype(BF16)


def _fox_call(qkv, fcol, frow, gn, n_pairs, tq):
    bsz, seq, _ = qkv.shape
    return pl.pallas_call(
        functools.partial(_fox_kernel, tq=tq),
        out_shape=jax.ShapeDtypeStruct((bsz, seq, n_pairs * LANES), BF16),
        grid=(bsz, n_pairs, seq // tq),
        in_specs=[
            pl.BlockSpec((1, tq, LANES), lambda b, p, i: (b, i, p)),
            pl.BlockSpec((1, seq, LANES), lambda b, p, i: (b, 0, n_pairs + p)),
            pl.BlockSpec((1, seq, LANES), lambda b, p, i: (b, 0, 2 * n_pairs + p)),
            pl.BlockSpec((1, tq, LANES), lambda b, p, i: (b, i, 0)),
            pl.BlockSpec((1, 8, seq), lambda b, p, i: (b, 0, 0)),
            pl.BlockSpec((1, 1, LANES), lambda b, p, i: (p, 0, 0)),
        ],
        out_specs=pl.BlockSpec((1, tq, LANES), lambda b, p, i: (b, i, p)),
        compiler_params=_cparams(("arbitrary", "arbitrary", "arbitrary")),
        name="fox_attention",
    )(qkv, qkv, qkv, fcol, frow, gn)


def _hgrn_kernel(hq_ref, hf_ref, hi_ref, hg_ref, lb_ref, gn_ref, o_ref, p_sc):
    ch = HGRN_CHUNK
    seq = hq_ref.shape[1]
    lane = _iota((1, LANES), 1)
    h0 = lane < HEAD_DIM
    r = _iota((ch, ch), 0)
    c = _iota((ch, ch), 1)
    tri = (r >= c).astype(F32)
    same_head = (r < HEAD_DIM) == (c < HEAD_DIM)
    seg = same_head.astype(BF16)
    ones = jnp.ones((ch, LANES), F32)
    gathers = []
    level_masks = []
    for m in HGRN_LEVELS:
        blk_r = r // (2 * m)
        gathers.append((c == blk_r * (2 * m) + (m - 1)).astype(F32))
        level_masks.append((blk_r == c // (2 * m)) & (r % (2 * m) >= m) & (c % (2 * m) < m))
    gather = jnp.concatenate(gathers, axis=0)
    row8 = _iota((HGRN_DIAG, LANES), 0)
    lb = lb_ref[0]
    nt = (((1,), (1,)), ((), ()))

    def chunk(ci, state):
        sl = pl.ds(pl.multiple_of(ci * ch, ch), ch)
        f = lb + (1.0 - lb) * jax.nn.sigmoid(hf_ref[0, sl, :])
        g = jnp.log(f)
        k = 1.0 - f
        q = _silu(hq_ref[0, sl, :])
        v = hi_ref[0, sl, :]
        vb = v.astype(BF16)
        b = jnp.dot(tri, g, precision=HIGHEST, preferred_element_type=F32)
        b_last = b[ch - 1:ch, :]
        refs = jnp.dot(gather, b, precision=HIGHEST, preferred_element_type=F32)

        sc0 = jnp.zeros((ch, ch), F32)
        sc1 = jnp.zeros((ch, ch), F32)
        for li in range(len(HGRN_LEVELS)):
            ref_l = refs[li * ch:(li + 1) * ch]
            qt = q * jnp.exp(jnp.minimum(b - ref_l, 0.0))
            kt = (k * jnp.exp(jnp.minimum(ref_l - b, 0.0))).astype(BF16)
            s0 = lax.dot_general(jnp.where(h0, qt, 0.0).astype(BF16), kt, nt, preferred_element_type=F32)
            s1 = lax.dot_general(jnp.where(h0, 0.0, qt).astype(BF16), kt, nt, preferred_element_type=F32)
            sc0 = sc0 + jnp.where(level_masks[li], s0, 0.0)
            sc1 = sc1 + jnp.where(level_masks[li], s1, 0.0)
        o = jnp.where(h0,
                      jnp.dot(sc0.astype(BF16), vb, preferred_element_type=F32),
                      jnp.dot(sc1.astype(BF16), vb, preferred_element_type=F32))

        nd = ch // HGRN_DIAG
        for blk in range(nd):
            t0 = blk * HGRN_DIAG
            qb = q[t0:t0 + HGRN_DIAG]
            bb = b[t0:t0 + HGRN_DIAG]
            for j in range(HGRN_DIAG):
                kj = k[t0 + j:t0 + j + 1]
                bj = b[t0 + j:t0 + j + 1]
                pj = jnp.where(row8 >= j, qb * kj * jnp.exp(jnp.minimum(bb - bj, 0.0)), 0.0)
                p_sc[(t0 + j) * HGRN_DIAG:(t0 + j + 1) * HGRN_DIAG, :] = pj
        w = jnp.dot(p_sc[...].astype(BF16), seg, preferred_element_type=F32)
        diag_rows = []
        for blk in range(nd):
            t0 = blk * HGRN_DIAG
            od = jnp.zeros((HGRN_DIAG, LANES), F32)
            for j in range(HGRN_DIAG):
                od = od + w[(t0 + j) * HGRN_DIAG:(t0 + j + 1) * HGRN_DIAG] * v[t0 + j:t0 + j + 1]
            diag_rows.append(od)
        o = o + jnp.concatenate(diag_rows, axis=0)

        o = o + jnp.dot((q * jnp.exp(b)).astype(BF16), state.astype(BF16), preferred_element_type=F32)

        k_dec = (k * jnp.exp(b_last - b)).T.astype(BF16)
        upd = jnp.dot(k_dec, vb, preferred_element_type=F32)
        total = jnp.dot(g.T, ones, precision=HIGHEST, preferred_element_type=F32)
        state = jnp.exp(total) * state + jnp.where(same_head, upd, 0.0)

        o2 = o * o
        ms0 = jnp.sum(jnp.where(h0, o2, 0.0), axis=-1, keepdims=True)
        ms1 = jnp.sum(jnp.where(h0, 0.0, o2), axis=-1, keepdims=True)
        ms = jnp.where(h0, ms0, ms1) * (1.0 / HEAD_DIM)
        out = o * lax.rsqrt(ms + EPS) * gn_ref[0] * _silu(hg_ref[0, sl, :])
        o_ref[0, sl, :] = out.astype(BF16)
        return state

    lax.fori_loop(0, seq // ch, chunk, jnp.zeros((LANES, LANES), F32))


def _hgrn_call(hg4, lb, gn, n_pairs):
    bsz, seq, _ = hg4.shape

    def spec(off):
        return pl.BlockSpec((1, seq, LANES), lambda b, p: (b, 0, off + p))

    return pl.pallas_call(
        _hgrn_kernel,
        out_shape=jax.ShapeDtypeStruct((bsz, seq, n_pairs * LANES), BF16),
        grid=(bsz, n_pairs),
        in_specs=[spec(0), spec(n_pairs), spec(2 * n_pairs), spec(3 * n_pairs),
                  pl.BlockSpec((1, 1, LANES), lambda b, p: (p, 0, 0)),
                  pl.BlockSpec((1, 1, LANES), lambda b, p: (p, 0, 0))],
        out_specs=pl.BlockSpec((1, seq, LANES), lambda b, p: (b, 0, p)),
        scratch_shapes=[pltpu.VMEM((HGRN_CHUNK * HGRN_DIAG, LANES), F32)],
        compiler_params=_cparams(("arbitrary", "arbitrary")),
        name="hgrn2",
    )(hg4, hg4, hg4, hg4, lb, gn)


def _pool_kernel(u_ref, w_ref, s_ref, o_ref):
    u = u_ref[0]
    seq, n = u.shape
    t = _iota((seq, 1), 0)
    lane = _iota((1, n), 1)

    def shifted(x, k):
        return jnp.where(t >= k, pltpu.roll(x, k, axis=0), 0.0)

    sums = []
    s = u
    for w in POOL_WINDOWS:
        s = s + shifted(s, w // 2)
        sums.append(s)
    pos1 = (t + 1).astype(F32)
    group = len(POOL_WINDOWS) - 1
    mean = sums[group] / jnp.minimum(pos1, float(POOL_WINDOWS[group]))
    group_dim = n // len(POOL_WINDOWS)
    for gi in range(group - 1, -1, -1):
        mean = jnp.where(lane < (gi + 1) * group_dim,
                         sums[gi] / jnp.minimum(pos1, float(POOL_WINDOWS[gi])), mean)
    pooled = (mean - u).astype(BF16)
    o_ref[0] = (jnp.dot(pooled, w_ref[...], preferred_element_type=F32) * s_ref[...]).astype(BF16)


def _pool_call(pu, w_bd, scale):
    bsz, seq, n = pu.shape
    return pl.pallas_call(
        _pool_kernel,
        out_shape=jax.ShapeDtypeStruct((bsz, seq, n), BF16),
        grid=(bsz,),
        in_specs=[pl.BlockSpec((1, seq, n), lambda b: (b, 0, 0)),
                  pl.BlockSpec((n, n), lambda b: (0, 0)),
                  pl.BlockSpec((1, n), lambda b: (0, 0))],
        out_specs=pl.BlockSpec((1, seq, n), lambda b: (b, 0, 0)),
        compiler_params=_cparams(("arbitrary",)),
        name="multiscale_pool",
    )(pu, w_bd, scale)


def _route(logits):
    lane = _iota(logits.shape, 1)
    lane_f = lane.astype(F32)
    big = float(LANES)
    is_g = lane < N_GROUPS
    gl = jnp.where(is_g, logits, -jnp.inf)
    gmax = jnp.max(gl, axis=-1, keepdims=True)
    gsum = jnp.sum(jnp.where(is_g, jnp.exp(gl - gmax), 0.0), axis=-1, keepdims=True)
    p_group = 1.0 / gsum
    g_sel = jnp.min(jnp.where(gl == gmax, lane_f, big), axis=-1, keepdims=True)
    lo = ROUTER_EXPERT_LANE0 + EXPERTS_PER_GROUP * g_sel
    in_group = (lane_f >= lo) & (lane_f < lo + EXPERTS_PER_GROUP)
    el = jnp.where(in_group, logits, -jnp.inf)
    m1 = jnp.max(el, axis=-1, keepdims=True)
    i1 = jnp.min(jnp.where(el == m1, lane_f, big), axis=-1, keepdims=True)
    el2 = jnp.where(lane_f == i1, -jnp.inf, el)
    m2 = jnp.max(el2, axis=-1, keepdims=True)
    i2 = jnp.min(jnp.where(el2 == m2, lane_f, big), axis=-1, keepdims=True)
    e2 = jnp.exp(m2 - m1)
    w1 = p_group / (1.0 + e2)
    w2 = p_group * e2 / (1.0 + e2)
    first_low = i1 < i2
    a = jnp.minimum(i1, i2) - lo
    b = jnp.maximum(i1, i2) - lo
    pair = jnp.where(a == 0.0, b - 1.0, jnp.where(a == 1.0, b + 1.0, float(N_PAIRS - 1)))
    cls_lane = ROUTE_CLASS_LANE0 + g_sel * N_PAIRS + pair
    return (jnp.where(lane == 0, jnp.where(first_low, w1, w2), 0.0)
            + jnp.where(lane == 1, jnp.where(first_low, w2, w1), 0.0)
            + jnp.where(lane_f == cls_lane, 1.0, 0.0))


def _outproj_kernel(of_ref, oh_ref, op_ref, x_ref, w_ref, mod_ref, g_ref, wr_ref, br_ref,
                    xn_ref, h2_ref, *, n_fox, n_hgrn):
    d = x_ref.shape[-1]
    y = jnp.dot(of_ref[...], w_ref[0:n_fox, :], preferred_element_type=F32)
    y = y + jnp.dot(oh_ref[...], w_ref[n_fox:n_fox + n_hgrn, :], preferred_element_type=F32)
    y = y + jnp.dot(op_ref[...], w_ref[n_fox + n_hgrn:, :], preferred_element_type=F32)
    xn = x_ref[...] + (1.0 + mod_ref[0, 2:3, :]) * y
    xn_ref[...] = xn
    h2 = _norm_mod(xn, g_ref[...], mod_ref[0, 3:4, :], mod_ref[0, 4:5, :])
    h2_ref[:, 0:d] = h2
    logits = jnp.dot(h2, wr_ref[...], precision=HIGHEST, preferred_element_type=F32) + br_ref[...]
    h2_ref[:, d:] = _route(logits)


def _outproj_call(o_fox, o_hgrn, o_pool, x, w_out, mod, g2, w_router, b_router, tm, seq):
    t, d = x.shape
    n_fox, n_hgrn, n_pool = o_fox.shape[-1], o_hgrn.shape[-1], o_pool.shape[-1]
    per_seq = seq // tm
    return pl.pallas_call(
        functools.partial(_outproj_kernel, n_fox=n_fox, n_hgrn=n_hgrn),
        out_shape=(jax.ShapeDtypeStruct((t, d), F32),
                   jax.ShapeDtypeStruct((t, d + LANES), F32)),
        grid=(t // tm,),
        in_specs=[
            pl.BlockSpec((tm, n_fox), lambda i: (i, 0)),
            pl.BlockSpec((tm, n_hgrn), lambda i: (i, 0)),
            pl.BlockSpec((tm, n_pool), lambda i: (i, 0)),
            pl.BlockSpec((tm, d), lambda i: (i, 0)),
            pl.BlockSpec((d, d), lambda i: (0, 0)),
            pl.BlockSpec((1, N_MOD, d), lambda i: (i // per_seq, 0, 0)),
            pl.BlockSpec((1, d), lambda i: (0, 0)),
            pl.BlockSpec((d, LANES), lambda i: (0, 0)),
            pl.BlockSpec((1, LANES), lambda i: (0, 0)),
        ],
        out_specs=(pl.BlockSpec((tm, d), lambda i: (i, 0)),
                   pl.BlockSpec((tm, d + LANES), lambda i: (i, 0))),
        compiler_params=_cparams(("arbitrary",)),
        name="outproj_router",
    )(o_fox, o_hgrn, o_pool, x, w_out, mod, g2, w_router, b_router)


def _sort_kernel(route_ref, pos_ref, meta_ref, *, tile_rows, cb):
    t = route_ref.shape[0]
    lane = _iota((1, LANES), 1)
    is_cls = (lane >= ROUTE_CLASS_LANE0) & (lane < ROUTE_CLASS_LANE0 + N_CLASSES)
    nblk = t // cb
    zero_row = jnp.zeros((1, LANES), F32)

    def onehot(i):
        return jnp.where(is_cls, route_ref[pl.ds(pl.multiple_of(i * cb, cb), cb), :], 0.0)

    counts = lax.fori_loop(0, nblk, lambda i, acc: acc + jnp.sum(onehot(i), axis=0, keepdims=True), zero_row)
    padded = jnp.floor((counts + float(tile_rows - 1)) * (1.0 / tile_rows)) * float(tile_rows)
    before = (_iota((LANES, LANES), 0) < _iota((LANES, LANES), 1)).astype(F32)
    offs = jnp.dot(jnp.broadcast_to(padded, (8, LANES)), before, precision=HIGHEST,
                   preferred_element_type=F32)[0:1]
    ends = offs + padded
    strict = (_iota((cb, cb), 0) > _iota((cb, cb), 1)).astype(BF16)
    ones8 = jnp.ones((8, LANES), F32)
    nt = (((1,), (1,)), ((), ()))

    def place(i, seen):
        oh = onehot(i)
        rank = jnp.dot(strict, oh.astype(BF16), preferred_element_type=F32) + seen
        dest = oh * (rank + offs)
        rows = lax.dot_general(ones8, dest, nt, precision=HIGHEST, preferred_element_type=F32)
        pos_ref[:, pl.ds(pl.multiple_of(i * cb, cb), cb)] = rows.astype(jnp.int32)
        return seen + jnp.sum(oh, axis=0, keepdims=True)

    lax.fori_loop(0, nblk, place, zero_row)
    tile_start = _iota((LANES, LANES), 0).astype(F32) * float(tile_rows)
    tile_cls = jnp.sum(jnp.where(is_cls & (ends <= tile_start), 1.0, 0.0), axis=-1, keepdims=True)
    n_used = jnp.sum(jnp.where(lane == ROUTE_CLASS_LANE0 + N_CLASSES - 1, ends, 0.0), axis=-1,
                     keepdims=True) * (1.0 / tile_rows)
    meta_ref[...] = jnp.where(lane == 0, tile_cls, jnp.where(lane == 1, n_used, 0.0))


def _sort_call(h2ext, d, tile_rows):
    t = h2ext.shape[0]
    cb = min(512, t)
    return pl.pallas_call(
        functools.partial(_sort_kernel, tile_rows=tile_rows, cb=cb),
        out_shape=(jax.ShapeDtypeStruct((8, t), jnp.int32),
                   jax.ShapeDtypeStruct((LANES, LANES), F32)),
        grid=(1,),
        in_specs=[pl.BlockSpec((t, LANES), lambda i: (0, d // LANES))],
        out_specs=(pl.BlockSpec((8, t), lambda i: (0, 0)),
                   pl.BlockSpec((LANES, LANES), lambda i: (0, 0))),
        compiler_params=_cparams(("arbitrary",)),
        name="route_sort",
    )(h2ext)


def _dispatch_kernel(pos_ref, h_ref, init_ref, out_ref, sem, *, tm):
    del init_ref
    base = pl.program_id(0) * tm

    def issue(r, carry):
        dst = pos_ref[base + r]
        pltpu.make_async_copy(h_ref.at[pl.ds(r, 1), :], out_ref.at[pl.ds(dst, 1), :], sem).start()
        return carry

    lax.fori_loop(0, tm, issue, 0, unroll=8)
    pltpu.make_async_copy(h_ref, out_ref.at[pl.ds(0, tm), :], sem).wait()


def _dispatch_call(pos, h2ext, n_rows, tm):
    t, w = h2ext.shape
    return pl.pallas_call(
        functools.partial(_dispatch_kernel, tm=tm),
        out_shape=jax.ShapeDtypeStruct((n_rows, w), F32),
        grid_spec=pltpu.PrefetchScalarGridSpec(
            num_scalar_prefetch=1,
            grid=(t // tm,),
            in_specs=[pl.BlockSpec((tm, w), lambda i, pos: (i, 0)),
                      pl.BlockSpec(memory_space=pl.ANY)],
            out_specs=pl.BlockSpec(memory_space=pl.ANY),
            scratch_shapes=[pltpu.SemaphoreType.DMA],
        ),
        input_output_aliases={2: 0},
        compiler_params=_cparams(("arbitrary",)),
        name="moe_dispatch",
    )(pos, h2ext, jnp.zeros((n_rows, w), F32))


def _moe_kernel(ea_ref, eb_ref, nu_ref, h_ref, wga_ref, wua_ref, wda_ref, wgb_ref, wub_ref, wdb_ref, o_ref):
    del ea_ref, eb_ref
    d = o_ref.shape[-1]

    @pl.when(pl.program_id(0) < nu_ref[0])
    def _():
        h = h_ref[:, 0:d].astype(BF16)
        tail = h_ref[:, d:]
        lane = _iota((1, LANES), 1)
        w_a = jnp.sum(jnp.where(lane == 0, tail, 0.0), axis=-1, keepdims=True)
        w_b = jnp.sum(jnp.where(lane == 1, tail, 0.0), axis=-1, keepdims=True)

        def expert(wg_ref, wu_ref, wd_ref):
            hidden = _silu(jnp.dot(h, wg_ref[0], preferred_element_type=F32)) * jnp.dot(
                h, wu_ref[0], preferred_element_type=F32)
            return jnp.dot(hidden.astype(BF16), wd_ref[0], preferred_element_type=F32)

        o_ref[...] = w_a * expert(wga_ref, wua_ref, wda_ref) + w_b * expert(wgb_ref, wub_ref, wdb_ref)

    @pl.when(pl.program_id(0) >= nu_ref[0])
    def _():
        o_ref[...] = jnp.zeros_like(o_ref)


def _moe_call(ea, eb, n_used, h_sorted, wg, wu, wd, tile_rows):
    n_rows, _ = h_sorted.shape
    w = h_sorted.shape[1]
    _, d, f = wg.shape

    def rows(j, ea, eb, nu):
        return (jnp.minimum(j, nu[0] - 1), 0)

    def wa(j, ea, eb, nu):
        return (ea[j], 0, 0)

    def wb(j, ea, eb, nu):
        return (eb[j], 0, 0)

    return pl.pallas_call(
        _moe_kernel,
        out_shape=jax.ShapeDtypeStruct((n_rows, d), F32),
        grid_spec=pltpu.PrefetchScalarGridSpec(
            num_scalar_prefetch=3,
            grid=(n_rows // tile_rows,),
            in_specs=[pl.BlockSpec((tile_rows, w), rows),
                      pl.BlockSpec((1, d, f), wa), pl.BlockSpec((1, d, f), wa), pl.BlockSpec((1, f, d), wa),
                      pl.BlockSpec((1, d, f), wb), pl.BlockSpec((1, d, f), wb), pl.BlockSpec((1, f, d), wb)],
            out_specs=pl.BlockSpec((tile_rows, d), lambda j, ea, eb, nu: (j, 0)),
        ),
        compiler_params=_cparams(("arbitrary",)),
        name="moe_experts",
    )(ea, eb, n_used, h_sorted, wg, wu, wd, wg, wu, wd)


def _combine_kernel(pos_ref, y_ref, x_ref, mod_ref, fg_ref, o_ref, buf, sem, *, tm, final):
    i = pl.program_id(0)

    def issue(tile, slot):
        def one(r, carry):
            src = pos_ref[tile * tm + r]
            pltpu.make_async_copy(y_ref.at[pl.ds(src, 1), :], buf.at[slot, pl.ds(r, 1), :], sem.at[slot]).start()
            return carry
        lax.fori_loop(0, tm, one, 0, unroll=8)

    @pl.when(i == 0)
    def _():
        issue(0, 0)

    @pl.when(i + 1 < pl.num_programs(0))
    def _():
        issue(i + 1, (i + 1) % 2)

    slot = i % 2
    pltpu.make_async_copy(y_ref.at[pl.ds(0, tm), :], buf.at[slot], sem.at[slot]).wait()
    out = x_ref[...] + (1.0 + mod_ref[0, 5:6, :]) * buf[slot]
    if final:
        ms = jnp.mean(out * out, axis=-1, keepdims=True)
        out = out * lax.rsqrt(ms + EPS) * fg_ref[...]
    o_ref[...] = out


def _combine_call(pos, y_sorted, x, mod, final_g, tm, seq, final):
    t, d = x.shape
    per_seq = seq // tm
    return pl.pallas_call(
        functools.partial(_combine_kernel, tm=tm, final=final),
        out_shape=jax.ShapeDtypeStruct((t, d), F32),
        grid_spec=pltpu.PrefetchScalarGridSpec(
            num_scalar_prefetch=1,
            grid=(t // tm,),
            in_specs=[pl.BlockSpec(memory_space=pl.ANY),
                      pl.BlockSpec((tm, d), lambda i, pos: (i, 0)),
                      pl.BlockSpec((1, N_MOD, d), lambda i, pos: (i // per_seq, 0, 0)),
                      pl.BlockSpec((1, d), lambda i, pos: (0, 0))],
            out_specs=pl.BlockSpec((tm, d), lambda i, pos: (i, 0)),
            scratch_shapes=[pltpu.VMEM((2, tm, d), F32), pltpu.SemaphoreType.DMA((2,))],
        ),
        compiler_params=_cparams(("arbitrary",)),
        name="moe_combine",
    )(pos, y_sorted, x, mod, final_g)


def _pad_lanes(a, n=LANES):
    return jnp.pad(a, [(0, 0)] * (a.ndim - 1) + [(0, n - a.shape[-1])])


def kernel(x, c, w_ada, b_ada, norm1_g, w_in, fox_f_bias, fox_norm_g, hgrn_lb_logits, hgrn_norm_g, pool_w, pool_scale, w_out, norm2_g, router_group_w, router_group_b, router_expert_w, router_expert_b, expert_w_gate, expert_w_up, expert_w_down, final_norm_g):
    bsz, seq, d = x.shape
    depth = w_ada.shape[0]
    fox_heads = fox_f_bias.shape[1]
    fox_dim = fox_heads * HEAD_DIM
    hgrn_dim = hgrn_lb_logits.shape[1]
    pool_dim = pool_scale.shape[1]
    n_fox_pairs = fox_dim // LANES
    n_hgrn_pairs = hgrn_dim // LANES
    t = bsz * seq
    tm = min(512, seq)
    tq = min(256, seq)
    n_tiles = t // MOE_TILE_ROWS + N_CLASSES
    assert n_tiles <= LANES and t % MOE_TILE_ROWS == 0

    o_ff = 3 * fox_dim
    w_in_r = jnp.concatenate(
        [w_in[:, :, :o_ff], w_in[:, :, o_ff + fox_heads:], _pad_lanes(w_in[:, :, o_ff:o_ff + fox_heads])],
        axis=-1).astype(BF16)
    f_bias = _pad_lanes(fox_f_bias)
    groups = pool_w.shape[1]
    pool_bd = jnp.einsum('lgcd,gh->lgchd', pool_w, jnp.eye(groups, dtype=pool_w.dtype)).reshape(
        depth, pool_dim, pool_dim).astype(BF16)
    w_router = _pad_lanes(jnp.concatenate([router_group_w, router_expert_w], axis=-1))
    b_router = _pad_lanes(jnp.concatenate([router_group_b, router_expert_b], axis=-1))
    w_out_b = w_out.astype(BF16)
    wg_b = expert_w_gate.astype(BF16)
    wu_b = expert_w_up.astype(BF16)
    wd_b = expert_w_down.astype(BF16)
    pair_a = jnp.array([0, 0, 0, 1, 1, 2], jnp.int32)
    pair_b = jnp.array([1, 2, 3, 2, 3, 3], jnp.int32)
    final_g = final_norm_g.reshape(1, d)

    mod_all = _ada_call(c, w_ada, b_ada).reshape(depth, bsz, N_MOD, d)
    lower = _lb_call(hgrn_lb_logits)

    for l in range(depth):
        mod = mod_all[l]
        qkv, hg4, pu, ff = _inproj_call(x, norm1_g[l:l + 1], mod, w_in_r[l], 3 * fox_dim, 4 * hgrn_dim,
                                        pool_dim, tm)
        fcol, frow = _fbias_call(ff, f_bias[l:l + 1])
        o_fox = _fox_call(qkv, fcol, frow, fox_norm_g[l].reshape(n_fox_pairs, 1, LANES), n_fox_pairs, tq)
        o_hgrn = _hgrn_call(hg4, lower[l].reshape(n_hgrn_pairs, 1, LANES),
                            hgrn_norm_g[l].reshape(n_hgrn_pairs, 1, LANES), n_hgrn_pairs)
        o_pool = _pool_call(pu, pool_bd[l], pool_scale[l:l + 1])
        x2, h2ext = _outproj_call(o_fox.reshape(t, -1), o_hgrn.reshape(t, -1), o_pool.reshape(t, -1),
                                  x.reshape(t, d), w_out_b[l], mod, norm2_g[l:l + 1],
                                  w_router[l], b_router[l:l + 1], tm, seq)
        pos8, meta = _sort_call(h2ext, d, MOE_TILE_ROWS)
        pos = pos8[0]
        n_used = meta[0, 1].astype(jnp.int32).reshape(1)
        tile = jnp.minimum(jnp.arange(n_tiles, dtype=jnp.int32), n_used[0] - 1)
        tile_cls = jnp.minimum(meta[:, 0].astype(jnp.int32)[tile], N_CLASSES - 1)
        e0 = (tile_cls // N_PAIRS) * EXPERTS_PER_GROUP
        ea = e0 + pair_a[tile_cls % N_PAIRS]
        eb = e0 + pair_b[tile_cls % N_PAIRS]
        h_sorted = _dispatch_call(pos, h2ext, n_tiles * MOE_TILE_ROWS, tm)
        y_sorted = _moe_call(ea, eb, n_used, h_sorted, wg_b[l], wu_b[l], wd_b[l], MOE_TILE_ROWS)
        x = _combine_call(pos, y_sorted, x2, mod, final_g, tm, seq, final=(l == depth - 1)).reshape(bsz, seq, d)
    return x
```

```python
import functools

import jax
import jax.numpy as jnp
from jax import lax
from jax.experimental import pallas as pl
from jax.experimental.pallas import tpu as pltpu

F32 = jnp.float32
BF16 = jnp.bfloat16
HIGHEST = lax.Precision.HIGHEST

HEAD_DIM = 64
LANES = 128
EPS = 1e-6
N_MOD = 6
POOL_WINDOWS = (2, 4, 8, 16)
N_GROUPS = 4
EXPERTS_PER_GROUP = 4
N_EXPERTS = N_GROUPS * EXPERTS_PER_GROUP
ROUTER_EXPERT_LANE0 = N_GROUPS
N_PAIRS = 6
N_CLASSES = N_GROUPS * N_PAIRS
ROUTE_CLASS_LANE0 = 32
MOE_TILE_ROWS = 256
HGRN_CHUNK = 128
HGRN_LEVELS = (64, 32, 16, 8)
HGRN_DIAG = 8
VMEM_LIMIT = 48 * 1024 * 1024
LOG2_E = 1.4426950408889634
FOX_Q_SCALE = HEAD_DIM ** -0.5 * LOG2_E


def _cparams(sem):
    return pltpu.CompilerParams(dimension_semantics=sem, vmem_limit_bytes=VMEM_LIMIT)


def _silu(x):
    return x * jax.nn.sigmoid(x)


def _iota(shape, dim):
    return lax.broadcasted_iota(jnp.int32, shape, dim)


def _ada_kernel(c_ref, w_ref, b_ref, o_ref):
    sc = _silu(c_ref[...]).astype(BF16)
    o_ref[0] = jnp.dot(sc, w_ref[0].astype(BF16), preferred_element_type=F32) + b_ref[0]


def _ada_call(c, w_ada, b_ada):
    depth, d, n = w_ada.shape
    bsz = c.shape[0]
    tn = 1536
    return pl.pallas_call(
        _ada_kernel,
        out_shape=jax.ShapeDtypeStruct((depth, bsz, n), F32),
        grid=(depth, n // tn),
        in_specs=[
            pl.BlockSpec((bsz, d), lambda l, j: (0, 0)),
            pl.BlockSpec((1, d, tn), lambda l, j: (l, 0, j)),
            pl.BlockSpec((1, 1, tn), lambda l, j: (l, 0, j)),
        ],
        out_specs=pl.BlockSpec((1, bsz, tn), lambda l, j: (l, 0, j)),
        compiler_params=_cparams(("arbitrary", "arbitrary")),
        name="ada_mod",
    )(c, w_ada, b_ada.reshape(depth, 1, n))


def _lb_kernel(x_ref, o_ref):
    x = x_ref[...]
    depth = x.shape[0]
    e = jnp.exp(x - jnp.max(x, axis=0, keepdims=True))
    p = e / jnp.sum(e, axis=0, keepdims=True)
    acc = jnp.zeros_like(p[0:1])
    o_ref[0:1, :] = acc
    for l in range(1, depth):
        acc = acc + p[l:l + 1]
        o_ref[l:l + 1, :] = acc


def _lb_call(lb_logits):
    return pl.pallas_call(
        _lb_kernel,
        out_shape=jax.ShapeDtypeStruct(lb_logits.shape, F32),
        name="hgrn_lower_bounds",
    )(lb_logits)


def _norm_mod(x, g, shift, scale):
    ms = jnp.mean(x * x, axis=-1, keepdims=True)
    return (x * lax.rsqrt(ms + EPS) * g) * (1.0 + scale) + shift


def _inproj_kernel(x_ref, g_ref, mod_ref, w_ref, qkv_ref, hg_ref, pu_ref, ff_ref, *, n_qkv, n_hg, n_pu):
    h = _norm_mod(x_ref[0], g_ref[...], mod_ref[0, 0:1, :], mod_ref[0, 1:2, :]).astype(BF16)
    o0, o1, o2 = n_qkv, n_qkv + n_hg, n_qkv + n_hg + n_pu
    q_scale = jnp.where(_iota((1, n_qkv), 1) < n_qkv // 3, FOX_Q_SCALE, 1.0)
    qkv_ref[0] = (jnp.dot(h, w_ref[:, 0:o0], preferred_element_type=F32) * q_scale).astype(BF16)
    hg_ref[0] = jnp.dot(h, w_ref[:, o0:o1], preferred_element_type=F32)
    pu_ref[0] = jnp.dot(h, w_ref[:, o1:o2], preferred_element_type=F32)
    ff_ref[0] = jnp.dot(h, w_ref[:, o2:], preferred_element_type=F32)


def _inproj_call(x, g, mod, w, n_qkv, n_hg, n_pu, tm):
    bsz, seq, d = x.shape
    n_all = w.shape[1]
    n_ff = n_all - n_qkv - n_hg - n_pu
    return pl.pallas_call(
        functools.partial(_inproj_kernel, n_qkv=n_qkv, n_hg=n_hg, n_pu=n_pu),
        out_shape=(
            jax.ShapeDtypeStruct((bsz, seq, n_qkv), BF16),
            jax.ShapeDtypeStruct((bsz, seq, n_hg), F32),
            jax.ShapeDtypeStruct((bsz, seq, n_pu), F32),
            jax.ShapeDtypeStruct((bsz, seq, n_ff), F32),
        ),
        grid=(bsz, seq // tm),
        in_specs=[
            pl.BlockSpec((1, tm, d), lambda b, i: (b, i, 0)),
            pl.BlockSpec((1, d), lambda b, i: (0, 0)),
            pl.BlockSpec((1, N_MOD, d), lambda b, i: (b, 0, 0)),
            pl.BlockSpec((d, n_all), lambda b, i: (0, 0)),
        ],
        out_specs=(
            pl.BlockSpec((1, tm, n_qkv), lambda b, i: (b, i, 0)),
            pl.BlockSpec((1, tm, n_hg), lambda b, i: (b, i, 0)),
            pl.BlockSpec((1, tm, n_pu), lambda b, i: (b, i, 0)),
            pl.BlockSpec((1, tm, n_ff), lambda b, i: (b, i, 0)),
        ),
        compiler_params=_cparams(("arbitrary", "arbitrary")),
        name="norm_inproj",
    )(x, g, mod, w)


def _fbias_kernel(ff_ref, bias_ref, fcol_ref, *, cb):
    seq = ff_ref.shape[1]
    tri = (_iota((cb, cb), 0) >= _iota((cb, cb), 1)).astype(F32)
    carry = jnp.zeros((1, LANES), F32)
    for blk in range(seq // cb):
        x = ff_ref[0, blk * cb:(blk + 1) * cb, :] + bias_ref[...]
        log_f = jnp.minimum(x, 0.0) - jnp.log1p(jnp.exp(-jnp.abs(x)))
        cs = jnp.dot(tri, log_f, precision=HIGHEST, preferred_element_type=F32) + carry
        fcol_ref[0, blk * cb:(blk + 1) * cb, :] = cs * LOG2_E
        carry = cs[cb - 1:cb, :]


def _fbias_call(ff, bias):
    bsz, seq, _ = ff.shape
    cb = min(256, seq)
    return pl.pallas_call(
        functools.partial(_fbias_kernel, cb=cb),
        out_shape=jax.ShapeDtypeStruct((bsz, seq, LANES), F32),
        grid=(bsz,),
        in_specs=[pl.BlockSpec((1, seq, LANES), lambda b: (b, 0, 0)),
                  pl.BlockSpec((1, LANES), lambda b: (0, 0))],
        out_specs=pl.BlockSpec((1, seq, LANES), lambda b: (b, 0, 0)),
        compiler_params=_cparams(("arbitrary",)),
        name="fox_forget_bias",
    )(ff, bias)


def _split3(f):
    hi = f.astype(BF16).astype(F32)
    rest = f - hi
    mid = rest.astype(BF16).astype(F32)
    return hi, mid, rest - mid


def _fox_kernel(q_ref, k_ref, v_ref, fc_ref, gn_ref, o_ref, k0_sc, k1_sc, v0_sc, v1_sc, *, tq):
    p = pl.program_id(1)
    i = pl.program_id(2)
    seq = k_ref.shape[1]
    lane = _iota((1, LANES), 1)
    h0 = lane < HEAD_DIM
    data = (h0, jnp.logical_not(h0))
    base = (HEAD_DIM, 0)
    k_sc = (k0_sc, k1_sc)
    v_sc = (v0_sc, v1_sc)
    nt = (((1,), (1,)), ((), ()))

    def head_f(fc, which):
        return jnp.sum(jnp.where(lane == 2 * p + which, fc, 0.0), axis=-1, keepdims=True)

    def with_bias_lanes(x, which, first3, last3):
        out = jnp.where(data[which], x, 0.0)
        for n in range(3):
            out = jnp.where(lane == base[which] + n, first3[n], out)
            out = jnp.where(lane == base[which] + 3 + n, last3[n], out)
        return out.astype(BF16)

    @pl.when(i == 0)
    def _():
        for blk in range(seq // tq):
            rows = slice(blk * tq, (blk + 1) * tq)
            fc = fc_ref[0, rows, :]
            kf = k_ref[0, rows, :].astype(F32)
            vf = v_ref[0, rows, :].astype(F32)
            for which in range(2):
                hi, mid, lo = _split3(head_f(fc, which))
                k_sc[which][rows, :] = with_bias_lanes(kf, which, (1.0, 1.0, 1.0), (-hi, -mid, -lo))
                v_sc[which][rows, :] = jnp.where(data[which], vf,
                                                 jnp.where(lane == base[which], 1.0, 0.0)).astype(BF16)

    qf = q_ref[0].astype(F32)
    fcq = fc_ref[0, pl.ds(pl.multiple_of(i * tq, tq), tq), :]
    qs = [with_bias_lanes(qf, which, _split3(head_f(fcq, which)), (1.0, 1.0, 1.0)) for which in range(2)]

    def step(j, carry, masked):
        start = pl.multiple_of(j * tq, tq)
        new = []
        for which in range(2):
            m, acc = carry[which]
            s = lax.dot_general(qs[which], k_sc[which][pl.ds(start, tq), :], nt, preferred_element_type=F32)
            if masked:
                s = jnp.where(_iota((tq, tq), 0) >= _iota((tq, tq), 1), s, -jnp.inf)
            n = jnp.maximum(m, jnp.max(s, axis=-1, keepdims=True))
            pr = jnp.exp2(s - n).astype(BF16)
            acc = jnp.exp2(m - n) * acc + jnp.dot(pr, v_sc[which][pl.ds(start, tq), :],
                                                  preferred_element_type=F32)
            new.append((n, acc))
        return tuple(new)

    init = ((jnp.full((tq, 1), -1e30, F32), jnp.zeros((tq, LANES), F32)),) * 2
    carry = lax.fori_loop(0, i, functools.partial(step, masked=False), init)
    (_, acc0), (_, acc1) = step(i, carry, True)
    l0 = jnp.sum(jnp.where(lane == base[0], acc0, 0.0), axis=-1, keepdims=True)
    l1 = jnp.sum(jnp.where(lane == base[1], acc1, 0.0), axis=-1, keepdims=True)
    o = jnp.where(h0, acc0 / l0, acc1 / l1)
    o2 = o * o
    ms0 = jnp.sum(jnp.where(h0, o2, 0.0), axis=-1, keepdims=True)
    ms1 = jnp.sum(jnp.where(h0, 0.0, o2), axis=-1, keepdims=True)
    ms = jnp.where(h0, ms0, ms1) * (1.0 / HEAD_DIM)
    o_ref[0] = (o * lax.rsqrt(ms + EPS) * gn_ref[0]).astype(BF16)


def _fox_call(qkv, fcol, gn, n_pairs, tq):
    bsz, seq, _ = qkv.shape
    return pl.pallas_call(
        functools.partial(_fox_kernel, tq=tq),
        out_shape=jax.ShapeDtypeStruct((bsz, seq, n_pairs * LANES), BF16),
        grid=(bsz, n_pairs, seq // tq),
        in_specs=[
            pl.BlockSpec((1, tq, LANES), lambda b, p, i: (b, i, p)),
            pl.BlockSpec((1, seq, LANES), lambda b, p, i: (b, 0, n_pairs + p)),
            pl.BlockSpec((1, seq, LANES), lambda b, p, i: (b, 0, 2 * n_pairs + p)),
            pl.BlockSpec((1, seq, LANES), lambda b, p, i: (b, 0, 0)),
            pl.BlockSpec((1, 1, LANES), lambda b, p, i: (p, 0, 0)),
        ],
        out_specs=pl.BlockSpec((1, tq, LANES), lambda b, p, i: (b, i, p)),
        scratch_shapes=[pltpu.VMEM((seq, LANES), BF16)] * 4,
        compiler_params=_cparams(("arbitrary", "arbitrary", "arbitrary")),
        name="fox_attention",
    )(qkv, qkv, qkv, fcol, gn)


def _hgrn_kernel(hq_ref, hf_ref, hi_ref, hg_ref, lb_ref, gn_ref, o_ref, p_sc0, p_sc1):
    ch = HGRN_CHUNK
    seq = hq_ref.shape[1]
    lane = _iota((1, LANES), 1)
    h0 = lane < HEAD_DIM
    r = _iota((ch, ch), 0)
    c = _iota((ch, ch), 1)
    tri = (r >= c).astype(F32)
    same_head = (r < HEAD_DIM) == (c < HEAD_DIM)
    seg = same_head.astype(BF16)
    level_masks = [(r // (2 * m) == c // (2 * m)) & (r % (2 * m) >= m) & (c % (2 * m) < m) for m in HGRN_LEVELS]
    row8 = _iota((HGRN_DIAG, LANES), 0)
    lb = lb_ref[0]
    nt = (((1,), (1,)), ((), ()))

    def chunk(ci, state_t, p_sc):
        sl = pl.ds(pl.multiple_of(ci * ch, ch), ch)
        f = lb + (1.0 - lb) * jax.nn.sigmoid(hf_ref[0, sl, :])
        g = jnp.log(f)
        k = 1.0 - f
        q = _silu(hq_ref[0, sl, :])
        v = hi_ref[0, sl, :]
        vb = v.astype(BF16)
        b = jnp.dot(tri, g, precision=HIGHEST, preferred_element_type=F32)
        b_last = b[ch - 1:ch, :]
        q_head = (jnp.where(h0, q, 0.0), jnp.where(h0, 0.0, q))

        sc = [jnp.zeros((ch, ch), F32), jnp.zeros((ch, ch), F32)]
        for li, m in enumerate(HGRN_LEVELS):
            ref_l = jnp.broadcast_to(b.reshape(ch // (2 * m), 2 * m, LANES)[:, m - 1:m, :],
                                     (ch // (2 * m), 2 * m, LANES)).reshape(ch, LANES)
            e = jnp.exp(-jnp.abs(b - ref_l))
            kt = (k * e).astype(BF16)
            for hd in range(2):
                s = lax.dot_general((q_head[hd] * e).astype(BF16), kt, nt, preferred_element_type=F32)
                sc[hd] = sc[hd] + jnp.where(level_masks[li], s, 0.0)
        o = jnp.where(h0,
                      jnp.dot(sc[0].astype(BF16), vb, preferred_element_type=F32),
                      jnp.dot(sc[1].astype(BF16), vb, preferred_element_type=F32))

        nd = ch // HGRN_DIAG
        for blk in range(nd):
            t0 = blk * HGRN_DIAG
            qb = q[t0:t0 + HGRN_DIAG]
            bb = b[t0:t0 + HGRN_DIAG]
            for j in range(HGRN_DIAG):
                kj = k[t0 + j:t0 + j + 1]
                bj = b[t0 + j:t0 + j + 1]
                decay = jnp.exp(jnp.where(row8 >= j, bb - bj, -jnp.inf))
                p_sc[(t0 + j) * HGRN_DIAG:(t0 + j + 1) * HGRN_DIAG, :] = qb * kj * decay
        w = jnp.dot(p_sc[...].astype(BF16), seg, preferred_element_type=F32)
        diag_rows = []
        for blk in range(nd):
            t0 = blk * HGRN_DIAG
            od = jnp.zeros((HGRN_DIAG, LANES), F32)
            for j in range(HGRN_DIAG):
                od = od + w[(t0 + j) * HGRN_DIAG:(t0 + j + 1) * HGRN_DIAG] * v[t0 + j:t0 + j + 1]
            diag_rows.append(od)
        o = o + jnp.concatenate(diag_rows, axis=0)

        o = o + lax.dot_general((q * jnp.exp(b)).astype(BF16), state_t.astype(BF16), nt,
                                preferred_element_type=F32)

        k_dec = (k * jnp.exp(b_last - b)).astype(BF16)
        upd_t = jnp.dot(v.T.astype(BF16), k_dec, preferred_element_type=F32)
        state_t = jnp.exp(b_last) * state_t + jnp.where(same_head, upd_t, 0.0)

        o2 = o * o
        ms0 = jnp.sum(jnp.where(h0, o2, 0.0), axis=-1, keepdims=True)
        ms1 = jnp.sum(jnp.where(h0, 0.0, o2), axis=-1, keepdims=True)
        ms = jnp.where(h0, ms0, ms1) * (1.0 / HEAD_DIM)
        out = o * lax.rsqrt(ms + EPS) * gn_ref[0] * _silu(hg_ref[0, sl, :])
        o_ref[0, sl, :] = out.astype(BF16)
        return state_t

    def two_chunks(cc, state_t):
        return chunk(2 * cc + 1, chunk(2 * cc, state_t, p_sc0), p_sc1)

    lax.fori_loop(0, seq // (2 * ch), two_chunks, jnp.zeros((LANES, LANES), F32))


def _hgrn_call(hg4, lb, gn, n_pairs):
    bsz, seq, _ = hg4.shape

    def spec(off):
        return pl.BlockSpec((1, seq, LANES), lambda b, p: (b, 0, off + p))

    return pl.pallas_call(
        _hgrn_kernel,
        out_shape=jax.ShapeDtypeStruct((bsz, seq, n_pairs * LANES), BF16),
        grid=(bsz, n_pairs),
        in_specs=[spec(0), spec(n_pairs), spec(2 * n_pairs), spec(3 * n_pairs),
                  pl.BlockSpec((1, 1, LANES), lambda b, p: (p, 0, 0)),
                  pl.BlockSpec((1, 1, LANES), lambda b, p: (p, 0, 0))],
        out_specs=pl.BlockSpec((1, seq, LANES), lambda b, p: (b, 0, p)),
        scratch_shapes=[pltpu.VMEM((HGRN_CHUNK * HGRN_DIAG, LANES), F32)] * 2,
        compiler_params=_cparams(("arbitrary", "arbitrary")),
        name="hgrn2",
    )(hg4, hg4, hg4, hg4, lb, gn)


def _pool_kernel(u_ref, w_ref, s_ref, o_ref):
    u = u_ref[0]
    seq, n = u.shape
    t = _iota((seq, 1), 0)
    lane = _iota((1, n), 1)

    def shifted(x, k):
        return jnp.where(t >= k, pltpu.roll(x, k, axis=0), 0.0)

    sums = []
    s = u
    for w in POOL_WINDOWS:
        s = s + shifted(s, w // 2)
        sums.append(s)
    pos1 = (t + 1).astype(F32)
    group = len(POOL_WINDOWS) - 1
    mean = sums[group] / jnp.minimum(pos1, float(POOL_WINDOWS[group]))
    group_dim = n // len(POOL_WINDOWS)
    for gi in range(group - 1, -1, -1):
        mean = jnp.where(lane < (gi + 1) * group_dim,
                         sums[gi] / jnp.minimum(pos1, float(POOL_WINDOWS[gi])), mean)
    pooled = (mean - u).astype(BF16)
    o_ref[0] = (jnp.dot(pooled, w_ref[...], preferred_element_type=F32) * s_ref[...]).astype(BF16)


def _pool_call(pu, w_bd, scale):
    bsz, seq, n = pu.shape
    return pl.pallas_call(
        _pool_kernel,
        out_shape=jax.ShapeDtypeStruct((bsz, seq, n), BF16),
        grid=(bsz,),
        in_specs=[pl.BlockSpec((1, seq, n), lambda b: (b, 0, 0)),
                  pl.BlockSpec((n, n), lambda b: (0, 0)),
                  pl.BlockSpec((1, n), lambda b: (0, 0))],
        out_specs=pl.BlockSpec((1, seq, n), lambda b: (b, 0, 0)),
        compiler_params=_cparams(("arbitrary",)),
        name="multiscale_pool",
    )(pu, w_bd, scale)


def _route(logits):
    lane = _iota(logits.shape, 1)
    lane_f = lane.astype(F32)
    big = float(LANES)
    is_g = lane < N_GROUPS
    gl = jnp.where(is_g, logits, -jnp.inf)
    gmax = jnp.max(gl, axis=-1, keepdims=True)
    gsum = jnp.sum(jnp.where(is_g, jnp.exp(gl - gmax), 0.0), axis=-1, keepdims=True)
    p_group = 1.0 / gsum
    g_sel = jnp.min(jnp.where(gl == gmax, lane_f, big), axis=-1, keepdims=True)
    lo = ROUTER_EXPERT_LANE0 + EXPERTS_PER_GROUP * g_sel
    in_group = (lane_f >= lo) & (lane_f < lo + EXPERTS_PER_GROUP)
    el = jnp.where(in_group, logits, -jnp.inf)
    m1 = jnp.max(el, axis=-1, keepdims=True)
    i1 = jnp.min(jnp.where(el == m1, lane_f, big), axis=-1, keepdims=True)
    el2 = jnp.where(lane_f == i1, -jnp.inf, el)
    m2 = jnp.max(el2, axis=-1, keepdims=True)
    i2 = jnp.min(jnp.where(el2 == m2, lane_f, big), axis=-1, keepdims=True)
    e2 = jnp.exp(m2 - m1)
    w1 = p_group / (1.0 + e2)
    w2 = p_group * e2 / (1.0 + e2)
    first_low = i1 < i2
    a = jnp.minimum(i1, i2) - lo
    b = jnp.maximum(i1, i2) - lo
    pair = jnp.where(a == 0.0, b - 1.0, jnp.where(a == 1.0, b + 1.0, float(N_PAIRS - 1)))
    cls_lane = ROUTE_CLASS_LANE0 + g_sel * N_PAIRS + pair
    return (jnp.where(lane == 0, jnp.where(first_low, w1, w2), 0.0)
            + jnp.where(lane == 1, jnp.where(first_low, w2, w1), 0.0)
            + jnp.where(lane_f == cls_lane, 1.0, 0.0))


def _outproj_kernel(of_ref, oh_ref, op_ref, x_ref, w_ref, mod_ref, g_ref, wr_ref, br_ref,
                    xn_ref, h2_ref, *, n_fox, n_hgrn):
    d = x_ref.shape[-1]
    y = jnp.dot(of_ref[...], w_ref[0:n_fox, :], preferred_element_type=F32)
    y = y + jnp.dot(oh_ref[...], w_ref[n_fox:n_fox + n_hgrn, :], preferred_element_type=F32)
    y = y + jnp.dot(op_ref[...], w_ref[n_fox + n_hgrn:, :], preferred_element_type=F32)
    xn = x_ref[...] + (1.0 + mod_ref[0, 2:3, :]) * y
    xn_ref[...] = xn
    h2 = _norm_mod(xn, g_ref[...], mod_ref[0, 3:4, :], mod_ref[0, 4:5, :])
    h2_ref[:, 0:d] = h2
    logits = jnp.dot(h2, wr_ref[...], precision=HIGHEST, preferred_element_type=F32) + br_ref[...]
    h2_ref[:, d:] = _route(logits)


def _outproj_call(o_fox, o_hgrn, o_pool, x, w_out, mod, g2, w_router, b_router, tm, seq):
    t, d = x.shape
    n_fox, n_hgrn, n_pool = o_fox.shape[-1], o_hgrn.shape[-1], o_pool.shape[-1]
    per_seq = seq // tm
    return pl.pallas_call(
        functools.partial(_outproj_kernel, n_fox=n_fox, n_hgrn=n_hgrn),
        out_shape=(jax.ShapeDtypeStruct((t, d), F32),
                   jax.ShapeDtypeStruct((t, d + LANES), F32)),
        grid=(t // tm,),
        in_specs=[
            pl.BlockSpec((tm, n_fox), lambda i: (i, 0)),
            pl.BlockSpec((tm, n_hgrn), lambda i: (i, 0)),
            pl.BlockSpec((tm, n_pool), lambda i: (i, 0)),
            pl.BlockSpec((tm, d), lambda i: (i, 0)),
            pl.BlockSpec((d, d), lambda i: (0, 0)),
            pl.BlockSpec((1, N_MOD, d), lambda i: (i // per_seq, 0, 0)),
            pl.BlockSpec((1, d), lambda i: (0, 0)),
            pl.BlockSpec((d, LANES), lambda i: (0, 0)),
            pl.BlockSpec((1, LANES), lambda i: (0, 0)),
        ],
        out_specs=(pl.BlockSpec((tm, d), lambda i: (i, 0)),
                   pl.BlockSpec((tm, d + LANES), lambda i: (i, 0))),
        compiler_params=_cparams(("arbitrary",)),
        name="outproj_router",
    )(o_fox, o_hgrn, o_pool, x, w_out, mod, g2, w_router, b_router)


def _sort_kernel(route_ref, pos_ref, meta_ref, *, tile_rows, cb):
    t = route_ref.shape[0]
    lane = _iota((1, LANES), 1)
    is_cls = (lane >= ROUTE_CLASS_LANE0) & (lane < ROUTE_CLASS_LANE0 + N_CLASSES)
    nblk = t // cb
    zero_row = jnp.zeros((1, LANES), F32)

    def onehot(i):
        return jnp.where(is_cls, route_ref[pl.ds(pl.multiple_of(i * cb, cb), cb), :], 0.0)

    counts = lax.fori_loop(0, nblk, lambda i, acc: acc + jnp.sum(onehot(i), axis=0, keepdims=True), zero_row)
    padded = jnp.floor((counts + float(tile_rows - 1)) * (1.0 / tile_rows)) * float(tile_rows)
    before = (_iota((LANES, LANES), 0) < _iota((LANES, LANES), 1)).astype(F32)
    offs = jnp.dot(jnp.broadcast_to(padded, (8, LANES)), before, precision=HIGHEST,
                   preferred_element_type=F32)[0:1]
    ends = offs + padded
    strict = (_iota((cb, cb), 0) > _iota((cb, cb), 1)).astype(BF16)
    ones8 = jnp.ones((8, LANES), F32)
    nt = (((1,), (1,)), ((), ()))

    def place(i, seen):
        oh = onehot(i)
        rank = jnp.dot(strict, oh.astype(BF16), preferred_element_type=F32) + seen
        dest = oh * (rank + offs)
        rows = lax.dot_general(ones8, dest, nt, precision=HIGHEST, preferred_element_type=F32)
        pos_ref[:, pl.ds(pl.multiple_of(i * cb, cb), cb)] = rows.astype(jnp.int32)
        return seen + jnp.sum(oh, axis=0, keepdims=True)

    lax.fori_loop(0, nblk, place, zero_row)
    tile_start = _iota((LANES, LANES), 0).astype(F32) * float(tile_rows)
    tile_cls = jnp.sum(jnp.where(is_cls & (ends <= tile_start), 1.0, 0.0), axis=-1, keepdims=True)
    n_used = jnp.sum(jnp.where(lane == ROUTE_CLASS_LANE0 + N_CLASSES - 1, ends, 0.0), axis=-1,
                     keepdims=True) * (1.0 / tile_rows)
    meta_ref[...] = jnp.where(lane == 0, tile_cls, jnp.where(lane == 1, n_used, 0.0))


def _sort_call(h2ext, d, tile_rows):
    t = h2ext.shape[0]
    cb = min(512, t)
    return pl.pallas_call(
        functools.partial(_sort_kernel, tile_rows=tile_rows, cb=cb),
        out_shape=(jax.ShapeDtypeStruct((8, t), jnp.int32),
                   jax.ShapeDtypeStruct((LANES, LANES), F32)),
        grid=(1,),
        in_specs=[pl.BlockSpec((t, LANES), lambda i: (0, d // LANES))],
        out_specs=(pl.BlockSpec((8, t), lambda i: (0, 0)),
                   pl.BlockSpec((LANES, LANES), lambda i: (0, 0))),
        compiler_params=_cparams(("arbitrary",)),
        name="route_sort",
    )(h2ext)


def _dispatch_kernel(pos_ref, h_ref, init_ref, out_ref, sem, *, tm):
    del init_ref
    base = pl.program_id(0) * tm

    def issue(r, carry):
        dst = pos_ref[base + r]
        pltpu.make_async_copy(h_ref.at[pl.ds(r, 1), :], out_ref.at[pl.ds(dst, 1), :], sem).start()
        return carry

    lax.fori_loop(0, tm, issue, 0, unroll=8)
    pltpu.make_async_copy(h_ref, out_ref.at[pl.ds(0, tm), :], sem).wait()


def _dispatch_call(pos, h2ext, n_rows, tm):
    t, w = h2ext.shape
    return pl.pallas_call(
        functools.partial(_dispatch_kernel, tm=tm),
        out_shape=jax.ShapeDtypeStruct((n_rows, w), F32),
        grid_spec=pltpu.PrefetchScalarGridSpec(
            num_scalar_prefetch=1,
            grid=(t // tm,),
            in_specs=[pl.BlockSpec((tm, w), lambda i, pos: (i, 0)),
                      pl.BlockSpec(memory_space=pl.ANY)],
            out_specs=pl.BlockSpec(memory_space=pl.ANY),
            scratch_shapes=[pltpu.SemaphoreType.DMA],
        ),
        input_output_aliases={2: 0},
        compiler_params=_cparams(("arbitrary",)),
        name="moe_dispatch",
    )(pos, h2ext, jnp.zeros((n_rows, w), F32))


def _moe_kernel(ea_ref, eb_ref, nu_ref, h_ref, wga_ref, wua_ref, wda_ref, wgb_ref, wub_ref, wdb_ref, o_ref):
    del ea_ref, eb_ref
    d = o_ref.shape[-1]

    @pl.when(pl.program_id(0) < nu_ref[0])
    def _():
        h = h_ref[:, 0:d].astype(BF16)
        tail = h_ref[:, d:]
        lane = _iota((1, LANES), 1)
        w_a = jnp.sum(jnp.where(lane == 0, tail, 0.0), axis=-1, keepdims=True)
        w_b = jnp.sum(jnp.where(lane == 1, tail, 0.0), axis=-1, keepdims=True)

        def expert(wg_ref, wu_ref, wd_ref):
            hidden = _silu(jnp.dot(h, wg_ref[0], preferred_element_type=F32)) * jnp.dot(
                h, wu_ref[0], preferred_element_type=F32)
            return jnp.dot(hidden.astype(BF16), wd_ref[0], preferred_element_type=F32)

        o_ref[...] = w_a * expert(wga_ref, wua_ref, wda_ref) + w_b * expert(wgb_ref, wub_ref, wdb_ref)

    @pl.when(pl.program_id(0) >= nu_ref[0])
    def _():
        o_ref[...] = jnp.zeros_like(o_ref)


def _moe_call(ea, eb, n_used, h_sorted, wg, wu, wd, tile_rows):
    n_rows, _ = h_sorted.shape
    w = h_sorted.shape[1]
    _, d, f = wg.shape

    def rows(j, ea, eb, nu):
        return (jnp.maximum(jnp.minimum(j, nu[0] - 1), 0), 0)

    def wa(j, ea, eb, nu):
        return (ea[j], 0, 0)

    def wb(j, ea, eb, nu):
        return (eb[j], 0, 0)

    return pl.pallas_call(
        _moe_kernel,
        out_shape=jax.ShapeDtypeStruct((n_rows, d), F32),
        grid_spec=pltpu.PrefetchScalarGridSpec(
            num_scalar_prefetch=3,
            grid=(n_rows // tile_rows,),
            in_specs=[pl.BlockSpec((tile_rows, w), rows),
                      pl.BlockSpec((1, d, f), wa), pl.BlockSpec((1, d, f), wa), pl.BlockSpec((1, f, d), wa),
                      pl.BlockSpec((1, d, f), wb), pl.BlockSpec((1, d, f), wb), pl.BlockSpec((1, f, d), wb)],
            out_specs=pl.BlockSpec((tile_rows, d), lambda j, ea, eb, nu: (j, 0)),
        ),
        compiler_params=_cparams(("arbitrary",)),
        name="moe_experts",
    )(ea, eb, n_used, h_sorted, wg, wu, wd, wg, wu, wd)


def _combine_kernel(pos_ref, y_ref, x_ref, mod_ref, fg_ref, o_ref, buf, sem, *, tm, final):
    i = pl.program_id(0)

    def issue(tile, slot):
        def one(r, carry):
            src = pos_ref[tile * tm + r]
            pltpu.make_async_copy(y_ref.at[pl.ds(src, 1), :], buf.at[slot, pl.ds(r, 1), :], sem.at[slot]).start()
            return carry
        lax.fori_loop(0, tm, one, 0, unroll=8)

    @pl.when(i == 0)
    def _():
        issue(0, 0)

    @pl.when(i + 1 < pl.num_programs(0))
    def _():
        issue(i + 1, (i + 1) % 2)

    slot = i % 2
    pltpu.make_async_copy(y_ref.at[pl.ds(0, tm), :], buf.at[slot], sem.at[slot]).wait()
    out = x_ref[...] + (1.0 + mod_ref[0, 5:6, :]) * buf[slot]
    if final:
        ms = jnp.mean(out * out, axis=-1, keepdims=True)
        out = out * lax.rsqrt(ms + EPS) * fg_ref[...]
    o_ref[...] = out


def _combine_call(pos, y_sorted, x, mod, final_g, tm, seq, final):
    t, d = x.shape
    per_seq = seq // tm
    return pl.pallas_call(
        functools.partial(_combine_kernel, tm=tm, final=final),
        out_shape=jax.ShapeDtypeStruct((t, d), F32),
        grid_spec=pltpu.PrefetchScalarGridSpec(
            num_scalar_prefetch=1,
            grid=(t // tm,),
            in_specs=[pl.BlockSpec(memory_space=pl.ANY),
                      pl.BlockSpec((tm, d), lambda i, pos: (i, 0)),
                      pl.BlockSpec((1, N_MOD, d), lambda i, pos: (i // per_seq, 0, 0)),
                      pl.BlockSpec((1, d), lambda i, pos: (0, 0))],
            out_specs=pl.BlockSpec((tm, d), lambda i, pos: (i, 0)),
            scratch_shapes=[pltpu.VMEM((2, tm, d), F32), pltpu.SemaphoreType.DMA((2,))],
        ),
        compiler_params=_cparams(("arbitrary",)),
        name="moe_combine",
    )(pos, y_sorted, x, mod, final_g)


def _pad_lanes(a, n=LANES):
    return jnp.pad(a, [(0, 0)] * (a.ndim - 1) + [(0, n - a.shape[-1])])


def kernel(x, c, w_ada, b_ada, norm1_g, w_in, fox_f_bias, fox_norm_g, hgrn_lb_logits, hgrn_norm_g, pool_w, pool_scale, w_out, norm2_g, router_group_w, router_group_b, router_expert_w, router_expert_b, expert_w_gate, expert_w_up, expert_w_down, final_norm_g):
    bsz, seq, d = x.shape
    depth = w_ada.shape[0]
    fox_heads = fox_f_bias.shape[1]
    fox_dim = fox_heads * HEAD_DIM
    hgrn_dim = hgrn_lb_logits.shape[1]
    pool_dim = pool_scale.shape[1]
    n_fox_pairs = fox_dim // LANES
    n_hgrn_pairs = hgrn_dim // LANES
    t = bsz * seq
    tm = min(512, seq)
    tq = min(512, seq)
    n_tiles = t // MOE_TILE_ROWS + N_CLASSES
    assert n_tiles <= LANES and t % MOE_TILE_ROWS == 0

    o_ff = 3 * fox_dim
    w_in_r = jnp.concatenate(
        [w_in[:, :, :o_ff], w_in[:, :, o_ff + fox_heads:], _pad_lanes(w_in[:, :, o_ff:o_ff + fox_heads])],
        axis=-1).astype(BF16)
    f_bias = _pad_lanes(fox_f_bias)
    groups = pool_w.shape[1]
    pool_bd = jnp.einsum('lgcd,gh->lgchd', pool_w, jnp.eye(groups, dtype=pool_w.dtype)).reshape(
        depth, pool_dim, pool_dim).astype(BF16)
    w_router = _pad_lanes(jnp.concatenate([router_group_w, router_expert_w], axis=-1))
    b_router = _pad_lanes(jnp.concatenate([router_group_b, router_expert_b], axis=-1))
    w_out_b = w_out.astype(BF16)
    wg_b = expert_w_gate.astype(BF16)
    wu_b = expert_w_up.astype(BF16)
    wd_b = expert_w_down.astype(BF16)
    pair_a = jnp.array([0, 0, 0, 1, 1, 2], jnp.int32)
    pair_b = jnp.array([1, 2, 3, 2, 3, 3], jnp.int32)
    final_g = final_norm_g.reshape(1, d)

    mod_all = _ada_call(c, w_ada, b_ada).reshape(depth, bsz, N_MOD, d)
    lower = _lb_call(hgrn_lb_logits)

    for l in range(depth):
        mod = mod_all[l]
        qkv, hg4, pu, ff = _inproj_call(x, norm1_g[l:l + 1], mod, w_in_r[l], 3 * fox_dim, 4 * hgrn_dim,
                                        pool_dim, tm)
        fcol = _fbias_call(ff, f_bias[l:l + 1])
        o_fox = _fox_call(qkv, fcol, fox_norm_g[l].reshape(n_fox_pairs, 1, LANES), n_fox_pairs, tq)
        o_hgrn = _hgrn_call(hg4, lower[l].reshape(n_hgrn_pairs, 1, LANES),
                            hgrn_norm_g[l].reshape(n_hgrn_pairs, 1, LANES), n_hgrn_pairs)
        o_pool = _pool_call(pu, pool_bd[l], pool_scale[l:l + 1])
        x2, h2ext = _outproj_call(o_fox.reshape(t, -1), o_hgrn.reshape(t, -1), o_pool.reshape(t, -1),
                                  x.reshape(t, d), w_out_b[l], mod, norm2_g[l:l + 1],
                                  w_router[l], b_router[l:l + 1], tm, seq)
        pos8, meta = _sort_call(h2ext, d, MOE_TILE_ROWS)
        pos = pos8[0]
        n_used = meta[0, 1].astype(jnp.int32).reshape(1)
        tile = jnp.minimum(jnp.arange(n_tiles, dtype=jnp.int32), n_used[0] - 1)
        tile_cls = jnp.minimum(meta[:, 0].astype(jnp.int32)[tile], N_CLASSES - 1)
        e0 = (tile_cls // N_PAIRS) * EXPERTS_PER_GROUP
        ea = e0 + pair_a[tile_cls % N_PAIRS]
        eb = e0 + pair_b[tile_cls % N_PAIRS]
        h_sorted = _dispatch_call(pos, h2ext, n_tiles * MOE_TILE_ROWS, tm)
        y_sorted = _moe_call(ea, eb, n_used, h_sorted, wg_b[l], wu_b[l], wd_b[l], MOE_TILE_ROWS)
        x = _combine_call(pos, y_sorted, x2, mod, final_g, tm, seq, final=(l == depth - 1)).reshape(bsz, seq, d)
    return x
```

```python
import functools

import jax
import jax.numpy as jnp
from jax import lax
from jax.experimental import pallas as pl
from jax.experimental.pallas import tpu as pltpu

F32 = jnp.float32
BF16 = jnp.bfloat16
HIGHEST = lax.Precision.HIGHEST

HEAD_DIM = 64
LANES = 128
SUBLANES = 8
EPS = 1e-6
N_MOD = 6
POOL_WINDOWS = (2, 4, 8, 16)
N_GROUPS = 4
EXPERTS_PER_GROUP = 4
N_EXPERTS = N_GROUPS * EXPERTS_PER_GROUP
ROUTER_EXPERT_LANE0 = N_GROUPS
N_PAIRS = 6
N_CLASSES = N_GROUPS * N_PAIRS
ROUTE_CLASS_LANE0 = 32
MOE_TILE_ROWS = 256
HGRN_CHUNK = 128
HGRN_LEVELS = (64, 32, 16, 8)
HGRN_DIAG = 8
VMEM_LIMIT = 48 * 1024 * 1024
LOG2_E = 1.4426950408889634
FOX_Q_SCALE = HEAD_DIM ** -0.5 * LOG2_E


def _cparams(sem):
    return pltpu.CompilerParams(dimension_semantics=sem, vmem_limit_bytes=VMEM_LIMIT)


def _silu(x):
    return x * jax.nn.sigmoid(x)


def _iota(shape, dim):
    return lax.broadcasted_iota(jnp.int32, shape, dim)


def _ada_kernel(c_ref, w_ref, b_ref, o_ref):
    sc = _silu(c_ref[...]).astype(BF16)
    o_ref[0] = jnp.dot(sc, w_ref[0].astype(BF16), preferred_element_type=F32) + b_ref[0]


def _ada_call(c, w_ada, b_ada):
    depth, d, n = w_ada.shape
    bsz = c.shape[0]
    tn = 1536
    return pl.pallas_call(
        _ada_kernel,
        out_shape=jax.ShapeDtypeStruct((depth, bsz, n), F32),
        grid=(depth, n // tn),
        in_specs=[
            pl.BlockSpec((bsz, d), lambda l, j: (0, 0)),
            pl.BlockSpec((1, d, tn), lambda l, j: (l, 0, j)),
            pl.BlockSpec((1, 1, tn), lambda l, j: (l, 0, j)),
        ],
        out_specs=pl.BlockSpec((1, bsz, tn), lambda l, j: (l, 0, j)),
        compiler_params=_cparams(("arbitrary", "arbitrary")),
        name="ada_mod",
    )(c, w_ada, b_ada.reshape(depth, 1, n))


def _lb_kernel(x_ref, o_ref):
    x = x_ref[...]
    depth = x.shape[0]
    e = jnp.exp(x - jnp.max(x, axis=0, keepdims=True))
    p = e / jnp.sum(e, axis=0, keepdims=True)
    acc = jnp.zeros_like(p[0:1])
    o_ref[0:1, :] = acc
    for l in range(1, depth):
        acc = acc + p[l:l + 1]
        o_ref[l:l + 1, :] = acc


def _lb_call(lb_logits):
    return pl.pallas_call(
        _lb_kernel,
        out_shape=jax.ShapeDtypeStruct(lb_logits.shape, F32),
        name="hgrn_lower_bounds",
    )(lb_logits)


def _norm_mod(x, g, shift, scale):
    ms = jnp.mean(x * x, axis=-1, keepdims=True)
    return (x * lax.rsqrt(ms + EPS) * g) * (1.0 + scale) + shift


def _inproj_kernel(x_ref, g_ref, mod_ref, w_ref, qkv_ref, hg_ref, pu_ref, ff_ref, *, n_qkv, n_hg, n_pu):
    h = _norm_mod(x_ref[0], g_ref[...], mod_ref[0, 0:1, :], mod_ref[0, 1:2, :]).astype(BF16)
    o0, o1, o2 = n_qkv, n_qkv + n_hg, n_qkv + n_hg + n_pu
    q_scale = jnp.where(_iota((1, n_qkv), 1) < n_qkv // 3, FOX_Q_SCALE, 1.0)
    qkv_ref[0] = (jnp.dot(h, w_ref[:, 0:o0], preferred_element_type=F32) * q_scale).astype(BF16)
    hg_ref[0] = jnp.dot(h, w_ref[:, o0:o1], preferred_element_type=F32)
    pu_ref[0] = jnp.dot(h, w_ref[:, o1:o2], preferred_element_type=F32)
    ff_ref[0] = jnp.dot(h, w_ref[:, o2:], preferred_element_type=F32)


def _inproj_call(x, g, mod, w, n_qkv, n_hg, n_pu, tm):
    bsz, seq, d = x.shape
    n_all = w.shape[1]
    n_ff = n_all - n_qkv - n_hg - n_pu
    return pl.pallas_call(
        functools.partial(_inproj_kernel, n_qkv=n_qkv, n_hg=n_hg, n_pu=n_pu),
        out_shape=(
            jax.ShapeDtypeStruct((bsz, seq, n_qkv), BF16),
            jax.ShapeDtypeStruct((bsz, seq, n_hg), F32),
            jax.ShapeDtypeStruct((bsz, seq, n_pu), F32),
            jax.ShapeDtypeStruct((bsz, seq, n_ff), F32),
        ),
        grid=(bsz, seq // tm),
        in_specs=[
            pl.BlockSpec((1, tm, d), lambda b, i: (b, i, 0)),
            pl.BlockSpec((1, d), lambda b, i: (0, 0)),
            pl.BlockSpec((1, N_MOD, d), lambda b, i: (b, 0, 0)),
            pl.BlockSpec((d, n_all), lambda b, i: (0, 0)),
        ],
        out_specs=(
            pl.BlockSpec((1, tm, n_qkv), lambda b, i: (b, i, 0)),
            pl.BlockSpec((1, tm, n_hg), lambda b, i: (b, i, 0)),
            pl.BlockSpec((1, tm, n_pu), lambda b, i: (b, i, 0)),
            pl.BlockSpec((1, tm, n_ff), lambda b, i: (b, i, 0)),
        ),
        compiler_params=_cparams(("arbitrary", "arbitrary")),
        name="norm_inproj",
    )(x, g, mod, w)


def _fbias_kernel(ff_ref, bias_ref, fcol_ref, *, cb):
    seq = ff_ref.shape[1]
    tri = (_iota((cb, cb), 0) >= _iota((cb, cb), 1)).astype(F32)
    carry = jnp.zeros((1, LANES), F32)
    for blk in range(seq // cb):
        x = ff_ref[0, blk * cb:(blk + 1) * cb, :] + bias_ref[...]
        log_f = jnp.minimum(x, 0.0) - jnp.log1p(jnp.exp(-jnp.abs(x)))
        cs = jnp.dot(tri, log_f, precision=HIGHEST, preferred_element_type=F32) + carry
        fcol_ref[0, blk * cb:(blk + 1) * cb, :] = cs * LOG2_E
        carry = cs[cb - 1:cb, :]


def _fbias_call(ff, bias):
    bsz, seq, _ = ff.shape
    cb = min(256, seq)
    return pl.pallas_call(
        functools.partial(_fbias_kernel, cb=cb),
        out_shape=jax.ShapeDtypeStruct((bsz, seq, LANES), F32),
        grid=(bsz,),
        in_specs=[pl.BlockSpec((1, seq, LANES), lambda b: (b, 0, 0)),
                  pl.BlockSpec((1, LANES), lambda b: (0, 0))],
        out_specs=pl.BlockSpec((1, seq, LANES), lambda b: (b, 0, 0)),
        compiler_params=_cparams(("arbitrary",)),
        name="fox_forget_bias",
    )(ff, bias)


def _split3(f):
    hi = f.astype(BF16).astype(F32)
    rest = f - hi
    mid = rest.astype(BF16).astype(F32)
    return hi, mid, rest - mid


def _fox_kernel(q_ref, k_ref, v_ref, fc_ref, gn_ref, o_ref, k0_sc, k1_sc, v0_sc, v1_sc, *, tq):
    p = pl.program_id(1)
    i = pl.program_id(2)
    seq = k_ref.shape[1]
    lane = _iota((1, LANES), 1)
    h0 = lane < HEAD_DIM
    data = (h0, jnp.logical_not(h0))
    base = (HEAD_DIM, 0)
    k_sc = (k0_sc, k1_sc)
    v_sc = (v0_sc, v1_sc)
    nt = (((1,), (1,)), ((), ()))

    def head_f(fc, which):
        return jnp.sum(jnp.where(lane == 2 * p + which, fc, 0.0), axis=-1, keepdims=True)

    def with_bias_lanes(x, which, first3, last3):
        out = jnp.where(data[which], x, 0.0)
        for n in range(3):
            out = jnp.where(lane == base[which] + n, first3[n], out)
            out = jnp.where(lane == base[which] + 3 + n, last3[n], out)
        return out.astype(BF16)

    @pl.when(i == 0)
    def _():
        for blk in range(seq // tq):
            rows = slice(blk * tq, (blk + 1) * tq)
            fc = fc_ref[0, rows, :]
            kf = k_ref[0, rows, :].astype(F32)
            vf = v_ref[0, rows, :].astype(F32)
            for which in range(2):
                hi, mid, lo = _split3(head_f(fc, which))
                k_sc[which][rows, :] = with_bias_lanes(kf, which, (1.0, 1.0, 1.0), (-hi, -mid, -lo))
                v_sc[which][rows, :] = jnp.where(data[which], vf,
                                                 jnp.where(lane == base[which], 1.0, 0.0)).astype(BF16)

    qf = q_ref[0].astype(F32)
    fcq = fc_ref[0, pl.ds(pl.multiple_of(i * tq, tq), tq), :]
    qs = [with_bias_lanes(qf, which, _split3(head_f(fcq, which)), (1.0, 1.0, 1.0)) for which in range(2)]

    def step(j, carry, masked):
        start = pl.multiple_of(j * tq, tq)
        new = []
        for which in range(2):
            m, acc = carry[which]
            s = lax.dot_general(qs[which], k_sc[which][pl.ds(start, tq), :], nt, preferred_element_type=F32)
            if masked:
                s = jnp.where(_iota((tq, tq), 0) >= _iota((tq, tq), 1), s, -jnp.inf)
            n = jnp.maximum(m, jnp.max(s, axis=-1, keepdims=True))
            pr = jnp.exp2(s - n).astype(BF16)
            acc = jnp.exp2(m - n) * acc + jnp.dot(pr, v_sc[which][pl.ds(start, tq), :],
                                                  preferred_element_type=F32)
            new.append((n, acc))
        return tuple(new)

    init = ((jnp.full((tq, 1), -1e30, F32), jnp.zeros((tq, LANES), F32)),) * 2
    carry = lax.fori_loop(0, i, functools.partial(step, masked=False), init)
    (_, acc0), (_, acc1) = step(i, carry, True)
    l0 = jnp.sum(jnp.where(lane == base[0], acc0, 0.0), axis=-1, keepdims=True)
    l1 = jnp.sum(jnp.where(lane == base[1], acc1, 0.0), axis=-1, keepdims=True)
    o = jnp.where(h0, acc0 / l0, acc1 / l1)
    o2 = o * o
    ms0 = jnp.sum(jnp.where(h0, o2, 0.0), axis=-1, keepdims=True)
    ms1 = jnp.sum(jnp.where(h0, 0.0, o2), axis=-1, keepdims=True)
    ms = jnp.where(h0, ms0, ms1) * (1.0 / HEAD_DIM)
    o_ref[0] = (o * lax.rsqrt(ms + EPS) * gn_ref[0]).astype(BF16)


def _fox_call(qkv, fcol, gn, n_pairs, tq):
    bsz, seq, _ = qkv.shape
    return pl.pallas_call(
        functools.partial(_fox_kernel, tq=tq),
        out_shape=jax.ShapeDtypeStruct((bsz, seq, n_pairs * LANES), BF16),
        grid=(bsz, n_pairs, seq // tq),
        in_specs=[
            pl.BlockSpec((1, tq, LANES), lambda b, p, i: (b, i, p)),
            pl.BlockSpec((1, seq, LANES), lambda b, p, i: (b, 0, n_pairs + p)),
            pl.BlockSpec((1, seq, LANES), lambda b, p, i: (b, 0, 2 * n_pairs + p)),
            pl.BlockSpec((1, seq, LANES), lambda b, p, i: (b, 0, 0)),
            pl.BlockSpec((1, 1, LANES), lambda b, p, i: (p, 0, 0)),
        ],
        out_specs=pl.BlockSpec((1, tq, LANES), lambda b, p, i: (b, i, p)),
        scratch_shapes=[pltpu.VMEM((seq, LANES), BF16)] * 4,
        compiler_params=_cparams(("arbitrary", "arbitrary", "arbitrary")),
        name="fox_attention",
    )(qkv, qkv, qkv, fcol, gn)


def _hgrn_kernel(hq_ref, hf_ref, hi_ref, hg_ref, lb_ref, gn_ref, o_ref, p_sc0, p_sc1):
    ch = HGRN_CHUNK
    seq = hq_ref.shape[1]
    lane = _iota((1, LANES), 1)
    h0 = lane < HEAD_DIM
    r = _iota((ch, ch), 0)
    c = _iota((ch, ch), 1)
    tri = (r >= c).astype(BF16)
    same_head = (r < HEAD_DIM) == (c < HEAD_DIM)
    seg = same_head.astype(BF16)
    level_masks = [(r // (2 * m) == c // (2 * m)) & (r % (2 * m) >= m) & (c % (2 * m) < m) for m in HGRN_LEVELS]
    row8 = _iota((HGRN_DIAG, LANES), 0)
    lb = lb_ref[0]
    nt = (((1,), (1,)), ((), ()))

    def chunk(ci, state_t, p_sc):
        sl = pl.ds(pl.multiple_of(ci * ch, ch), ch)
        f = lb + (1.0 - lb) * jax.nn.sigmoid(hf_ref[0, sl, :])
        g = jnp.log(f)
        k = 1.0 - f
        q = _silu(hq_ref[0, sl, :])
        v = hi_ref[0, sl, :]
        vb = v.astype(BF16)
        g_parts = jnp.concatenate([part.astype(BF16) for part in _split3(g)], axis=1)
        b3 = jnp.dot(tri, g_parts, preferred_element_type=F32)
        b = b3[:, 0:LANES] + b3[:, LANES:2 * LANES] + b3[:, 2 * LANES:]
        b_last = b[ch - 1:ch, :]
        q_head = (jnp.where(h0, q, 0.0), jnp.where(h0, 0.0, q))

        sc = [jnp.zeros((ch, ch), F32), jnp.zeros((ch, ch), F32)]
        for li, m in enumerate(HGRN_LEVELS):
            ref_l = jnp.broadcast_to(b.reshape(ch // (2 * m), 2 * m, LANES)[:, m - 1:m, :],
                                     (ch // (2 * m), 2 * m, LANES)).reshape(ch, LANES)
            e = jnp.exp(-jnp.abs(b - ref_l))
            kt = (k * e).astype(BF16)
            for hd in range(2):
                s = lax.dot_general((q_head[hd] * e).astype(BF16), kt, nt, preferred_element_type=F32)
                sc[hd] = sc[hd] + jnp.where(level_masks[li], s, 0.0)
        o = jnp.where(h0,
                      jnp.dot(sc[0].astype(BF16), vb, preferred_element_type=F32),
                      jnp.dot(sc[1].astype(BF16), vb, preferred_element_type=F32))

        nd = ch // HGRN_DIAG
        for blk in range(nd):
            t0 = blk * HGRN_DIAG
            qb = q[t0:t0 + HGRN_DIAG]
            bb = b[t0:t0 + HGRN_DIAG]
            for j in range(HGRN_DIAG):
                kj = k[t0 + j:t0 + j + 1]
                bj = b[t0 + j:t0 + j + 1]
                decay = jnp.exp(jnp.where(row8 >= j, bb - bj, -jnp.inf))
                p_sc[(t0 + j) * HGRN_DIAG:(t0 + j + 1) * HGRN_DIAG, :] = qb * kj * decay
        w = jnp.dot(p_sc[...].astype(BF16), seg, preferred_element_type=F32)
        diag_rows = []
        for blk in range(nd):
            t0 = blk * HGRN_DIAG
            od = jnp.zeros((HGRN_DIAG, LANES), F32)
            for j in range(HGRN_DIAG):
                od = od + w[(t0 + j) * HGRN_DIAG:(t0 + j + 1) * HGRN_DIAG] * v[t0 + j:t0 + j + 1]
            diag_rows.append(od)
        o = o + jnp.concatenate(diag_rows, axis=0)

        o = o + lax.dot_general((q * jnp.exp(b)).astype(BF16), state_t.astype(BF16), nt,
                                preferred_element_type=F32)

        k_dec = (k * jnp.exp(b_last - b)).astype(BF16)
        upd_t = jnp.dot(v.T.astype(BF16), k_dec, preferred_element_type=F32)
        state_t = jnp.exp(b_last) * state_t + jnp.where(same_head, upd_t, 0.0)

        o2 = o * o
        ms0 = jnp.sum(jnp.where(h0, o2, 0.0), axis=-1, keepdims=True)
        ms1 = jnp.sum(jnp.where(h0, 0.0, o2), axis=-1, keepdims=True)
        ms = jnp.where(h0, ms0, ms1) * (1.0 / HEAD_DIM)
        out = o * lax.rsqrt(ms + EPS) * gn_ref[0] * _silu(hg_ref[0, sl, :])
        o_ref[0, sl, :] = out.astype(BF16)
        return state_t

    def two_chunks(cc, state_t):
        return chunk(2 * cc + 1, chunk(2 * cc, state_t, p_sc0), p_sc1)

    lax.fori_loop(0, seq // (2 * ch), two_chunks, jnp.zeros((LANES, LANES), F32))


def _hgrn_call(hg4, lb, gn, n_pairs):
    bsz, seq, _ = hg4.shape

    def spec(off):
        return pl.BlockSpec((1, seq, LANES), lambda b, p: (b, 0, off + p))

    return pl.pallas_call(
        _hgrn_kernel,
        out_shape=jax.ShapeDtypeStruct((bsz, seq, n_pairs * LANES), BF16),
        grid=(bsz, n_pairs),
        in_specs=[spec(0), spec(n_pairs), spec(2 * n_pairs), spec(3 * n_pairs),
                  pl.BlockSpec((1, 1, LANES), lambda b, p: (p, 0, 0)),
                  pl.BlockSpec((1, 1, LANES), lambda b, p: (p, 0, 0))],
        out_specs=pl.BlockSpec((1, seq, LANES), lambda b, p: (b, 0, p)),
        scratch_shapes=[pltpu.VMEM((HGRN_CHUNK * HGRN_DIAG, LANES), F32)] * 2,
        compiler_params=_cparams(("arbitrary", "arbitrary")),
        name="hgrn2",
    )(hg4, hg4, hg4, hg4, lb, gn)


def _pool_kernel(u_ref, w_ref, s_ref, o_ref):
    u = u_ref[0]
    seq, n = u.shape
    t = _iota((seq, 1), 0)
    lane = _iota((1, n), 1)

    def shifted(x, k):
        return jnp.where(t >= k, pltpu.roll(x, k, axis=0), 0.0)

    sums = []
    s = u
    for w in POOL_WINDOWS:
        s = s + shifted(s, w // 2)
        sums.append(s)
    pos1 = (t + 1).astype(F32)
    group = len(POOL_WINDOWS) - 1
    mean = sums[group] / jnp.minimum(pos1, float(POOL_WINDOWS[group]))
    group_dim = n // len(POOL_WINDOWS)
    for gi in range(group - 1, -1, -1):
        mean = jnp.where(lane < (gi + 1) * group_dim,
                         sums[gi] / jnp.minimum(pos1, float(POOL_WINDOWS[gi])), mean)
    pooled = (mean - u).astype(BF16)
    o_ref[0] = (jnp.dot(pooled, w_ref[...], preferred_element_type=F32) * s_ref[...]).astype(BF16)


def _pool_call(pu, w_bd, scale):
    bsz, seq, n = pu.shape
    return pl.pallas_call(
        _pool_kernel,
        out_shape=jax.ShapeDtypeStruct((bsz, seq, n), BF16),
        grid=(bsz,),
        in_specs=[pl.BlockSpec((1, seq, n), lambda b: (b, 0, 0)),
                  pl.BlockSpec((n, n), lambda b: (0, 0)),
                  pl.BlockSpec((1, n), lambda b: (0, 0))],
        out_specs=pl.BlockSpec((1, seq, n), lambda b: (b, 0, 0)),
        compiler_params=_cparams(("arbitrary",)),
        name="multiscale_pool",
    )(pu, w_bd, scale)


def _route(logits):
    lane = _iota(logits.shape, 1)
    lane_f = lane.astype(F32)
    big = float(LANES)
    is_g = lane < N_GROUPS
    gl = jnp.where(is_g, logits, -jnp.inf)
    gmax = jnp.max(gl, axis=-1, keepdims=True)
    gsum = jnp.sum(jnp.where(is_g, jnp.exp(gl - gmax), 0.0), axis=-1, keepdims=True)
    p_group = 1.0 / gsum
    g_sel = jnp.min(jnp.where(gl == gmax, lane_f, big), axis=-1, keepdims=True)
    lo = ROUTER_EXPERT_LANE0 + EXPERTS_PER_GROUP * g_sel
    in_group = (lane_f >= lo) & (lane_f < lo + EXPERTS_PER_GROUP)
    el = jnp.where(in_group, logits, -jnp.inf)
    m1 = jnp.max(el, axis=-1, keepdims=True)
    i1 = jnp.min(jnp.where(el == m1, lane_f, big), axis=-1, keepdims=True)
    el2 = jnp.where(lane_f == i1, -jnp.inf, el)
    m2 = jnp.max(el2, axis=-1, keepdims=True)
    i2 = jnp.min(jnp.where(el2 == m2, lane_f, big), axis=-1, keepdims=True)
    e2 = jnp.exp(m2 - m1)
    w1 = p_group / (1.0 + e2)
    w2 = p_group * e2 / (1.0 + e2)
    first_low = i1 < i2
    a = jnp.minimum(i1, i2) - lo
    b = jnp.maximum(i1, i2) - lo
    pair = jnp.where(a == 0.0, b - 1.0, jnp.where(a == 1.0, b + 1.0, float(N_PAIRS - 1)))
    cls_lane = ROUTE_CLASS_LANE0 + g_sel * N_PAIRS + pair
    return (jnp.where(lane == 0, jnp.where(first_low, w1, w2), 0.0)
            + jnp.where(lane == 1, jnp.where(first_low, w2, w1), 0.0)
            + jnp.where(lane_f == cls_lane, 1.0, 0.0))


def _outproj_kernel(of_ref, oh_ref, op_ref, x_ref, w_ref, mod_ref, g_ref, wr_ref, br_ref,
                    xn_ref, h2_ref, *, n_fox, n_hgrn):
    d = x_ref.shape[-1]
    y = jnp.dot(of_ref[...], w_ref[0:n_fox, :], preferred_element_type=F32)
    y = y + jnp.dot(oh_ref[...], w_ref[n_fox:n_fox + n_hgrn, :], preferred_element_type=F32)
    y = y + jnp.dot(op_ref[...], w_ref[n_fox + n_hgrn:, :], preferred_element_type=F32)
    xn = x_ref[...] + (1.0 + mod_ref[0, 2:3, :]) * y
    xn_ref[...] = xn
    h2 = _norm_mod(xn, g_ref[...], mod_ref[0, 3:4, :], mod_ref[0, 4:5, :])
    h2_ref[:, 0:d] = h2
    h_hi = h2.astype(BF16)
    h_lo = (h2 - h_hi.astype(F32)).astype(BF16)
    logits = (jnp.dot(h_hi, wr_ref[0], preferred_element_type=F32)
              + jnp.dot(h_lo, wr_ref[0], preferred_element_type=F32)
              + jnp.dot(h_hi, wr_ref[1], preferred_element_type=F32)) + br_ref[...]
    h2_ref[:, d:] = _route(logits)


def _outproj_call(o_fox, o_hgrn, o_pool, x, w_out, mod, g2, w_router, b_router, tm, seq):
    t, d = x.shape
    n_fox, n_hgrn, n_pool = o_fox.shape[-1], o_hgrn.shape[-1], o_pool.shape[-1]
    per_seq = seq // tm
    return pl.pallas_call(
        functools.partial(_outproj_kernel, n_fox=n_fox, n_hgrn=n_hgrn),
        out_shape=(jax.ShapeDtypeStruct((t, d), F32),
                   jax.ShapeDtypeStruct((t, d + LANES), F32)),
        grid=(t // tm,),
        in_specs=[
            pl.BlockSpec((tm, n_fox), lambda i: (i, 0)),
            pl.BlockSpec((tm, n_hgrn), lambda i: (i, 0)),
            pl.BlockSpec((tm, n_pool), lambda i: (i, 0)),
            pl.BlockSpec((tm, d), lambda i: (i, 0)),
            pl.BlockSpec((d, d), lambda i: (0, 0)),
            pl.BlockSpec((1, N_MOD, d), lambda i: (i // per_seq, 0, 0)),
            pl.BlockSpec((1, d), lambda i: (0, 0)),
            pl.BlockSpec((2, d, LANES), lambda i: (0, 0, 0)),
            pl.BlockSpec((1, LANES), lambda i: (0, 0)),
        ],
        out_specs=(pl.BlockSpec((tm, d), lambda i: (i, 0)),
                   pl.BlockSpec((tm, d + LANES), lambda i: (i, 0))),
        compiler_params=_cparams(("arbitrary",)),
        name="outproj_router",
    )(o_fox, o_hgrn, o_pool, x, w_out, mod, g2, w_router, b_router)


def _sort_kernel(route_ref, pos_ref, meta_ref, *, tile_rows, cb):
    t = route_ref.shape[0]
    lane = _iota((1, LANES), 1)
    is_cls = (lane >= ROUTE_CLASS_LANE0) & (lane < ROUTE_CLASS_LANE0 + N_CLASSES)
    nblk = t // cb
    zero_row = jnp.zeros((1, LANES), F32)

    def onehot(i):
        return jnp.where(is_cls, route_ref[pl.ds(pl.multiple_of(i * cb, cb), cb), :], 0.0)

    counts = lax.fori_loop(0, nblk, lambda i, acc: acc + jnp.sum(onehot(i), axis=0, keepdims=True), zero_row)
    padded = jnp.floor((counts + float(tile_rows - 1)) * (1.0 / tile_rows)) * float(tile_rows)
    before = (_iota((LANES, LANES), 0) < _iota((LANES, LANES), 1)).astype(F32)
    offs = jnp.dot(jnp.broadcast_to(padded, (8, LANES)), before, precision=HIGHEST,
                   preferred_element_type=F32)[0:1]
    ends = offs + padded
    strict = (_iota((cb, cb), 0) > _iota((cb, cb), 1)).astype(BF16)
    ones8 = jnp.ones((8, LANES), F32)
    nt = (((1,), (1,)), ((), ()))

    def place(i, seen):
        oh = onehot(i)
        rank = jnp.dot(strict, oh.astype(BF16), preferred_element_type=F32) + seen
        dest = oh * (rank + offs)
        rows = lax.dot_general(ones8, dest, nt, precision=HIGHEST, preferred_element_type=F32)
        pos_ref[:, pl.ds(pl.multiple_of(i * cb, cb), cb)] = rows.astype(jnp.int32)
        return seen + jnp.sum(oh, axis=0, keepdims=True)

    lax.fori_loop(0, nblk, place, zero_row)
    tile_start = _iota((LANES, LANES), 0).astype(F32) * float(tile_rows)
    tile_cls = jnp.sum(jnp.where(is_cls & (ends <= tile_start), 1.0, 0.0), axis=-1, keepdims=True)
    n_used = jnp.sum(jnp.where(lane == ROUTE_CLASS_LANE0 + N_CLASSES - 1, ends, 0.0), axis=-1,
                     keepdims=True) * (1.0 / tile_rows)
    meta_ref[...] = jnp.where(lane == 0, tile_cls, jnp.where(lane == 1, n_used, 0.0))


def _sort_call(h2ext, d, tile_rows):
    t = h2ext.shape[0]
    cb = min(512, t)
    return pl.pallas_call(
        functools.partial(_sort_kernel, tile_rows=tile_rows, cb=cb),
        out_shape=(jax.ShapeDtypeStruct((8, t), jnp.int32),
                   jax.ShapeDtypeStruct((LANES, LANES), F32)),
        grid=(1,),
        in_specs=[pl.BlockSpec((t, LANES), lambda i: (0, d // LANES))],
        out_specs=(pl.BlockSpec((8, t), lambda i: (0, 0)),
                   pl.BlockSpec((LANES, LANES), lambda i: (0, 0))),
        compiler_params=_cparams(("arbitrary",)),
        name="route_sort",
    )(h2ext)


def _dispatch_kernel(pos_ref, h_ref, init_ref, out_ref, sem, *, tm):
    del init_ref
    base = pl.program_id(0) * tm

    def issue(g, carry):
        for u in range(SUBLANES):
            dst = pos_ref[base + g * SUBLANES + u]
            pltpu.make_async_copy(h_ref.at[g, pl.ds(u, 1), :], out_ref.at[pl.ds(dst, 1), :], sem).start()
        return carry

    lax.fori_loop(0, tm // SUBLANES, issue, 0)
    pltpu.make_async_copy(h_ref, h_ref, sem).wait()


def _dispatch_call(pos, h2ext, n_rows, tm):
    t, w = h2ext.shape
    return pl.pallas_call(
        functools.partial(_dispatch_kernel, tm=tm),
        out_shape=jax.ShapeDtypeStruct((n_rows, w), F32),
        grid_spec=pltpu.PrefetchScalarGridSpec(
            num_scalar_prefetch=1,
            grid=(t // tm,),
            in_specs=[pl.BlockSpec((tm // SUBLANES, SUBLANES, w), lambda i, pos: (i, 0, 0)),
                      pl.BlockSpec(memory_space=pl.ANY)],
            out_specs=pl.BlockSpec(memory_space=pl.ANY),
            scratch_shapes=[pltpu.SemaphoreType.DMA],
        ),
        input_output_aliases={2: 0},
        compiler_params=_cparams(("arbitrary",)),
        name="moe_dispatch",
    )(pos, h2ext.reshape(t // SUBLANES, SUBLANES, w), jnp.zeros((n_rows, w), F32))


def _moe_kernel(ea_ref, eb_ref, nu_ref, h_ref, wga_ref, wua_ref, wda_ref, wgb_ref, wub_ref, wdb_ref, o_ref):
    del ea_ref, eb_ref
    d = o_ref.shape[-1]

    @pl.when(pl.program_id(0) < nu_ref[0])
    def _():
        h = h_ref[:, 0:d].astype(BF16)
        tail = h_ref[:, d:]
        lane = _iota((1, LANES), 1)
        w_a = jnp.sum(jnp.where(lane == 0, tail, 0.0), axis=-1, keepdims=True)
        w_b = jnp.sum(jnp.where(lane == 1, tail, 0.0), axis=-1, keepdims=True)

        def expert(wg_ref, wu_ref, wd_ref):
            hidden = _silu(jnp.dot(h, wg_ref[0], preferred_element_type=F32)) * jnp.dot(
                h, wu_ref[0], preferred_element_type=F32)
            return jnp.dot(hidden.astype(BF16), wd_ref[0], preferred_element_type=F32)

        o_ref[...] = w_a * expert(wga_ref, wua_ref, wda_ref) + w_b * expert(wgb_ref, wub_ref, wdb_ref)

    @pl.when(pl.program_id(0) >= nu_ref[0])
    def _():
        o_ref[...] = jnp.zeros_like(o_ref)


def _moe_call(ea, eb, n_used, h_sorted, wg, wu, wd, tile_rows):
    n_rows, _ = h_sorted.shape
    w = h_sorted.shape[1]
    _, d, f = wg.shape

    def rows(j, ea, eb, nu):
        return (jnp.maximum(jnp.minimum(j, nu[0] - 1), 0), 0)

    def wa(j, ea, eb, nu):
        return (ea[j], 0, 0)

    def wb(j, ea, eb, nu):
        return (eb[j], 0, 0)

    return pl.pallas_call(
        _moe_kernel,
        out_shape=jax.ShapeDtypeStruct((n_rows, d), F32),
        grid_spec=pltpu.PrefetchScalarGridSpec(
            num_scalar_prefetch=3,
            grid=(n_rows // tile_rows,),
            in_specs=[pl.BlockSpec((tile_rows, w), rows),
                      pl.BlockSpec((1, d, f), wa), pl.BlockSpec((1, d, f), wa), pl.BlockSpec((1, f, d), wa),
                      pl.BlockSpec((1, d, f), wb), pl.BlockSpec((1, d, f), wb), pl.BlockSpec((1, f, d), wb)],
            out_specs=pl.BlockSpec((tile_rows, d), lambda j, ea, eb, nu: (j, 0)),
        ),
        compiler_params=_cparams(("arbitrary",)),
        name="moe_experts",
    )(ea, eb, n_used, h_sorted, wg, wu, wd, wg, wu, wd)


def _combine_kernel(pos_ref, y_ref, x_ref, mod_ref, fg_ref, o_ref, buf, sem, *, tm, final):
    i = pl.program_id(0)

    def issue(tile, slot):
        def eight(g, carry):
            for u in range(SUBLANES):
                src = pos_ref[tile * tm + g * SUBLANES + u]
                pltpu.make_async_copy(y_ref.at[pl.ds(src, 1), :], buf.at[slot, g, pl.ds(u, 1), :],
                                      sem.at[slot]).start()
            return carry
        lax.fori_loop(0, tm // SUBLANES, eight, 0)

    @pl.when(i == 0)
    def _():
        issue(0, 0)

    @pl.when(i + 1 < pl.num_programs(0))
    def _():
        issue(i + 1, (i + 1) % 2)

    slot = i % 2
    pltpu.make_async_copy(buf.at[slot], buf.at[slot], sem.at[slot]).wait()
    out = x_ref[...] + (1.0 + mod_ref[0, 5:6, :]) * buf[slot].reshape(x_ref.shape)
    if final:
        ms = jnp.mean(out * out, axis=-1, keepdims=True)
        out = out * lax.rsqrt(ms + EPS) * fg_ref[...]
    o_ref[...] = out


def _combine_call(pos, y_sorted, x, mod, final_g, tm, seq, final):
    t, d = x.shape
    per_seq = seq // tm
    return pl.pallas_call(
        functools.partial(_combine_kernel, tm=tm, final=final),
        out_shape=jax.ShapeDtypeStruct((t, d), F32),
        grid_spec=pltpu.PrefetchScalarGridSpec(
            num_scalar_prefetch=1,
            grid=(t // tm,),
            in_specs=[pl.BlockSpec(memory_space=pl.ANY),
                      pl.BlockSpec((tm, d), lambda i, pos: (i, 0)),
                      pl.BlockSpec((1, N_MOD, d), lambda i, pos: (i // per_seq, 0, 0)),
                      pl.BlockSpec((1, d), lambda i, pos: (0, 0))],
            out_specs=pl.BlockSpec((tm, d), lambda i, pos: (i, 0)),
            scratch_shapes=[pltpu.VMEM((2, tm // SUBLANES, SUBLANES, d), F32), pltpu.SemaphoreType.DMA((2,))],
        ),
        compiler_params=_cparams(("arbitrary",)),
        name="moe_combine",
    )(pos, y_sorted, x, mod, final_g)


def _pad_lanes(a, n=LANES):
    return jnp.pad(a, [(0, 0)] * (a.ndim - 1) + [(0, n - a.shape[-1])])


def kernel(x, c, w_ada, b_ada, norm1_g, w_in, fox_f_bias, fox_norm_g, hgrn_lb_logits, hgrn_norm_g, pool_w, pool_scale, w_out, norm2_g, router_group_w, router_group_b, router_expert_w, router_expert_b, expert_w_gate, expert_w_up, expert_w_down, final_norm_g):
    bsz, seq, d = x.shape
    depth = w_ada.shape[0]
    fox_heads = fox_f_bias.shape[1]
    fox_dim = fox_heads * HEAD_DIM
    hgrn_dim = hgrn_lb_logits.shape[1]
    pool_dim = pool_scale.shape[1]
    n_fox_pairs = fox_dim // LANES
    n_hgrn_pairs = hgrn_dim // LANES
    t = bsz * seq
    tm = min(512, seq)
    tq = min(512, seq)
    n_tiles = t // MOE_TILE_ROWS + N_CLASSES
    assert n_tiles <= LANES and t % MOE_TILE_ROWS == 0

    o_ff = 3 * fox_dim
    w_in_r = jnp.concatenate(
        [w_in[:, :, :o_ff], w_in[:, :, o_ff + fox_heads:], _pad_lanes(w_in[:, :, o_ff:o_ff + fox_heads])],
        axis=-1).astype(BF16)
    f_bias = _pad_lanes(fox_f_bias)
    groups = pool_w.shape[1]
    pool_bd = jnp.einsum('lgcd,gh->lgchd', pool_w, jnp.eye(groups, dtype=pool_w.dtype)).reshape(
        depth, pool_dim, pool_dim).astype(BF16)
    w_router = _pad_lanes(jnp.concatenate([router_group_w, router_expert_w], axis=-1))
    w_router_hi = w_router.astype(BF16)
    w_router = jnp.stack([w_router_hi, (w_router - w_router_hi.astype(F32)).astype(BF16)], axis=1)
    b_router = _pad_lanes(jnp.concatenate([router_group_b, router_expert_b], axis=-1))
    w_out_b = w_out.astype(BF16)
    wg_b = expert_w_gate.astype(BF16)
    wu_b = expert_w_up.astype(BF16)
    wd_b = expert_w_down.astype(BF16)
    pair_a = jnp.array([0, 0, 0, 1, 1, 2], jnp.int32)
    pair_b = jnp.array([1, 2, 3, 2, 3, 3], jnp.int32)
    final_g = final_norm_g.reshape(1, d)

    mod_all = _ada_call(c, w_ada, b_ada).reshape(depth, bsz, N_MOD, d)
    lower = _lb_call(hgrn_lb_logits)

    for l in range(depth):
        mod = mod_all[l]
        qkv, hg4, pu, ff = _inproj_call(x, norm1_g[l:l + 1], mod, w_in_r[l], 3 * fox_dim, 4 * hgrn_dim,
                                        pool_dim, tm)
        fcol = _fbias_call(ff, f_bias[l:l + 1])
        o_fox = _fox_call(qkv, fcol, fox_norm_g[l].reshape(n_fox_pairs, 1, LANES), n_fox_pairs, tq)
        o_hgrn = _hgrn_call(hg4, lower[l].reshape(n_hgrn_pairs, 1, LANES),
                            hgrn_norm_g[l].reshape(n_hgrn_pairs, 1, LANES), n_hgrn_pairs)
        o_pool = _pool_call(pu, pool_bd[l], pool_scale[l:l + 1])
        x2, h2ext = _outproj_call(o_fox.reshape(t, -1), o_hgrn.reshape(t, -1), o_pool.reshape(t, -1),
                                  x.reshape(t, d), w_out_b[l], mod, norm2_g[l:l + 1],
                                  w_router[l], b_router[l:l + 1], tm, seq)
        pos8, meta = _sort_call(h2ext, d, MOE_TILE_ROWS)
        pos = pos8[0]
        n_used = meta[0, 1].astype(jnp.int32).reshape(1)
        tile = jnp.minimum(jnp.arange(n_tiles, dtype=jnp.int32), n_used[0] - 1)
        tile_cls = jnp.minimum(meta[:, 0].astype(jnp.int32)[tile], N_CLASSES - 1)
        e0 = (tile_cls // N_PAIRS) * EXPERTS_PER_GROUP
        ea = e0 + pair_a[tile_cls % N_PAIRS]
        eb = e0 + pair_b[tile_cls % N_PAIRS]
        h_sorted = _dispatch_call(pos, h2ext, n_tiles * MOE_TILE_ROWS, tm)
        y_sorted = _moe_call(ea, eb, n_used, h_sorted, wg_b[l], wu_b[l], wd_b[l], MOE_TILE_ROWS)
        x = _combine_call(pos, y_sorted, x2, mod, final_g, tm, seq, final=(l == depth - 1)).reshape(bsz, seq, d)
    return x
```

```python
import functools

import jax
import jax.numpy as jnp
from jax import lax
from jax.experimental import pallas as pl
from jax.experimental.pallas import tpu as pltpu

F32 = jnp.float32
BF16 = jnp.bfloat16
HIGHEST = lax.Precision.HIGHEST

HEAD_DIM = 64
LANES = 128
SUBLANES = 8
EPS = 1e-6
N_MOD = 6
POOL_WINDOWS = (2, 4, 8, 16)
N_GROUPS = 4
EXPERTS_PER_GROUP = 4
N_EXPERTS = N_GROUPS * EXPERTS_PER_GROUP
ROUTER_EXPERT_LANE0 = N_GROUPS
N_PAIRS = 6
N_CLASSES = N_GROUPS * N_PAIRS
ROUTE_CLASS_LANE0 = 32
MOE_TILE_ROWS = 256
HGRN_CHUNK = 128
HGRN_LEVELS = (64, 32, 16, 8, 4, 2, 1)
HGRN_GROUP = 4
VMEM_LIMIT = 48 * 1024 * 1024
LOG2_E = 1.4426950408889634
FOX_Q_SCALE = HEAD_DIM ** -0.5 * LOG2_E


def _cparams(sem):
    return pltpu.CompilerParams(dimension_semantics=sem, vmem_limit_bytes=VMEM_LIMIT)


def _silu(x):
    return x * jax.nn.sigmoid(x)


def _iota(shape, dim):
    return lax.broadcasted_iota(jnp.int32, shape, dim)


def _ada_kernel(c_ref, w_ref, b_ref, o_ref):
    sc = _silu(c_ref[...]).astype(BF16)
    o_ref[0] = jnp.dot(sc, w_ref[0].astype(BF16), preferred_element_type=F32) + b_ref[0]


def _ada_call(c, w_ada, b_ada):
    depth, d, n = w_ada.shape
    bsz = c.shape[0]
    tn = 1536
    return pl.pallas_call(
        _ada_kernel,
        out_shape=jax.ShapeDtypeStruct((depth, bsz, n), F32),
        grid=(depth, n // tn),
        in_specs=[
            pl.BlockSpec((bsz, d), lambda l, j: (0, 0)),
            pl.BlockSpec((1, d, tn), lambda l, j: (l, 0, j)),
            pl.BlockSpec((1, 1, tn), lambda l, j: (l, 0, j)),
        ],
        out_specs=pl.BlockSpec((1, bsz, tn), lambda l, j: (l, 0, j)),
        compiler_params=_cparams(("arbitrary", "arbitrary")),
        name="ada_mod",
    )(c, w_ada, b_ada.reshape(depth, 1, n))


def _lb_kernel(x_ref, o_ref):
    x = x_ref[...]
    depth = x.shape[0]
    e = jnp.exp(x - jnp.max(x, axis=0, keepdims=True))
    p = e / jnp.sum(e, axis=0, keepdims=True)
    acc = jnp.zeros_like(p[0:1])
    o_ref[0:1, :] = acc
    for l in range(1, depth):
        acc = acc + p[l:l + 1]
        o_ref[l:l + 1, :] = acc


def _lb_call(lb_logits):
    return pl.pallas_call(
        _lb_kernel,
        out_shape=jax.ShapeDtypeStruct(lb_logits.shape, F32),
        name="hgrn_lower_bounds",
    )(lb_logits)


def _norm_mod(x, g, shift, scale):
    ms = jnp.mean(x * x, axis=-1, keepdims=True)
    return (x * lax.rsqrt(ms + EPS) * g) * (1.0 + scale) + shift


def _inproj_kernel(x_ref, g_ref, mod_ref, wq_ref, wh_ref, wf_ref, qkv_ref, hg_ref, pu_ref, ff_ref, *, n_hg):
    h = _norm_mod(x_ref[0], g_ref[...], mod_ref[0, 0:1, :], mod_ref[0, 1:2, :]).astype(BF16)
    n_qkv = qkv_ref.shape[-1]
    q_scale = jnp.where(_iota((1, n_qkv), 1) < n_qkv // 3, FOX_Q_SCALE, 1.0)
    qkv_ref[0] = (jnp.dot(h, wq_ref[...], preferred_element_type=F32) * q_scale).astype(BF16)
    hg_ref[0] = jnp.dot(h, wh_ref[:, 0:n_hg], preferred_element_type=F32)
    pu_ref[0] = jnp.dot(h, wh_ref[:, n_hg:], preferred_element_type=F32)
    ff_ref[0] = jnp.dot(h, wf_ref[...], preferred_element_type=F32)


def _inproj_call(x, g, mod, w_qkv, w_hp, w_ff, n_hg, tm):
    bsz, seq, d = x.shape
    n_qkv, n_ff = w_qkv.shape[1], w_ff.shape[1]
    n_pu = w_hp.shape[1] - n_hg
    return pl.pallas_call(
        functools.partial(_inproj_kernel, n_hg=n_hg),
        out_shape=(
            jax.ShapeDtypeStruct((bsz, seq, n_qkv), BF16),
            jax.ShapeDtypeStruct((bsz, seq, n_hg), F32),
            jax.ShapeDtypeStruct((bsz, seq, n_pu), F32),
            jax.ShapeDtypeStruct((bsz, seq, n_ff), F32),
        ),
        grid=(bsz, seq // tm),
        in_specs=[
            pl.BlockSpec((1, tm, d), lambda b, i: (b, i, 0)),
            pl.BlockSpec((1, d), lambda b, i: (0, 0)),
            pl.BlockSpec((1, N_MOD, d), lambda b, i: (b, 0, 0)),
            pl.BlockSpec((d, n_qkv), lambda b, i: (0, 0)),
            pl.BlockSpec((d, n_hg + n_pu), lambda b, i: (0, 0)),
            pl.BlockSpec((d, n_ff), lambda b, i: (0, 0)),
        ],
        out_specs=(
            pl.BlockSpec((1, tm, n_qkv), lambda b, i: (b, i, 0)),
            pl.BlockSpec((1, tm, n_hg), lambda b, i: (b, i, 0)),
            pl.BlockSpec((1, tm, n_pu), lambda b, i: (b, i, 0)),
            pl.BlockSpec((1, tm, n_ff), lambda b, i: (b, i, 0)),
        ),
        compiler_params=_cparams(("arbitrary", "arbitrary")),
        name="norm_inproj",
    )(x, g, mod, w_qkv, w_hp, w_ff)


def _fbias_kernel(ff_ref, bias_ref, fcol_ref, *, cb):
    seq = ff_ref.shape[1]
    tri = (_iota((cb, cb), 0) >= _iota((cb, cb), 1)).astype(F32)
    carry = jnp.zeros((1, LANES), F32)
    for blk in range(seq // cb):
        x = ff_ref[0, blk * cb:(blk + 1) * cb, :] + bias_ref[...]
        log_f = jnp.minimum(x, 0.0) - jnp.log1p(jnp.exp(-jnp.abs(x)))
        cs = jnp.dot(tri, log_f, precision=HIGHEST, preferred_element_type=F32) + carry
        fcol_ref[0, blk * cb:(blk + 1) * cb, :] = cs * LOG2_E
        carry = cs[cb - 1:cb, :]


def _fbias_call(ff, bias):
    bsz, seq, _ = ff.shape
    cb = min(256, seq)
    return pl.pallas_call(
        functools.partial(_fbias_kernel, cb=cb),
        out_shape=jax.ShapeDtypeStruct((bsz, seq, LANES), F32),
        grid=(bsz,),
        in_specs=[pl.BlockSpec((1, seq, LANES), lambda b: (b, 0, 0)),
                  pl.BlockSpec((1, LANES), lambda b: (0, 0))],
        out_specs=pl.BlockSpec((1, seq, LANES), lambda b: (b, 0, 0)),
        compiler_params=_cparams(("arbitrary",)),
        name="fox_forget_bias",
    )(ff, bias)


def _split3(f):
    hi = f.astype(BF16).astype(F32)
    rest = f - hi
    mid = rest.astype(BF16).astype(F32)
    return hi, mid, rest - mid


def _fox_kernel(q_ref, k_ref, v_ref, fc_ref, gn_ref, o_ref, k0_sc, k1_sc, v0_sc, v1_sc, *, tq):
    p = pl.program_id(1)
    i = pl.program_id(2)
    seq = k_ref.shape[1]
    lane = _iota((1, LANES), 1)
    h0 = lane < HEAD_DIM
    data = (h0, jnp.logical_not(h0))
    base = (HEAD_DIM, 0)
    k_sc = (k0_sc, k1_sc)
    v_sc = (v0_sc, v1_sc)
    nt = (((1,), (1,)), ((), ()))

    def head_f(fc, which):
        return jnp.sum(jnp.where(lane == 2 * p + which, fc, 0.0), axis=-1, keepdims=True)

    def with_bias_lanes(x, which, first3, last3):
        out = jnp.where(data[which], x, 0.0)
        for n in range(3):
            out = jnp.where(lane == base[which] + n, first3[n], out)
            out = jnp.where(lane == base[which] + 3 + n, last3[n], out)
        return out.astype(BF16)

    @pl.when(i == 0)
    def _():
        for blk in range(seq // tq):
            rows = slice(blk * tq, (blk + 1) * tq)
            fc = fc_ref[0, rows, :]
            kf = k_ref[0, rows, :].astype(F32)
            vf = v_ref[0, rows, :].astype(F32)
            for which in range(2):
                hi, mid, lo = _split3(head_f(fc, which))
                k_sc[which][rows, :] = with_bias_lanes(kf, which, (1.0, 1.0, 1.0), (-hi, -mid, -lo))
                v_sc[which][rows, :] = jnp.where(data[which], vf,
                                                 jnp.where(lane == base[which], 1.0, 0.0)).astype(BF16)

    qf = q_ref[0].astype(F32)
    fcq = fc_ref[0, pl.ds(pl.multiple_of(i * tq, tq), tq), :]
    qs = [with_bias_lanes(qf, which, _split3(head_f(fcq, which)), (1.0, 1.0, 1.0)) for which in range(2)]

    def step(j, carry, masked):
        start = pl.multiple_of(j * tq, tq)
        new = []
        for which in range(2):
            m, acc = carry[which]
            s = lax.dot_general(qs[which], k_sc[which][pl.ds(start, tq), :], nt, preferred_element_type=F32)
            if masked:
                s = jnp.where(_iota((tq, tq), 0) >= _iota((tq, tq), 1), s, -jnp.inf)
            n = jnp.maximum(m, jnp.max(s, axis=-1, keepdims=True))
            pr = jnp.exp2(s - n).astype(BF16)
            acc = jnp.exp2(m - n) * acc + jnp.dot(pr, v_sc[which][pl.ds(start, tq), :],
                                                  preferred_element_type=F32)
            new.append((n, acc))
        return tuple(new)

    init = ((jnp.full((tq, 1), -1e30, F32), jnp.zeros((tq, LANES), F32)),) * 2
    carry = lax.fori_loop(0, i, functools.partial(step, masked=False), init)
    (_, acc0), (_, acc1) = step(i, carry, True)
    l0 = jnp.sum(jnp.where(lane == base[0], acc0, 0.0), axis=-1, keepdims=True)
    l1 = jnp.sum(jnp.where(lane == base[1], acc1, 0.0), axis=-1, keepdims=True)
    o = jnp.where(h0, acc0 / l0, acc1 / l1)
    o2 = o * o
    ms0 = jnp.sum(jnp.where(h0, o2, 0.0), axis=-1, keepdims=True)
    ms1 = jnp.sum(jnp.where(h0, 0.0, o2), axis=-1, keepdims=True)
    ms = jnp.where(h0, ms0, ms1) * (1.0 / HEAD_DIM)
    o_ref[0] = (o * lax.rsqrt(ms + EPS) * gn_ref[0]).astype(BF16)


def _fox_call(qkv, fcol, gn, n_pairs, tq):
    bsz, seq, _ = qkv.shape
    return pl.pallas_call(
        functools.partial(_fox_kernel, tq=tq),
        out_shape=jax.ShapeDtypeStruct((bsz, seq, n_pairs * LANES), BF16),
        grid=(bsz, n_pairs, seq // tq),
        in_specs=[
            pl.BlockSpec((1, tq, LANES), lambda b, p, i: (b, i, p)),
            pl.BlockSpec((1, seq, LANES), lambda b, p, i: (b, 0, n_pairs + p)),
            pl.BlockSpec((1, seq, LANES), lambda b, p, i: (b, 0, 2 * n_pairs + p)),
            pl.BlockSpec((1, seq, LANES), lambda b, p, i: (b, 0, 0)),
            pl.BlockSpec((1, 1, LANES), lambda b, p, i: (p, 0, 0)),
        ],
        out_specs=pl.BlockSpec((1, tq, LANES), lambda b, p, i: (b, i, p)),
        scratch_shapes=[pltpu.VMEM((seq, LANES), BF16)] * 4,
        compiler_params=_cparams(("arbitrary", "arbitrary", "arbitrary")),
        name="fox_attention",
    )(qkv, qkv, qkv, fcol, gn)


def _hgrn_kernel(hq_ref, hf_ref, hi_ref, hg_ref, lb_ref, gn_ref, o_ref):
    ch = HGRN_CHUNK
    seq = hq_ref.shape[1]
    lane = _iota((1, LANES), 1)
    h0 = lane < HEAD_DIM
    r = _iota((ch, ch), 0)
    c = _iota((ch, ch), 1)
    tri = (r >= c).astype(BF16)
    same_head = (r < HEAD_DIM) == (c < HEAD_DIM)
    seg = same_head.astype(BF16)
    r2 = _iota((ch, 2 * ch), 0)
    c2 = _iota((ch, 2 * ch), 1) % ch
    level_masks = [(r2 // (2 * m) == c2 // (2 * m)) & (r2 % (2 * m) >= m) & (c2 % (2 * m) < m)
                   for m in HGRN_LEVELS]
    small_levels = [m for m in HGRN_LEVELS if m < SUBLANES]
    pick = jnp.concatenate([(c == (r // (2 * m)) * (2 * m) + (m - 1)) for m in small_levels], axis=0).astype(BF16)
    lb = lb_ref[0]
    nt = (((1,), (1,)), ((), ()))
    zero_b = jnp.zeros((ch, LANES), BF16)

    def both_heads(x):
        return jnp.concatenate([jnp.where(h0, x, zero_b), jnp.where(h0, zero_b, x)], axis=0)

    def sum3(x):
        return x[:, 0:LANES] + x[:, LANES:2 * LANES] + x[:, 2 * LANES:]

    def parts3(x):
        return jnp.concatenate([part.astype(BF16) for part in _split3(x)], axis=1)

    def group(gi, state_t):
        n = HGRN_GROUP
        sls = [pl.ds(pl.multiple_of((gi * n + i) * ch, ch), ch) for i in range(n)]
        k, q, v, vb, b, picked = [], [], [], [], [], []
        for sl in sls:
            f = lb + (1.0 - lb) * jax.nn.sigmoid(hf_ref[0, sl, :])
            k.append(1.0 - f)
            q.append(_silu(hq_ref[0, sl, :]))
            v.append(hi_ref[0, sl, :])
            vb.append(v[-1].astype(BF16))
            b.append(sum3(jnp.dot(tri, parts3(jnp.log(f) * LOG2_E), preferred_element_type=F32)))
        for i in range(n):
            picked.append(jnp.dot(pick, parts3(b[i]), preferred_element_type=F32))

        sc = [jnp.zeros((ch, 2 * ch), F32) for _ in range(n)]
        for li, m in enumerate(HGRN_LEVELS):
            for i in range(n):
                if m >= SUBLANES:
                    ref_l = jnp.broadcast_to(b[i].reshape(ch // (2 * m), 2 * m, LANES)[:, m - 1:m, :],
                                             (ch // (2 * m), 2 * m, LANES)).reshape(ch, LANES)
                else:
                    at = small_levels.index(m) * ch
                    ref_l = sum3(picked[i][at:at + ch])
                e = jnp.exp2(-jnp.abs(b[i] - ref_l))
                s = lax.dot_general((q[i] * e).astype(BF16), both_heads((k[i] * e).astype(BF16)), nt,
                                    preferred_element_type=F32)
                sc[i] = jnp.where(level_masks[li], s, sc[i])

        o, upd_t, dec = [], [], []
        for i in range(n):
            oi = jnp.dot(sc[i].astype(BF16), both_heads(vb[i]), preferred_element_type=F32)
            o.append(oi + jnp.dot((q[i] * k[i]).astype(BF16), seg, preferred_element_type=F32) * v[i])
            b_last = b[i][ch - 1:ch, :]
            k_dec = (k[i] * jnp.exp2(b_last - b[i])).astype(BF16)
            upd = jnp.dot(v[i].T.astype(BF16), k_dec, preferred_element_type=F32)
            upd_t.append(jnp.where(same_head, upd, 0.0))
            dec.append(jnp.exp2(b_last))

        states = [state_t]
        for i in range(n):
            states.append(dec[i] * states[i] + upd_t[i])

        for i, sl in enumerate(sls):
            oi = o[i] + lax.dot_general((q[i] * jnp.exp2(b[i])).astype(BF16), states[i].astype(BF16), nt,
                                        preferred_element_type=F32)
            o2 = oi * oi
            ms0 = jnp.sum(jnp.where(h0, o2, 0.0), axis=-1, keepdims=True)
            ms1 = jnp.sum(jnp.where(h0, 0.0, o2), axis=-1, keepdims=True)
            ms = jnp.where(h0, ms0, ms1) * (1.0 / HEAD_DIM)
            out = oi * lax.rsqrt(ms + EPS) * gn_ref[0] * _silu(hg_ref[0, sl, :])
            o_ref[0, sl, :] = out.astype(BF16)
        return states[n]

    lax.fori_loop(0, seq // (HGRN_GROUP * ch), group, jnp.zeros((LANES, LANES), F32))


def _hgrn_call(hg4, lb, gn, n_pairs):
    bsz, seq, _ = hg4.shape

    def spec(off):
        return pl.BlockSpec((1, seq, LANES), lambda b, p: (b, 0, off + p))

    return pl.pallas_call(
        _hgrn_kernel,
        out_shape=jax.ShapeDtypeStruct((bsz, seq, n_pairs * LANES), BF16),
        grid=(bsz, n_pairs),
        in_specs=[spec(0), spec(n_pairs), spec(2 * n_pairs), spec(3 * n_pairs),
                  pl.BlockSpec((1, 1, LANES), lambda b, p: (p, 0, 0)),
                  pl.BlockSpec((1, 1, LANES), lambda b, p: (p, 0, 0))],
        out_specs=pl.BlockSpec((1, seq, LANES), lambda b, p: (b, 0, p)),
        compiler_params=_cparams(("arbitrary", "arbitrary")),
        name="hgrn2",
    )(hg4, hg4, hg4, hg4, lb, gn)


def _pool_kernel(u_ref, w_ref, s_ref, o_ref):
    u = u_ref[0]
    seq, n = u.shape
    t = _iota((seq, 1), 0)
    lane = _iota((1, n), 1)

    def shifted(x, k):
        return jnp.where(t >= k, pltpu.roll(x, k, axis=0), 0.0)

    sums = []
    s = u
    for w in POOL_WINDOWS:
        s = s + shifted(s, w // 2)
        sums.append(s)
    pos1 = (t + 1).astype(F32)
    group = len(POOL_WINDOWS) - 1
    mean = sums[group] / jnp.minimum(pos1, float(POOL_WINDOWS[group]))
    group_dim = n // len(POOL_WINDOWS)
    for gi in range(group - 1, -1, -1):
        mean = jnp.where(lane < (gi + 1) * group_dim,
                         sums[gi] / jnp.minimum(pos1, float(POOL_WINDOWS[gi])), mean)
    pooled = (mean - u).astype(BF16)
    o_ref[0] = (jnp.dot(pooled, w_ref[...], preferred_element_type=F32) * s_ref[...]).astype(BF16)


def _pool_call(pu, w_bd, scale):
    bsz, seq, n = pu.shape
    return pl.pallas_call(
        _pool_kernel,
        out_shape=jax.ShapeDtypeStruct((bsz, seq, n), BF16),
        grid=(bsz,),
        in_specs=[pl.BlockSpec((1, seq, n), lambda b: (b, 0, 0)),
                  pl.BlockSpec((n, n), lambda b: (0, 0)),
                  pl.BlockSpec((1, n), lambda b: (0, 0))],
        out_specs=pl.BlockSpec((1, seq, n), lambda b: (b, 0, 0)),
        compiler_params=_cparams(("arbitrary",)),
        name="multiscale_pool",
    )(pu, w_bd, scale)


def _route(logits):
    lane = _iota(logits.shape, 1)
    lane_f = lane.astype(F32)
    big = float(LANES)
    is_g = lane < N_GROUPS
    gl = jnp.where(is_g, logits, -jnp.inf)
    gmax = jnp.max(gl, axis=-1, keepdims=True)
    gsum = jnp.sum(jnp.where(is_g, jnp.exp(gl - gmax), 0.0), axis=-1, keepdims=True)
    p_group = 1.0 / gsum
    g_sel = jnp.min(jnp.where(gl == gmax, lane_f, big), axis=-1, keepdims=True)
    lo = ROUTER_EXPERT_LANE0 + EXPERTS_PER_GROUP * g_sel
    in_group = (lane_f >= lo) & (lane_f < lo + EXPERTS_PER_GROUP)
    el = jnp.where(in_group, logits, -jnp.inf)
    m1 = jnp.max(el, axis=-1, keepdims=True)
    i1 = jnp.min(jnp.where(el == m1, lane_f, big), axis=-1, keepdims=True)
    el2 = jnp.where(lane_f == i1, -jnp.inf, el)
    m2 = jnp.max(el2, axis=-1, keepdims=True)
    i2 = jnp.min(jnp.where(el2 == m2, lane_f, big), axis=-1, keepdims=True)
    e2 = jnp.exp(m2 - m1)
    w1 = p_group / (1.0 + e2)
    w2 = p_group * e2 / (1.0 + e2)
    first_low = i1 < i2
    a = jnp.minimum(i1, i2) - lo
    b = jnp.maximum(i1, i2) - lo
    pair = jnp.where(a == 0.0, b - 1.0, jnp.where(a == 1.0, b + 1.0, float(N_PAIRS - 1)))
    cls_lane = ROUTE_CLASS_LANE0 + g_sel * N_PAIRS + pair
    return (jnp.where(lane == 0, jnp.where(first_low, w1, w2), 0.0)
            + jnp.where(lane == 1, jnp.where(first_low, w2, w1), 0.0)
            + jnp.where(lane_f == cls_lane, 1.0, 0.0))


def _outproj_kernel(of_ref, oh_ref, op_ref, x_ref, w_ref, mod_ref, g_ref, wr_ref, br_ref,
                    xn_ref, h2_ref, *, n_fox, n_hgrn):
    d = x_ref.shape[-1]
    y = jnp.dot(of_ref[...], w_ref[0:n_fox, :], preferred_element_type=F32)
    y = y + jnp.dot(oh_ref[...], w_ref[n_fox:n_fox + n_hgrn, :], preferred_element_type=F32)
    y = y + jnp.dot(op_ref[...], w_ref[n_fox + n_hgrn:, :], preferred_element_type=F32)
    xn = x_ref[...] + (1.0 + mod_ref[0, 2:3, :]) * y
    xn_ref[...] = xn
    h2 = _norm_mod(xn, g_ref[...], mod_ref[0, 3:4, :], mod_ref[0, 4:5, :])
    h2_ref[:, 0:d] = h2
    h_hi = h2.astype(BF16)
    h_lo = (h2 - h_hi.astype(F32)).astype(BF16)
    logits = (jnp.dot(h_hi, wr_ref[0], preferred_element_type=F32)
              + jnp.dot(h_lo, wr_ref[0], preferred_element_type=F32)
              + jnp.dot(h_hi, wr_ref[1], preferred_element_type=F32)) + br_ref[...]
    h2_ref[:, d:] = _route(logits)


def _outproj_call(o_fox, o_hgrn, o_pool, x, w_out, mod, g2, w_router, b_router, tm, seq):
    t, d = x.shape
    n_fox, n_hgrn, n_pool = o_fox.shape[-1], o_hgrn.shape[-1], o_pool.shape[-1]
    per_seq = seq // tm
    return pl.pallas_call(
        functools.partial(_outproj_kernel, n_fox=n_fox, n_hgrn=n_hgrn),
        out_shape=(jax.ShapeDtypeStruct((t, d), F32),
                   jax.ShapeDtypeStruct((t, d + LANES), F32)),
        grid=(t // tm,),
        in_specs=[
            pl.BlockSpec((tm, n_fox), lambda i: (i, 0)),
            pl.BlockSpec((tm, n_hgrn), lambda i: (i, 0)),
            pl.BlockSpec((tm, n_pool), lambda i: (i, 0)),
            pl.BlockSpec((tm, d), lambda i: (i, 0)),
            pl.BlockSpec((d, d), lambda i: (0, 0)),
            pl.BlockSpec((1, N_MOD, d), lambda i: (i // per_seq, 0, 0)),
            pl.BlockSpec((1, d), lambda i: (0, 0)),
            pl.BlockSpec((2, d, LANES), lambda i: (0, 0, 0)),
            pl.BlockSpec((1, LANES), lambda i: (0, 0)),
        ],
        out_specs=(pl.BlockSpec((tm, d), lambda i: (i, 0)),
                   pl.BlockSpec((tm, d + LANES), lambda i: (i, 0))),
        compiler_params=_cparams(("arbitrary",)),
        name="outproj_router",
    )(o_fox, o_hgrn, o_pool, x, w_out, mod, g2, w_router, b_router)


def _sort_kernel(route_ref, pos_ref, meta_ref, *, tile_rows, cb):
    t = route_ref.shape[0]
    lane = _iota((1, LANES), 1)
    is_cls = (lane >= ROUTE_CLASS_LANE0) & (lane < ROUTE_CLASS_LANE0 + N_CLASSES)
    nblk = t // cb
    zero_row = jnp.zeros((1, LANES), F32)

    def onehot(i):
        return jnp.where(is_cls, route_ref[pl.ds(pl.multiple_of(i * cb, cb), cb), :], 0.0)

    counts = lax.fori_loop(0, nblk, lambda i, acc: acc + jnp.sum(onehot(i), axis=0, keepdims=True), zero_row)
    padded = jnp.floor((counts + float(tile_rows - 1)) * (1.0 / tile_rows)) * float(tile_rows)
    before = (_iota((LANES, LANES), 0) < _iota((LANES, LANES), 1)).astype(F32)
    offs = jnp.dot(jnp.broadcast_to(padded, (8, LANES)), before, precision=HIGHEST,
                   preferred_element_type=F32)[0:1]
    ends = offs + padded
    strict = (_iota((cb, cb), 0) > _iota((cb, cb), 1)).astype(BF16)
    ones8 = jnp.ones((8, LANES), F32)
    nt = (((1,), (1,)), ((), ()))

    def place(i, seen):
        oh = onehot(i)
        rank = jnp.dot(strict, oh.astype(BF16), preferred_element_type=F32) + seen
        dest = oh * (rank + offs)
        rows = lax.dot_general(ones8, dest, nt, precision=HIGHEST, preferred_element_type=F32)
        pos_ref[:, pl.ds(pl.multiple_of(i * cb, cb), cb)] = rows.astype(jnp.int32)
        return seen + jnp.sum(oh, axis=0, keepdims=True)

    lax.fori_loop(0, nblk, place, zero_row)
    tile_start = _iota((LANES, LANES), 0).astype(F32) * float(tile_rows)
    tile_cls = jnp.sum(jnp.where(is_cls & (ends <= tile_start), 1.0, 0.0), axis=-1, keepdims=True)
    n_used = jnp.sum(jnp.where(lane == ROUTE_CLASS_LANE0 + N_CLASSES - 1, ends, 0.0), axis=-1,
                     keepdims=True) * (1.0 / tile_rows)
    meta_ref[...] = jnp.where(lane == 0, tile_cls, jnp.where(lane == 1, n_used, 0.0))


def _sort_call(h2ext, d, tile_rows):
    t = h2ext.shape[0]
    cb = min(512, t)
    return pl.pallas_call(
        functools.partial(_sort_kernel, tile_rows=tile_rows, cb=cb),
        out_shape=(jax.ShapeDtypeStruct((8, t), jnp.int32),
                   jax.ShapeDtypeStruct((LANES, LANES), F32)),
        grid=(1,),
        in_specs=[pl.BlockSpec((t, LANES), lambda i: (0, d // LANES))],
        out_specs=(pl.BlockSpec((8, t), lambda i: (0, 0)),
                   pl.BlockSpec((LANES, LANES), lambda i: (0, 0))),
        compiler_params=_cparams(("arbitrary",)),
        name="route_sort",
    )(h2ext)


def _dispatch_kernel(pos_ref, h_ref, init_ref, out_ref, sem, *, tm):
    del init_ref
    base = pl.program_id(0) * tm

    def issue(g, carry):
        for u in range(SUBLANES):
            dst = pos_ref[base + g * SUBLANES + u]
            pltpu.make_async_copy(h_ref.at[g, pl.ds(u, 1), :], out_ref.at[pl.ds(dst, 1), :], sem).start()
        return carry

    lax.fori_loop(0, tm // SUBLANES, issue, 0)
    pltpu.make_async_copy(h_ref, h_ref, sem).wait()


def _dispatch_call(pos, h2ext, n_rows, tm):
    t, w = h2ext.shape
    return pl.pallas_call(
        functools.partial(_dispatch_kernel, tm=tm),
        out_shape=jax.ShapeDtypeStruct((n_rows, w), F32),
        grid_spec=pltpu.PrefetchScalarGridSpec(
            num_scalar_prefetch=1,
            grid=(t // tm,),
            in_specs=[pl.BlockSpec((tm // SUBLANES, SUBLANES, w), lambda i, pos: (i, 0, 0)),
                      pl.BlockSpec(memory_space=pl.ANY)],
            out_specs=pl.BlockSpec(memory_space=pl.ANY),
            scratch_shapes=[pltpu.SemaphoreType.DMA],
        ),
        input_output_aliases={2: 0},
        compiler_params=_cparams(("arbitrary",)),
        name="moe_dispatch",
    )(pos, h2ext.reshape(t // SUBLANES, SUBLANES, w), jnp.zeros((n_rows, w), F32))


def _moe_kernel(ea_ref, eb_ref, nu_ref, h_ref, wga_ref, wua_ref, wda_ref, wgb_ref, wub_ref, wdb_ref, o_ref):
    del ea_ref, eb_ref
    d = o_ref.shape[-1]

    @pl.when(pl.program_id(0) < nu_ref[0])
    def _():
        h = h_ref[:, 0:d].astype(BF16)
        tail = h_ref[:, d:]
        lane = _iota((1, LANES), 1)
        w_a = jnp.sum(jnp.where(lane == 0, tail, 0.0), axis=-1, keepdims=True)
        w_b = jnp.sum(jnp.where(lane == 1, tail, 0.0), axis=-1, keepdims=True)

        def expert(wg_ref, wu_ref, wd_ref):
            hidden = _silu(jnp.dot(h, wg_ref[0], preferred_element_type=F32)) * jnp.dot(
                h, wu_ref[0], preferred_element_type=F32)
            return jnp.dot(hidden.astype(BF16), wd_ref[0], preferred_element_type=F32)

        o_ref[...] = w_a * expert(wga_ref, wua_ref, wda_ref) + w_b * expert(wgb_ref, wub_ref, wdb_ref)

    @pl.when(pl.program_id(0) >= nu_ref[0])
    def _():
        o_ref[...] = jnp.zeros_like(o_ref)


def _moe_call(ea, eb, n_used, h_sorted, wg, wu, wd, tile_rows):
    n_rows, _ = h_sorted.shape
    w = h_sorted.shape[1]
    _, d, f = wg.shape

    def rows(j, ea, eb, nu):
        return (jnp.maximum(jnp.minimum(j, nu[0] - 1), 0), 0)

    def wa(j, ea, eb, nu):
        return (ea[j], 0, 0)

    def wb(j, ea, eb, nu):
        return (eb[j], 0, 0)

    return pl.pallas_call(
        _moe_kernel,
        out_shape=jax.ShapeDtypeStruct((n_rows, d), F32),
        grid_spec=pltpu.PrefetchScalarGridSpec(
            num_scalar_prefetch=3,
            grid=(n_rows // tile_rows,),
            in_specs=[pl.BlockSpec((tile_rows, w), rows),
                      pl.BlockSpec((1, d, f), wa), pl.BlockSpec((1, d, f), wa), pl.BlockSpec((1, f, d), wa),
                      pl.BlockSpec((1, d, f), wb), pl.BlockSpec((1, d, f), wb), pl.BlockSpec((1, f, d), wb)],
            out_specs=pl.BlockSpec((tile_rows, d), lambda j, ea, eb, nu: (j, 0)),
        ),
        compiler_params=_cparams(("arbitrary",)),
        name="moe_experts",
    )(ea, eb, n_used, h_sorted, wg, wu, wd, wg, wu, wd)


def _combine_kernel(pos_ref, y_ref, x_ref, mod_ref, fg_ref, o_ref, buf, sem, *, tm, final):
    i = pl.program_id(0)

    def issue(tile, slot):
        def eight(g, carry):
            for u in range(SUBLANES):
                src = pos_ref[tile * tm + g * SUBLANES + u]
                pltpu.make_async_copy(y_ref.at[pl.ds(src, 1), :], buf.at[slot, g, pl.ds(u, 1), :],
                                      sem.at[slot]).start()
            return carry
        lax.fori_loop(0, tm // SUBLANES, eight, 0)

    @pl.when(i == 0)
    def _():
        issue(0, 0)

    @pl.when(i + 1 < pl.num_programs(0))
    def _():
        issue(i + 1, (i + 1) % 2)

    slot = i % 2
    pltpu.make_async_copy(buf.at[slot], buf.at[slot], sem.at[slot]).wait()
    out = x_ref[...] + (1.0 + mod_ref[0, 5:6, :]) * buf[slot].reshape(x_ref.shape)
    if final:
        ms = jnp.mean(out * out, axis=-1, keepdims=True)
        out = out * lax.rsqrt(ms + EPS) * fg_ref[...]
    o_ref[...] = out


def _combine_call(pos, y_sorted, x, mod, final_g, tm, seq, final):
    t, d = x.shape
    per_seq = seq // tm
    return pl.pallas_call(
        functools.partial(_combine_kernel, tm=tm, final=final),
        out_shape=jax.ShapeDtypeStruct((t, d), F32),
        grid_spec=pltpu.PrefetchScalarGridSpec(
            num_scalar_prefetch=1,
            grid=(t // tm,),
            in_specs=[pl.BlockSpec(memory_space=pl.ANY),
                      pl.BlockSpec((tm, d), lambda i, pos: (i, 0)),
                      pl.BlockSpec((1, N_MOD, d), lambda i, pos: (i // per_seq, 0, 0)),
                      pl.BlockSpec((1, d), lambda i, pos: (0, 0))],
            out_specs=pl.BlockSpec((tm, d), lambda i, pos: (i, 0)),
            scratch_shapes=[pltpu.VMEM((2, tm // SUBLANES, SUBLANES, d), F32), pltpu.SemaphoreType.DMA((2,))],
        ),
        compiler_params=_cparams(("arbitrary",)),
        name="moe_combine",
    )(pos, y_sorted, x, mod, final_g)


def _pad_lanes(a, n=LANES):
    return jnp.pad(a, [(0, 0)] * (a.ndim - 1) + [(0, n - a.shape[-1])])


def kernel(x, c, w_ada, b_ada, norm1_g, w_in, fox_f_bias, fox_norm_g, hgrn_lb_logits, hgrn_norm_g, pool_w, pool_scale, w_out, norm2_g, router_group_w, router_group_b, router_expert_w, router_expert_b, expert_w_gate, expert_w_up, expert_w_down, final_norm_g):
    bsz, seq, d = x.shape
    depth = w_ada.shape[0]
    fox_heads = fox_f_bias.shape[1]
    fox_dim = fox_heads * HEAD_DIM
    hgrn_dim = hgrn_lb_logits.shape[1]
    pool_dim = pool_scale.shape[1]
    n_fox_pairs = fox_dim // LANES
    n_hgrn_pairs = hgrn_dim // LANES
    t = bsz * seq
    tm = min(512, seq)
    tq = min(512, seq)
    n_tiles = t // MOE_TILE_ROWS + N_CLASSES
    assert n_tiles <= LANES and t % MOE_TILE_ROWS == 0

    o_ff = 3 * fox_dim
    w_qkv = w_in[:, :, :o_ff].astype(BF16)
    w_hp = w_in[:, :, o_ff + fox_heads:].astype(BF16)
    w_ff = _pad_lanes(w_in[:, :, o_ff:o_ff + fox_heads]).astype(BF16)
    f_bias = _pad_lanes(fox_f_bias)
    groups = pool_w.shape[1]
    pool_bd = jnp.einsum('lgcd,gh->lgchd', pool_w, jnp.eye(groups, dtype=pool_w.dtype)).reshape(
        depth, pool_dim, pool_dim).astype(BF16)
    w_router = _pad_lanes(jnp.concatenate([router_group_w, router_expert_w], axis=-1))
    w_router_hi = w_router.astype(BF16)
    w_router = jnp.stack([w_router_hi, (w_router - w_router_hi.astype(F32)).astype(BF16)], axis=1)
    b_router = _pad_lanes(jnp.concatenate([router_group_b, router_expert_b], axis=-1))
    w_out_b = w_out.astype(BF16)
    wg_b = expert_w_gate.astype(BF16)
    wu_b = expert_w_up.astype(BF16)
    wd_b = expert_w_down.astype(BF16)
    pair_a = jnp.array([0, 0, 0, 1, 1, 2], jnp.int32)
    pair_b = jnp.array([1, 2, 3, 2, 3, 3], jnp.int32)
    final_g = final_norm_g.reshape(1, d)

    mod_all = _ada_call(c, w_ada, b_ada).reshape(depth, bsz, N_MOD, d)
    lower = _lb_call(hgrn_lb_logits)

    for l in range(depth):
        mod = mod_all[l]
        qkv, hg4, pu, ff = _inproj_call(x, norm1_g[l:l + 1], mod, w_qkv[l], w_hp[l], w_ff[l], 4 * hgrn_dim, tm)
        fcol = _fbias_call(ff, f_bias[l:l + 1])
        o_fox = _fox_call(qkv, fcol, fox_norm_g[l].reshape(n_fox_pairs, 1, LANES), n_fox_pairs, tq)
        o_hgrn = _hgrn_call(hg4, lower[l].reshape(n_hgrn_pairs, 1, LANES),
                            hgrn_norm_g[l].reshape(n_hgrn_pairs, 1, LANES), n_hgrn_pairs)
        o_pool = _pool_call(pu, pool_bd[l], pool_scale[l:l + 1])
        x2, h2ext = _outproj_call(o_fox.reshape(t, -1), o_hgrn.reshape(t, -1), o_pool.reshape(t, -1),
                                  x.reshape(t, d), w_out_b[l], mod, norm2_g[l:l + 1],
                                  w_router[l], b_router[l:l + 1], tm, seq)
        pos8, meta = _sort_call(h2ext, d, MOE_TILE_ROWS)
        pos = pos8[0]
        n_used = meta[0, 1].astype(jnp.int32).reshape(1)
        tile = jnp.minimum(jnp.arange(n_tiles, dtype=jnp.int32), n_used[0] - 1)
        tile_cls = jnp.minimum(meta[:, 0].astype(jnp.int32)[tile], N_CLASSES - 1)
        e0 = (tile_cls // N_PAIRS) * EXPERTS_PER_GROUP
        ea = e0 + pair_a[tile_cls % N_PAIRS]
        eb = e0 + pair_b[tile_cls % N_PAIRS]
        h_sorted = _dispatch_call(pos, h2ext, n_tiles * MOE_TILE_ROWS, tm)
        y_sorted = _moe_call(ea, eb, n_used, h_sorted, wg_b[l], wu_b[l], wd_b[l], MOE_TILE_ROWS)
        x = _combine_call(pos, y_sorted, x2, mod, final_g, tm, seq, final=(l == depth - 1)).reshape(bsz, seq, d)
    return x
```

```python
import functools

import jax
import jax.numpy as jnp
from jax import lax
from jax.experimental import pallas as pl
from jax.experimental.pallas import tpu as pltpu

F32 = jnp.float32
BF16 = jnp.bfloat16
HIGHEST = lax.Precision.HIGHEST

HEAD_DIM = 64
LANES = 128
SUBLANES = 8
EPS = 1e-6
N_MOD = 6
POOL_WINDOWS = (2, 4, 8, 16)
N_GROUPS = 4
EXPERTS_PER_GROUP = 4
N_EXPERTS = N_GROUPS * EXPERTS_PER_GROUP
ROUTER_EXPERT_LANE0 = N_GROUPS
N_PAIRS = 6
N_CLASSES = N_GROUPS * N_PAIRS
ROUTE_CLASS_LANE0 = 32
MOE_TILE_ROWS = 256
HGRN_CHUNK = 128
HGRN_LEVELS = (64, 32, 16, 8, 4, 2, 1)
HGRN_GROUP = 4
VMEM_LIMIT = 48 * 1024 * 1024
LOG2_E = 1.4426950408889634
FOX_Q_SCALE = HEAD_DIM ** -0.5 * LOG2_E


def _cparams(sem):
    return pltpu.CompilerParams(dimension_semantics=sem, vmem_limit_bytes=VMEM_LIMIT)


def _silu(x):
    return x * jax.nn.sigmoid(x)


def _iota(shape, dim):
    return lax.broadcasted_iota(jnp.int32, shape, dim)


def _ada_kernel(c_ref, w_ref, b_ref, o_ref):
    sc = _silu(c_ref[...]).astype(BF16)
    o_ref[0] = jnp.dot(sc, w_ref[0].astype(BF16), preferred_element_type=F32) + b_ref[0]


def _ada_call(c, w_ada, b_ada):
    depth, d, n = w_ada.shape
    bsz = c.shape[0]
    tn = 1536
    return pl.pallas_call(
        _ada_kernel,
        out_shape=jax.ShapeDtypeStruct((depth, bsz, n), F32),
        grid=(depth, n // tn),
        in_specs=[
            pl.BlockSpec((bsz, d), lambda l, j: (0, 0)),
            pl.BlockSpec((1, d, tn), lambda l, j: (l, 0, j)),
            pl.BlockSpec((1, 1, tn), lambda l, j: (l, 0, j)),
        ],
        out_specs=pl.BlockSpec((1, bsz, tn), lambda l, j: (l, 0, j)),
        compiler_params=_cparams(("arbitrary", "arbitrary")),
        name="ada_mod",
    )(c, w_ada, b_ada.reshape(depth, 1, n))


def _lb_kernel(x_ref, o_ref):
    x = x_ref[...]
    depth = x.shape[0]
    e = jnp.exp(x - jnp.max(x, axis=0, keepdims=True))
    p = e / jnp.sum(e, axis=0, keepdims=True)
    acc = jnp.zeros_like(p[0:1])
    o_ref[0:1, :] = acc
    for l in range(1, depth):
        acc = acc + p[l:l + 1]
        o_ref[l:l + 1, :] = acc


def _lb_call(lb_logits):
    return pl.pallas_call(
        _lb_kernel,
        out_shape=jax.ShapeDtypeStruct(lb_logits.shape, F32),
        name="hgrn_lower_bounds",
    )(lb_logits)


def _norm_mod(x, g, shift, scale):
    ms = jnp.mean(x * x, axis=-1, keepdims=True)
    return (x * lax.rsqrt(ms + EPS) * g) * (1.0 + scale) + shift


def _inproj_kernel(x_ref, g_ref, mod_ref, wq_ref, wh_ref, wf_ref, qkv_ref, hg_ref, pu_ref, ff_ref, *, n_hg):
    h = _norm_mod(x_ref[0], g_ref[...], mod_ref[0, 0:1, :], mod_ref[0, 1:2, :]).astype(BF16)
    n_qkv = qkv_ref.shape[-1]
    q_scale = jnp.where(_iota((1, n_qkv), 1) < n_qkv // 3, FOX_Q_SCALE, 1.0)
    qkv_ref[0] = (jnp.dot(h, wq_ref[...], preferred_element_type=F32) * q_scale).astype(BF16)
    hg_ref[0] = jnp.dot(h, wh_ref[:, 0:n_hg], preferred_element_type=F32)
    pu_ref[0] = jnp.dot(h, wh_ref[:, n_hg:], preferred_element_type=F32)
    ff_ref[0] = jnp.dot(h, wf_ref[...], preferred_element_type=F32)


def _inproj_call(x, g, mod, w_qkv, w_hp, w_ff, n_hg, tm):
    bsz, seq, d = x.shape
    n_qkv, n_ff = w_qkv.shape[1], w_ff.shape[1]
    n_pu = w_hp.shape[1] - n_hg
    return pl.pallas_call(
        functools.partial(_inproj_kernel, n_hg=n_hg),
        out_shape=(
            jax.ShapeDtypeStruct((bsz, seq, n_qkv), BF16),
            jax.ShapeDtypeStruct((bsz, seq, n_hg), F32),
            jax.ShapeDtypeStruct((bsz, seq, n_pu), F32),
            jax.ShapeDtypeStruct((bsz, seq, n_ff), F32),
        ),
        grid=(bsz, seq // tm),
        in_specs=[
            pl.BlockSpec((1, tm, d), lambda b, i: (b, i, 0)),
            pl.BlockSpec((1, d), lambda b, i: (0, 0)),
            pl.BlockSpec((1, N_MOD, d), lambda b, i: (b, 0, 0)),
            pl.BlockSpec((d, n_qkv), lambda b, i: (0, 0)),
            pl.BlockSpec((d, n_hg + n_pu), lambda b, i: (0, 0)),
            pl.BlockSpec((d, n_ff), lambda b, i: (0, 0)),
        ],
        out_specs=(
            pl.BlockSpec((1, tm, n_qkv), lambda b, i: (b, i, 0)),
            pl.BlockSpec((1, tm, n_hg), lambda b, i: (b, i, 0)),
            pl.BlockSpec((1, tm, n_pu), lambda b, i: (b, i, 0)),
            pl.BlockSpec((1, tm, n_ff), lambda b, i: (b, i, 0)),
        ),
        compiler_params=_cparams(("arbitrary", "arbitrary")),
        name="norm_inproj",
    )(x, g, mod, w_qkv, w_hp, w_ff)


def _fbias_kernel(ff_ref, bias_ref, fcol_ref, *, cb):
    seq = ff_ref.shape[1]
    tri = (_iota((cb, cb), 0) >= _iota((cb, cb), 1)).astype(F32)
    carry = jnp.zeros((1, LANES), F32)
    for blk in range(seq // cb):
        x = ff_ref[0, blk * cb:(blk + 1) * cb, :] + bias_ref[...]
        log_f = jnp.minimum(x, 0.0) - jnp.log1p(jnp.exp(-jnp.abs(x)))
        cs = jnp.dot(tri, log_f, precision=HIGHEST, preferred_element_type=F32) + carry
        fcol_ref[0, blk * cb:(blk + 1) * cb, :] = cs * LOG2_E
        carry = cs[cb - 1:cb, :]


def _fbias_call(ff, bias):
    bsz, seq, _ = ff.shape
    cb = min(256, seq)
    return pl.pallas_call(
        functools.partial(_fbias_kernel, cb=cb),
        out_shape=jax.ShapeDtypeStruct((bsz, seq, LANES), F32),
        grid=(bsz,),
        in_specs=[pl.BlockSpec((1, seq, LANES), lambda b: (b, 0, 0)),
                  pl.BlockSpec((1, LANES), lambda b: (0, 0))],
        out_specs=pl.BlockSpec((1, seq, LANES), lambda b: (b, 0, 0)),
        compiler_params=_cparams(("arbitrary",)),
        name="fox_forget_bias",
    )(ff, bias)


def _split3(f):
    hi = f.astype(BF16).astype(F32)
    rest = f - hi
    mid = rest.astype(BF16).astype(F32)
    return hi, mid, rest - mid


def _fox_kernel(q_ref, k_ref, v_ref, fc_ref, gn_ref, o_ref, k0_sc, k1_sc, v0_sc, v1_sc, *, tq):
    p = pl.program_id(1)
    seq = k_ref.shape[1]
    nb = seq // tq
    lane = _iota((1, LANES), 1)
    h0 = lane < HEAD_DIM
    data = (h0, jnp.logical_not(h0))
    base = (HEAD_DIM, 0)
    k_sc = (k0_sc, k1_sc)
    v_sc = (v0_sc, v1_sc)
    nt = (((1,), (1,)), ((), ()))
    heads = (0, 1)

    def head_f(fc, which):
        return jnp.sum(jnp.where(lane == 2 * p + which, fc, 0.0), axis=-1, keepdims=True)

    def with_bias_lanes(x, which, first3, last3):
        out = jnp.where(data[which], x, 0.0)
        for n in range(3):
            out = jnp.where(lane == base[which] + n, first3[n], out)
            out = jnp.where(lane == base[which] + 3 + n, last3[n], out)
        return out.astype(BF16)

    qs = {}
    for blk in range(nb):
        rows = slice(blk * tq, (blk + 1) * tq)
        fc = fc_ref[0, rows, :]
        qf = q_ref[0, rows, :].astype(F32)
        kf = k_ref[0, rows, :].astype(F32)
        vf = v_ref[0, rows, :].astype(F32)
        for which in heads:
            hi, mid, lo = _split3(head_f(fc, which))
            qs[blk, which] = with_bias_lanes(qf, which, (hi, mid, lo), (1.0, 1.0, 1.0))
            k_sc[which][rows, :] = with_bias_lanes(kf, which, (1.0, 1.0, 1.0), (-hi, -mid, -lo))
            v_sc[which][rows, :] = jnp.where(data[which], vf,
                                             jnp.where(lane == base[which], 1.0, 0.0)).astype(BF16)

    causal = _iota((tq, tq), 0) >= _iota((tq, tq), 1)
    m, acc = {}, {}
    for r in range(nb):
        keys = slice(r * tq, (r + 1) * tq)
        tasks = [(i, which) for i in range(r, nb) for which in heads]
        s = {}
        for i, which in tasks:
            s[i, which] = lax.dot_general(qs[i, which], k_sc[which][keys, :], nt, preferred_element_type=F32)
            if i == r:
                s[i, which] = jnp.where(causal, s[i, which], -jnp.inf)
        n = {}
        for t in tasks:
            n[t] = jnp.max(s[t], axis=-1, keepdims=True)
            if r > 0:
                n[t] = jnp.maximum(m[t], n[t])
        pv = {}
        for i, which in tasks:
            pr = jnp.exp2(s[i, which] - n[i, which]).astype(BF16)
            pv[i, which] = jnp.dot(pr, v_sc[which][keys, :], preferred_element_type=F32)
        for t in tasks:
            acc[t] = pv[t] if r == 0 else jnp.exp2(m[t] - n[t]) * acc[t] + pv[t]
            m[t] = n[t]

        l0 = jnp.sum(jnp.where(lane == base[0], acc[r, 0], 0.0), axis=-1, keepdims=True)
        l1 = jnp.sum(jnp.where(lane == base[1], acc[r, 1], 0.0), axis=-1, keepdims=True)
        o = jnp.where(h0, acc[r, 0] / l0, acc[r, 1] / l1)
        o2 = o * o
        ms0 = jnp.sum(jnp.where(h0, o2, 0.0), axis=-1, keepdims=True)
        ms1 = jnp.sum(jnp.where(h0, 0.0, o2), axis=-1, keepdims=True)
        ms = jnp.where(h0, ms0, ms1) * (1.0 / HEAD_DIM)
        o_ref[0, keys, :] = (o * lax.rsqrt(ms + EPS) * gn_ref[0]).astype(BF16)


def _fox_call(qkv, fcol, gn, n_pairs, tq):
    bsz, seq, _ = qkv.shape
    return pl.pallas_call(
        functools.partial(_fox_kernel, tq=tq),
        out_shape=jax.ShapeDtypeStruct((bsz, seq, n_pairs * LANES), BF16),
        grid=(bsz, n_pairs),
        in_specs=[
            pl.BlockSpec((1, seq, LANES), lambda b, p: (b, 0, p)),
            pl.BlockSpec((1, seq, LANES), lambda b, p: (b, 0, n_pairs + p)),
            pl.BlockSpec((1, seq, LANES), lambda b, p: (b, 0, 2 * n_pairs + p)),
            pl.BlockSpec((1, seq, LANES), lambda b, p: (b, 0, 0)),
            pl.BlockSpec((1, 1, LANES), lambda b, p: (p, 0, 0)),
        ],
        out_specs=pl.BlockSpec((1, seq, LANES), lambda b, p: (b, 0, p)),
        scratch_shapes=[pltpu.VMEM((seq, LANES), BF16)] * 4,
        compiler_params=_cparams(("arbitrary", "arbitrary")),
        name="fox_attention",
    )(qkv, qkv, qkv, fcol, gn)


def _hgrn_kernel(hq_ref, hf_ref, hi_ref, hg_ref, lb_ref, gn_ref, o_ref):
    ch = HGRN_CHUNK
    seq = hq_ref.shape[1]
    lane = _iota((1, LANES), 1)
    h0 = lane < HEAD_DIM
    r = _iota((ch, ch), 0)
    c = _iota((ch, ch), 1)
    tri = (r >= c).astype(BF16)
    same_head = (r < HEAD_DIM) == (c < HEAD_DIM)
    seg = same_head.astype(BF16)
    r2 = _iota((ch, 2 * ch), 0)
    c2 = _iota((ch, 2 * ch), 1) % ch
    level_masks = [(r2 // (2 * m) == c2 // (2 * m)) & (r2 % (2 * m) >= m) & (c2 % (2 * m) < m)
                   for m in HGRN_LEVELS]
    small_levels = [m for m in HGRN_LEVELS if m < SUBLANES]
    pick = jnp.concatenate([(c == (r // (2 * m)) * (2 * m) + (m - 1)) for m in small_levels], axis=0).astype(BF16)
    lb = lb_ref[0]
    nt = (((1,), (1,)), ((), ()))
    zero_b = jnp.zeros((ch, LANES), BF16)

    def both_heads(x):
        return jnp.concatenate([jnp.where(h0, x, zero_b), jnp.where(h0, zero_b, x)], axis=0)

    def sum3(x):
        return x[:, 0:LANES] + x[:, LANES:2 * LANES] + x[:, 2 * LANES:]

    def parts3(x):
        return jnp.concatenate([part.astype(BF16) for part in _split3(x)], axis=1)

    def group(gi, state_t):
        n = HGRN_GROUP
        sls = [pl.ds(pl.multiple_of((gi * n + i) * ch, ch), ch) for i in range(n)]
        k, q, v, vb, b, picked = [], [], [], [], [], []
        for sl in sls:
            f = lb + (1.0 - lb) * jax.nn.sigmoid(hf_ref[0, sl, :])
            k.append(1.0 - f)
            q.append(_silu(hq_ref[0, sl, :]))
            v.append(hi_ref[0, sl, :])
            vb.append(v[-1].astype(BF16))
            b.append(sum3(jnp.dot(tri, parts3(jnp.log(f) * LOG2_E), preferred_element_type=F32)))
        for i in range(n):
            picked.append(jnp.dot(pick, parts3(b[i]), preferred_element_type=F32))

        sc = [jnp.zeros((ch, 2 * ch), F32) for _ in range(n)]
        for li, m in enumerate(HGRN_LEVELS):
            for i in range(n):
                if m >= SUBLANES:
                    ref_l = jnp.broadcast_to(b[i].reshape(ch // (2 * m), 2 * m, LANES)[:, m - 1:m, :],
                                             (ch // (2 * m), 2 * m, LANES)).reshape(ch, LANES)
                else:
                    at = small_levels.index(m) * ch
                    ref_l = sum3(picked[i][at:at + ch])
                e = jnp.exp2(-jnp.abs(b[i] - ref_l))
                s = lax.dot_general((q[i] * e).astype(BF16), both_heads((k[i] * e).astype(BF16)), nt,
                                    preferred_element_type=F32)
                sc[i] = jnp.where(level_masks[li], s, sc[i])

        o, upd_t, dec = [], [], []
        for i in range(n):
            oi = jnp.dot(sc[i].astype(BF16), both_heads(vb[i]), preferred_element_type=F32)
            o.append(oi + jnp.dot((q[i] * k[i]).astype(BF16), seg, preferred_element_type=F32) * v[i])
            b_last = b[i][ch - 1:ch, :]
            k_dec = (k[i] * jnp.exp2(b_last - b[i])).astype(BF16)
            upd = jnp.dot(v[i].T.astype(BF16), k_dec, preferred_element_type=F32)
            upd_t.append(jnp.where(same_head, upd, 0.0))
            dec.append(jnp.exp2(b_last))

        states = [state_t]
        for i in range(n):
            states.append(dec[i] * states[i] + upd_t[i])

        for i, sl in enumerate(sls):
            oi = o[i] + lax.dot_general((q[i] * jnp.exp2(b[i])).astype(BF16), states[i].astype(BF16), nt,
                                        preferred_element_type=F32)
            o2 = oi * oi
            ms0 = jnp.sum(jnp.where(h0, o2, 0.0), axis=-1, keepdims=True)
            ms1 = jnp.sum(jnp.where(h0, 0.0, o2), axis=-1, keepdims=True)
            ms = jnp.where(h0, ms0, ms1) * (1.0 / HEAD_DIM)
            out = oi * lax.rsqrt(ms + EPS) * gn_ref[0] * _silu(hg_ref[0, sl, :])
            o_ref[0, sl, :] = out.astype(BF16)
        return states[n]

    lax.fori_loop(0, seq // (HGRN_GROUP * ch), group, jnp.zeros((LANES, LANES), F32))


def _hgrn_call(hg4, lb, gn, n_pairs):
    bsz, seq, _ = hg4.shape

    def spec(off):
        return pl.BlockSpec((1, seq, LANES), lambda b, p: (b, 0, off + p))

    return pl.pallas_call(
        _hgrn_kernel,
        out_shape=jax.ShapeDtypeStruct((bsz, seq, n_pairs * LANES), BF16),
        grid=(bsz, n_pairs),
        in_specs=[spec(0), spec(n_pairs), spec(2 * n_pairs), spec(3 * n_pairs),
                  pl.BlockSpec((1, 1, LANES), lambda b, p: (p, 0, 0)),
                  pl.BlockSpec((1, 1, LANES), lambda b, p: (p, 0, 0))],
        out_specs=pl.BlockSpec((1, seq, LANES), lambda b, p: (b, 0, p)),
        compiler_params=_cparams(("arbitrary", "arbitrary")),
        name="hgrn2",
    )(hg4, hg4, hg4, hg4, lb, gn)


def _pool_kernel(u_ref, w_ref, s_ref, o_ref):
    u = u_ref[0]
    seq, n = u.shape
    t = _iota((seq, 1), 0)
    lane = _iota((1, n), 1)

    def shifted(x, k):
        return jnp.where(t >= k, pltpu.roll(x, k, axis=0), 0.0)

    sums = []
    s = u
    for w in POOL_WINDOWS:
        s = s + shifted(s, w // 2)
        sums.append(s)
    pos1 = (t + 1).astype(F32)
    group = len(POOL_WINDOWS) - 1
    mean = sums[group] / jnp.minimum(pos1, float(POOL_WINDOWS[group]))
    group_dim = n // len(POOL_WINDOWS)
    for gi in range(group - 1, -1, -1):
        mean = jnp.where(lane < (gi + 1) * group_dim,
                         sums[gi] / jnp.minimum(pos1, float(POOL_WINDOWS[gi])), mean)
    pooled = (mean - u).astype(BF16)
    o_ref[0] = (jnp.dot(pooled, w_ref[...], preferred_element_type=F32) * s_ref[...]).astype(BF16)


def _pool_call(pu, w_bd, scale):
    bsz, seq, n = pu.shape
    return pl.pallas_call(
        _pool_kernel,
        out_shape=jax.ShapeDtypeStruct((bsz, seq, n), BF16),
        grid=(bsz,),
        in_specs=[pl.BlockSpec((1, seq, n), lambda b: (b, 0, 0)),
                  pl.BlockSpec((n, n), lambda b: (0, 0)),
                  pl.BlockSpec((1, n), lambda b: (0, 0))],
        out_specs=pl.BlockSpec((1, seq, n), lambda b: (b, 0, 0)),
        compiler_params=_cparams(("arbitrary",)),
        name="multiscale_pool",
    )(pu, w_bd, scale)


def _route(logits):
    lane = _iota(logits.shape, 1)
    lane_f = lane.astype(F32)
    big = float(LANES)
    is_g = lane < N_GROUPS
    gl = jnp.where(is_g, logits, -jnp.inf)
    gmax = jnp.max(gl, axis=-1, keepdims=True)
    gsum = jnp.sum(jnp.where(is_g, jnp.exp(gl - gmax), 0.0), axis=-1, keepdims=True)
    p_group = 1.0 / gsum
    g_sel = jnp.min(jnp.where(gl == gmax, lane_f, big), axis=-1, keepdims=True)
    lo = ROUTER_EXPERT_LANE0 + EXPERTS_PER_GROUP * g_sel
    in_group = (lane_f >= lo) & (lane_f < lo + EXPERTS_PER_GROUP)
    el = jnp.where(in_group, logits, -jnp.inf)
    m1 = jnp.max(el, axis=-1, keepdims=True)
    i1 = jnp.min(jnp.where(el == m1, lane_f, big), axis=-1, keepdims=True)
    el2 = jnp.where(lane_f == i1, -jnp.inf, el)
    m2 = jnp.max(el2, axis=-1, keepdims=True)
    i2 = jnp.min(jnp.where(el2 == m2, lane_f, big), axis=-1, keepdims=True)
    e2 = jnp.exp(m2 - m1)
    w1 = p_group / (1.0 + e2)
    w2 = p_group * e2 / (1.0 + e2)
    first_low = i1 < i2
    a = jnp.minimum(i1, i2) - lo
    b = jnp.maximum(i1, i2) - lo
    pair = jnp.where(a == 0.0, b - 1.0, jnp.where(a == 1.0, b + 1.0, float(N_PAIRS - 1)))
    cls_lane = ROUTE_CLASS_LANE0 + g_sel * N_PAIRS + pair
    return (jnp.where(lane == 0, jnp.where(first_low, w1, w2), 0.0)
            + jnp.where(lane == 1, jnp.where(first_low, w2, w1), 0.0)
            + jnp.where(lane_f == cls_lane, 1.0, 0.0))


def _outproj_kernel(of_ref, oh_ref, op_ref, x_ref, w_ref, mod_ref, g_ref, wr_ref, br_ref,
                    xn_ref, h2_ref, *, n_fox, n_hgrn):
    d = x_ref.shape[-1]
    y = jnp.dot(of_ref[...], w_ref[0:n_fox, :], preferred_element_type=F32)
    y = y + jnp.dot(oh_ref[...], w_ref[n_fox:n_fox + n_hgrn, :], preferred_element_type=F32)
    y = y + jnp.dot(op_ref[...], w_ref[n_fox + n_hgrn:, :], preferred_element_type=F32)
    xn = x_ref[...] + (1.0 + mod_ref[0, 2:3, :]) * y
    xn_ref[...] = xn
    h2 = _norm_mod(xn, g_ref[...], mod_ref[0, 3:4, :], mod_ref[0, 4:5, :])
    h2_ref[:, 0:d] = h2
    h_hi = h2.astype(BF16)
    h_lo = (h2 - h_hi.astype(F32)).astype(BF16)
    logits = (jnp.dot(h_hi, wr_ref[0], preferred_element_type=F32)
              + jnp.dot(h_lo, wr_ref[0], preferred_element_type=F32)
              + jnp.dot(h_hi, wr_ref[1], preferred_element_type=F32)) + br_ref[...]
    h2_ref[:, d:] = _route(logits)


def _outproj_call(o_fox, o_hgrn, o_pool, x, w_out, mod, g2, w_router, b_router, tm, seq):
    t, d = x.shape
    n_fox, n_hgrn, n_pool = o_fox.shape[-1], o_hgrn.shape[-1], o_pool.shape[-1]
    per_seq = seq // tm
    return pl.pallas_call(
        functools.partial(_outproj_kernel, n_fox=n_fox, n_hgrn=n_hgrn),
        out_shape=(jax.ShapeDtypeStruct((t, d), F32),
                   jax.ShapeDtypeStruct((t, d + LANES), F32)),
        grid=(t // tm,),
        in_specs=[
            pl.BlockSpec((tm, n_fox), lambda i: (i, 0)),
            pl.BlockSpec((tm, n_hgrn), lambda i: (i, 0)),
            pl.BlockSpec((tm, n_pool), lambda i: (i, 0)),
            pl.BlockSpec((tm, d), lambda i: (i, 0)),
            pl.BlockSpec((d, d), lambda i: (0, 0)),
            pl.BlockSpec((1, N_MOD, d), lambda i: (i // per_seq, 0, 0)),
            pl.BlockSpec((1, d), lambda i: (0, 0)),
            pl.BlockSpec((2, d, LANES), lambda i: (0, 0, 0)),
            pl.BlockSpec((1, LANES), lambda i: (0, 0)),
        ],
        out_specs=(pl.BlockSpec((tm, d), lambda i: (i, 0)),
                   pl.BlockSpec((tm, d + LANES), lambda i: (i, 0))),
        compiler_params=_cparams(("arbitrary",)),
        name="outproj_router",
    )(o_fox, o_hgrn, o_pool, x, w_out, mod, g2, w_router, b_router)


def _sort_kernel(route_ref, pos_ref, meta_ref, *, tile_rows, cb):
    t = route_ref.shape[0]
    lane = _iota((1, LANES), 1)
    is_cls = (lane >= ROUTE_CLASS_LANE0) & (lane < ROUTE_CLASS_LANE0 + N_CLASSES)
    nblk = t // cb
    zero_row = jnp.zeros((1, LANES), F32)

    def onehot(i):
        return jnp.where(is_cls, route_ref[pl.ds(pl.multiple_of(i * cb, cb), cb), :], 0.0)

    counts = lax.fori_loop(0, nblk, lambda i, acc: acc + jnp.sum(onehot(i), axis=0, keepdims=True), zero_row)
    padded = jnp.floor((counts + float(tile_rows - 1)) * (1.0 / tile_rows)) * float(tile_rows)
    before = (_iota((LANES, LANES), 0) < _iota((LANES, LANES), 1)).astype(F32)
    offs = jnp.dot(jnp.broadcast_to(padded, (8, LANES)), before, precision=HIGHEST,
                   preferred_element_type=F32)[0:1]
    ends = offs + padded
    strict = (_iota((cb, cb), 0) > _iota((cb, cb), 1)).astype(BF16)
    ones8 = jnp.ones((8, LANES), F32)
    nt = (((1,), (1,)), ((), ()))

    def place(i, seen):
        oh = onehot(i)
        rank = jnp.dot(strict, oh.astype(BF16), preferred_element_type=F32) + seen
        dest = oh * (rank + offs)
        rows = lax.dot_general(ones8, dest, nt, precision=HIGHEST, preferred_element_type=F32)
        pos_ref[:, pl.ds(pl.multiple_of(i * cb, cb), cb)] = rows.astype(jnp.int32)
        return seen + jnp.sum(oh, axis=0, keepdims=True)

    lax.fori_loop(0, nblk, place, zero_row)
    tile_start = _iota((LANES, LANES), 0).astype(F32) * float(tile_rows)
    tile_cls = jnp.sum(jnp.where(is_cls & (ends <= tile_start), 1.0, 0.0), axis=-1, keepdims=True)
    n_used = jnp.sum(jnp.where(lane == ROUTE_CLASS_LANE0 + N_CLASSES - 1, ends, 0.0), axis=-1,
                     keepdims=True) * (1.0 / tile_rows)
    meta_ref[...] = jnp.where(lane == 0, tile_cls, jnp.where(lane == 1, n_used, 0.0))


def _sort_call(h2ext, d, tile_rows):
    t = h2ext.shape[0]
    cb = min(512, t)
    return pl.pallas_call(
        functools.partial(_sort_kernel, tile_rows=tile_rows, cb=cb),
        out_shape=(jax.ShapeDtypeStruct((8, t), jnp.int32),
                   jax.ShapeDtypeStruct((LANES, LANES), F32)),
        grid=(1,),
        in_specs=[pl.BlockSpec((t, LANES), lambda i: (0, d // LANES))],
        out_specs=(pl.BlockSpec((8, t), lambda i: (0, 0)),
                   pl.BlockSpec((LANES, LANES), lambda i: (0, 0))),
        compiler_params=_cparams(("arbitrary",)),
        name="route_sort",
    )(h2ext)


def _dispatch_kernel(pos_ref, h_ref, init_ref, out_ref, sem, *, tm):
    del init_ref
    base = pl.program_id(0) * tm

    def issue(g, carry):
        for u in range(SUBLANES):
            dst = pos_ref[base + g * SUBLANES + u]
            pltpu.make_async_copy(h_ref.at[g, pl.ds(u, 1), :], out_ref.at[pl.ds(dst, 1), :], sem).start()
        return carry

    lax.fori_loop(0, tm // SUBLANES, issue, 0)
    pltpu.make_async_copy(h_ref, h_ref, sem).wait()


def _dispatch_call(pos, h2ext, n_rows, tm):
    t, w = h2ext.shape
    return pl.pallas_call(
        functools.partial(_dispatch_kernel, tm=tm),
        out_shape=jax.ShapeDtypeStruct((n_rows, w), F32),
        grid_spec=pltpu.PrefetchScalarGridSpec(
            num_scalar_prefetch=1,
            grid=(t // tm,),
            in_specs=[pl.BlockSpec((tm // SUBLANES, SUBLANES, w), lambda i, pos: (i, 0, 0)),
                      pl.BlockSpec(memory_space=pl.ANY)],
            out_specs=pl.BlockSpec(memory_space=pl.ANY),
            scratch_shapes=[pltpu.SemaphoreType.DMA],
        ),
        input_output_aliases={2: 0},
        compiler_params=_cparams(("arbitrary",)),
        name="moe_dispatch",
    )(pos, h2ext.reshape(t // SUBLANES, SUBLANES, w), jnp.zeros((n_rows, w), F32))


def _moe_kernel(ea_ref, eb_ref, nu_ref, h_ref, wga_ref, wua_ref, wda_ref, wgb_ref, wub_ref, wdb_ref, o_ref):
    del ea_ref, eb_ref
    d = o_ref.shape[-1]

    @pl.when(pl.program_id(0) < nu_ref[0])
    def _():
        h = h_ref[:, 0:d].astype(BF16)
        tail = h_ref[:, d:]
        lane = _iota((1, LANES), 1)
        w_a = jnp.sum(jnp.where(lane == 0, tail, 0.0), axis=-1, keepdims=True)
        w_b = jnp.sum(jnp.where(lane == 1, tail, 0.0), axis=-1, keepdims=True)

        def expert(wg_ref, wu_ref, wd_ref):
            hidden = _silu(jnp.dot(h, wg_ref[0, 0], preferred_element_type=F32)) * jnp.dot(
                h, wu_ref[0, 0], preferred_element_type=F32)
            return jnp.dot(hidden.astype(BF16), wd_ref[0, 0], preferred_element_type=F32)

        o_ref[...] = w_a * expert(wga_ref, wua_ref, wda_ref) + w_b * expert(wgb_ref, wub_ref, wdb_ref)

    @pl.when(pl.program_id(0) >= nu_ref[0])
    def _():
        o_ref[...] = jnp.zeros_like(o_ref)


def _moe_call(ea, eb, n_used, h_sorted, wg, wu, wd, layer, tile_rows):
    n_rows, w = h_sorted.shape
    _, _, d, f = wg.shape

    def rows(j, ea, eb, nu):
        return (jnp.maximum(jnp.minimum(j, nu[0] - 1), 0), 0)

    def wa(j, ea, eb, nu):
        return (layer, ea[j], 0, 0)

    def wb(j, ea, eb, nu):
        return (layer, eb[j], 0, 0)

    return pl.pallas_call(
        _moe_kernel,
        out_shape=jax.ShapeDtypeStruct((n_rows, d), F32),
        grid_spec=pltpu.PrefetchScalarGridSpec(
            num_scalar_prefetch=3,
            grid=(n_rows // tile_rows,),
            in_specs=[pl.BlockSpec((tile_rows, w), rows),
                      pl.BlockSpec((1, 1, d, f), wa), pl.BlockSpec((1, 1, d, f), wa),
                      pl.BlockSpec((1, 1, f, d), wa),
                      pl.BlockSpec((1, 1, d, f), wb), pl.BlockSpec((1, 1, d, f), wb),
                      pl.BlockSpec((1, 1, f, d), wb)],
            out_specs=pl.BlockSpec((tile_rows, d), lambda j, ea, eb, nu: (j, 0)),
        ),
        compiler_params=_cparams(("arbitrary",)),
        name="moe_experts",
    )(ea, eb, n_used, h_sorted, wg, wu, wd, wg, wu, wd)


def _combine_kernel(pos_ref, y_ref, x_ref, mod_ref, fg_ref, o_ref, buf, sem, *, tm, final):
    i = pl.program_id(0)

    def issue(tile, slot):
        def eight(g, carry):
            for u in range(SUBLANES):
                src = pos_ref[tile * tm + g * SUBLANES + u]
                pltpu.make_async_copy(y_ref.at[pl.ds(src, 1), :], buf.at[slot, g, pl.ds(u, 1), :],
                                      sem.at[slot]).start()
            return carry
        lax.fori_loop(0, tm // SUBLANES, eight, 0)

    @pl.when(i == 0)
    def _():
        issue(0, 0)

    @pl.when(i + 1 < pl.num_programs(0))
    def _():
        issue(i + 1, (i + 1) % 2)

    slot = i % 2
    pltpu.make_async_copy(buf.at[slot], buf.at[slot], sem.at[slot]).wait()
    out = x_ref[...] + (1.0 + mod_ref[0, 5:6, :]) * buf[slot].reshape(x_ref.shape)
    if final:
        ms = jnp.mean(out * out, axis=-1, keepdims=True)
        out = out * lax.rsqrt(ms + EPS) * fg_ref[...]
    o_ref[...] = out


def _combine_call(pos, y_sorted, x, mod, final_g, tm, seq, final):
    t, d = x.shape
    per_seq = seq // tm
    return pl.pallas_call(
        functools.partial(_combine_kernel, tm=tm, final=final),
        out_shape=jax.ShapeDtypeStruct((t, d), F32),
        grid_spec=pltpu.PrefetchScalarGridSpec(
            num_scalar_prefetch=1,
            grid=(t // tm,),
            in_specs=[pl.BlockSpec(memory_space=pl.ANY),
                      pl.BlockSpec((tm, d), lambda i, pos: (i, 0)),
                      pl.BlockSpec((1, N_MOD, d), lambda i, pos: (i // per_seq, 0, 0)),
                      pl.BlockSpec((1, d), lambda i, pos: (0, 0))],
            out_specs=pl.BlockSpec((tm, d), lambda i, pos: (i, 0)),
            scratch_shapes=[pltpu.VMEM((2, tm // SUBLANES, SUBLANES, d), F32), pltpu.SemaphoreType.DMA((2,))],
        ),
        compiler_params=_cparams(("arbitrary",)),
        name="moe_combine",
    )(pos, y_sorted, x, mod, final_g)


def _pad_lanes(a, n=LANES):
    return jnp.pad(a, [(0, 0)] * (a.ndim - 1) + [(0, n - a.shape[-1])])


def kernel(x, c, w_ada, b_ada, norm1_g, w_in, fox_f_bias, fox_norm_g, hgrn_lb_logits, hgrn_norm_g, pool_w, pool_scale, w_out, norm2_g, router_group_w, router_group_b, router_expert_w, router_expert_b, expert_w_gate, expert_w_up, expert_w_down, final_norm_g):
    bsz, seq, d = x.shape
    depth = w_ada.shape[0]
    fox_heads = fox_f_bias.shape[1]
    fox_dim = fox_heads * HEAD_DIM
    hgrn_dim = hgrn_lb_logits.shape[1]
    pool_dim = pool_scale.shape[1]
    n_fox_pairs = fox_dim // LANES
    n_hgrn_pairs = hgrn_dim // LANES
    t = bsz * seq
    tm = min(512, seq)
    tq = min(512, seq)
    n_tiles = t // MOE_TILE_ROWS + N_CLASSES
    assert n_tiles <= LANES and t % MOE_TILE_ROWS == 0

    o_ff = 3 * fox_dim
    w_qkv = w_in[:, :, :o_ff].astype(BF16)
    w_hp = w_in[:, :, o_ff + fox_heads:].astype(BF16)
    w_ff = _pad_lanes(w_in[:, :, o_ff:o_ff + fox_heads]).astype(BF16)
    f_bias = _pad_lanes(fox_f_bias)
    groups = pool_w.shape[1]
    pool_bd = jnp.einsum('lgcd,gh->lgchd', pool_w, jnp.eye(groups, dtype=pool_w.dtype)).reshape(
        depth, pool_dim, pool_dim).astype(BF16)
    w_router = _pad_lanes(jnp.concatenate([router_group_w, router_expert_w], axis=-1))
    w_router_hi = w_router.astype(BF16)
    w_router = jnp.stack([w_router_hi, (w_router - w_router_hi.astype(F32)).astype(BF16)], axis=1)
    b_router = _pad_lanes(jnp.concatenate([router_group_b, router_expert_b], axis=-1))
    w_out_b = w_out.astype(BF16)
    wg_b = expert_w_gate.astype(BF16)
    wu_b = expert_w_up.astype(BF16)
    wd_b = expert_w_down.astype(BF16)
    pair_a = jnp.array([0, 0, 0, 1, 1, 2], jnp.int32)
    pair_b = jnp.array([1, 2, 3, 2, 3, 3], jnp.int32)
    final_g = final_norm_g.reshape(1, d)

    mod_all = _ada_call(c, w_ada, b_ada).reshape(depth, bsz, N_MOD, d)
    lower = _lb_call(hgrn_lb_logits)

    for l in range(depth):
        mod = mod_all[l]
        qkv, hg4, pu, ff = _inproj_call(x, norm1_g[l:l + 1], mod, w_qkv[l], w_hp[l], w_ff[l], 4 * hgrn_dim, tm)
        fcol = _fbias_call(ff, f_bias[l:l + 1])
        o_fox = _fox_call(qkv, fcol, fox_norm_g[l].reshape(n_fox_pairs, 1, LANES), n_fox_pairs, tq)
        o_hgrn = _hgrn_call(hg4, lower[l].reshape(n_hgrn_pairs, 1, LANES),
                            hgrn_norm_g[l].reshape(n_hgrn_pairs, 1, LANES), n_hgrn_pairs)
        o_pool = _pool_call(pu, pool_bd[l], pool_scale[l:l + 1])
        x2, h2ext = _outproj_call(o_fox.reshape(t, -1), o_hgrn.reshape(t, -1), o_pool.reshape(t, -1),
                                  x.reshape(t, d), w_out_b[l], mod, norm2_g[l:l + 1],
                                  w_router[l], b_router[l:l + 1], tm, seq)
        pos8, meta = _sort_call(h2ext, d, MOE_TILE_ROWS)
        pos = pos8[0]
        n_used = meta[0, 1].astype(jnp.int32).reshape(1)
        tile = jnp.minimum(jnp.arange(n_tiles, dtype=jnp.int32), n_used[0] - 1)
        tile_cls = jnp.minimum(meta[:, 0].astype(jnp.int32)[tile], N_CLASSES - 1)
        e0 = (tile_cls // N_PAIRS) * EXPERTS_PER_GROUP
        ea = e0 + pair_a[tile_cls % N_PAIRS]
        eb = e0 + pair_b[tile_cls % N_PAIRS]
        h_sorted = _dispatch_call(pos, h2ext, n_tiles * MOE_TILE_ROWS, tm)
        y_sorted = _moe_call(ea, eb, n_used, h_sorted, wg_b, wu_b, wd_b, l, MOE_TILE_ROWS)
        x = _combine_call(pos, y_sorted, x2, mod, final_g, tm, seq, final=(l == depth - 1)).reshape(bsz, seq, d)
    return x
```

```python
import functools

import jax
import jax.numpy as jnp
from jax import lax
from jax.experimental import pallas as pl
from jax.experimental.pallas import tpu as pltpu

F32 = jnp.float32
BF16 = jnp.bfloat16
HIGHEST = lax.Precision.HIGHEST

HEAD_DIM = 64
LANES = 128
SUBLANES = 8
EPS = 1e-6
N_MOD = 6
POOL_WINDOWS = (2, 4, 8, 16)
N_GROUPS = 4
EXPERTS_PER_GROUP = 4
N_EXPERTS = N_GROUPS * EXPERTS_PER_GROUP
ROUTER_EXPERT_LANE0 = N_GROUPS
N_PAIRS = 6
N_CLASSES = N_GROUPS * N_PAIRS
PAIR_SLOT_A = (0, 0, 0, 1, 1, 3)
PAIR_SLOT_B = (1, 2, 3, 3, 2, 2)
ROUTE_CLASS_LANE0 = 32
MOE_TILE_ROWS = 256
HGRN_CHUNK = 128
HGRN_LEVELS = (64, 32, 16, 8, 4, 2, 1)
HGRN_GROUP = 4
VMEM_LIMIT = 48 * 1024 * 1024
LOG2_E = 1.4426950408889634
FOX_Q_SCALE = HEAD_DIM ** -0.5 * LOG2_E


def _cparams(sem):
    return pltpu.CompilerParams(dimension_semantics=sem, vmem_limit_bytes=VMEM_LIMIT)


def _silu(x):
    return x * jax.nn.sigmoid(x)


def _iota(shape, dim):
    return lax.broadcasted_iota(jnp.int32, shape, dim)


def _ada_kernel(c_ref, w_ref, b_ref, o_ref):
    sc = _silu(c_ref[...]).astype(BF16)
    o_ref[0] = jnp.dot(sc, w_ref[0].astype(BF16), preferred_element_type=F32) + b_ref[0]


def _ada_call(c, w_ada, b_ada):
    depth, d, n = w_ada.shape
    bsz = c.shape[0]
    tn = 1536
    return pl.pallas_call(
        _ada_kernel,
        out_shape=jax.ShapeDtypeStruct((depth, bsz, n), F32),
        grid=(depth, n // tn),
        in_specs=[
            pl.BlockSpec((bsz, d), lambda l, j: (0, 0)),
            pl.BlockSpec((1, d, tn), lambda l, j: (l, 0, j)),
            pl.BlockSpec((1, 1, tn), lambda l, j: (l, 0, j)),
        ],
        out_specs=pl.BlockSpec((1, bsz, tn), lambda l, j: (l, 0, j)),
        compiler_params=_cparams(("arbitrary", "arbitrary")),
        name="ada_mod",
    )(c, w_ada, b_ada.reshape(depth, 1, n))


def _lb_kernel(x_ref, o_ref):
    x = x_ref[...]
    depth = x.shape[0]
    e = jnp.exp(x - jnp.max(x, axis=0, keepdims=True))
    p = e / jnp.sum(e, axis=0, keepdims=True)
    acc = jnp.zeros_like(p[0:1])
    o_ref[0:1, :] = acc
    for l in range(1, depth):
        acc = acc + p[l:l + 1]
        o_ref[l:l + 1, :] = acc


def _lb_call(lb_logits):
    return pl.pallas_call(
        _lb_kernel,
        out_shape=jax.ShapeDtypeStruct(lb_logits.shape, F32),
        name="hgrn_lower_bounds",
    )(lb_logits)


def _norm_mod(x, g, shift, scale):
    ms = jnp.mean(x * x, axis=-1, keepdims=True)
    return (x * lax.rsqrt(ms + EPS) * g) * (1.0 + scale) + shift


def _inproj_kernel(x_ref, g_ref, mod_ref, wq_ref, wh_ref, wf_ref, qkv_ref, hg_ref, pu_ref, ff_ref, *, n_hg):
    h = _norm_mod(x_ref[0], g_ref[...], mod_ref[0, 0:1, :], mod_ref[0, 1:2, :]).astype(BF16)
    n_qkv = qkv_ref.shape[-1]
    q_scale = jnp.where(_iota((1, n_qkv), 1) < n_qkv // 3, FOX_Q_SCALE, 1.0)
    qkv_ref[0] = (jnp.dot(h, wq_ref[...], preferred_element_type=F32) * q_scale).astype(BF16)
    hg_ref[0] = jnp.dot(h, wh_ref[:, 0:n_hg], preferred_element_type=F32)
    pu_ref[0] = jnp.dot(h, wh_ref[:, n_hg:], preferred_element_type=F32)
    ff_ref[0] = jnp.dot(h, wf_ref[...], preferred_element_type=F32)


def _inproj_call(x, g, mod, w_qkv, w_hp, w_ff, n_hg, tm):
    bsz, seq, d = x.shape
    n_qkv, n_ff = w_qkv.shape[1], w_ff.shape[1]
    n_pu = w_hp.shape[1] - n_hg
    return pl.pallas_call(
        functools.partial(_inproj_kernel, n_hg=n_hg),
        out_shape=(
            jax.ShapeDtypeStruct((bsz, seq, n_qkv), BF16),
            jax.ShapeDtypeStruct((bsz, seq, n_hg), F32),
            jax.ShapeDtypeStruct((bsz, seq, n_pu), F32),
            jax.ShapeDtypeStruct((bsz, seq, n_ff), F32),
        ),
        grid=(bsz, seq // tm),
        in_specs=[
            pl.BlockSpec((1, tm, d), lambda b, i: (b, i, 0)),
            pl.BlockSpec((1, d), lambda b, i: (0, 0)),
            pl.BlockSpec((1, N_MOD, d), lambda b, i: (b, 0, 0)),
            pl.BlockSpec((d, n_qkv), lambda b, i: (0, 0)),
            pl.BlockSpec((d, n_hg + n_pu), lambda b, i: (0, 0)),
            pl.BlockSpec((d, n_ff), lambda b, i: (0, 0)),
        ],
        out_specs=(
            pl.BlockSpec((1, tm, n_qkv), lambda b, i: (b, i, 0)),
            pl.BlockSpec((1, tm, n_hg), lambda b, i: (b, i, 0)),
            pl.BlockSpec((1, tm, n_pu), lambda b, i: (b, i, 0)),
            pl.BlockSpec((1, tm, n_ff), lambda b, i: (b, i, 0)),
        ),
        compiler_params=_cparams(("arbitrary", "arbitrary")),
        name="norm_inproj",
    )(x, g, mod, w_qkv, w_hp, w_ff)


def _fbias_kernel(ff_ref, bias_ref, fcol_ref, *, cb):
    seq = ff_ref.shape[1]
    tri = (_iota((cb, cb), 0) >= _iota((cb, cb), 1)).astype(F32)
    carry = jnp.zeros((1, LANES), F32)
    for blk in range(seq // cb):
        x = ff_ref[0, blk * cb:(blk + 1) * cb, :] + bias_ref[...]
        log_f = jnp.minimum(x, 0.0) - jnp.log1p(jnp.exp(-jnp.abs(x)))
        cs = jnp.dot(tri, log_f, precision=HIGHEST, preferred_element_type=F32) + carry
        fcol_ref[0, blk * cb:(blk + 1) * cb, :] = cs * LOG2_E
        carry = cs[cb - 1:cb, :]


def _fbias_call(ff, bias):
    bsz, seq, _ = ff.shape
    cb = min(256, seq)
    return pl.pallas_call(
        functools.partial(_fbias_kernel, cb=cb),
        out_shape=jax.ShapeDtypeStruct((bsz, seq, LANES), F32),
        grid=(bsz,),
        in_specs=[pl.BlockSpec((1, seq, LANES), lambda b: (b, 0, 0)),
                  pl.BlockSpec((1, LANES), lambda b: (0, 0))],
        out_specs=pl.BlockSpec((1, seq, LANES), lambda b: (b, 0, 0)),
        compiler_params=_cparams(("arbitrary",)),
        name="fox_forget_bias",
    )(ff, bias)


def _split3(f):
    hi = f.astype(BF16).astype(F32)
    rest = f - hi
    mid = rest.astype(BF16).astype(F32)
    return hi, mid, rest - mid


def _fox_kernel(q_ref, k_ref, v_ref, fc_ref, gn_ref, o_ref, k0_sc, k1_sc, v0_sc, v1_sc, *, tq):
    p = pl.program_id(1)
    seq = k_ref.shape[1]
    nb = seq // tq
    lane = _iota((1, LANES), 1)
    h0 = lane < HEAD_DIM
    data = (h0, jnp.logical_not(h0))
    base = (HEAD_DIM, 0)
    k_sc = (k0_sc, k1_sc)
    v_sc = (v0_sc, v1_sc)
    nt = (((1,), (1,)), ((), ()))
    heads = (0, 1)

    def head_f(fc, which):
        return jnp.sum(jnp.where(lane == 2 * p + which, fc, 0.0), axis=-1, keepdims=True)

    def with_bias_lanes(x, which, first3, last3):
        out = jnp.where(data[which], x, 0.0)
        for n in range(3):
            out = jnp.where(lane == base[which] + n, first3[n], out)
            out = jnp.where(lane == base[which] + 3 + n, last3[n], out)
        return out.astype(BF16)

    qs = {}
    for blk in range(nb):
        rows = slice(blk * tq, (blk + 1) * tq)
        fc = fc_ref[0, rows, :]
        qf = q_ref[0, rows, :].astype(F32)
        kf = k_ref[0, rows, :].astype(F32)
        vf = v_ref[0, rows, :].astype(F32)
        for which in heads:
            hi, mid, lo = _split3(head_f(fc, which))
            qs[blk, which] = with_bias_lanes(qf, which, (hi, mid, lo), (1.0, 1.0, 1.0))
            k_sc[which][rows, :] = with_bias_lanes(kf, which, (1.0, 1.0, 1.0), (-hi, -mid, -lo))
            v_sc[which][rows, :] = jnp.where(data[which], vf,
                                             jnp.where(lane == base[which], 1.0, 0.0)).astype(BF16)

    causal = _iota((tq, tq), 0) >= _iota((tq, tq), 1)
    m, acc = {}, {}
    for r in range(nb):
        keys = slice(r * tq, (r + 1) * tq)
        tasks = [(i, which) for i in range(r, nb) for which in heads]
        s = {}
        for i, which in tasks:
            s[i, which] = lax.dot_general(qs[i, which], k_sc[which][keys, :], nt, preferred_element_type=F32)
            if i == r:
                s[i, which] = jnp.where(causal, s[i, which], -jnp.inf)
        n = {}
        for t in tasks:
            n[t] = jnp.max(s[t], axis=-1, keepdims=True)
            if r > 0:
                n[t] = jnp.maximum(m[t], n[t])
        pv = {}
        for i, which in tasks:
            pr = jnp.exp2(s[i, which] - n[i, which]).astype(BF16)
            pv[i, which] = jnp.dot(pr, v_sc[which][keys, :], preferred_element_type=F32)
        for t in tasks:
            acc[t] = pv[t] if r == 0 else jnp.exp2(m[t] - n[t]) * acc[t] + pv[t]
            m[t] = n[t]

        l0 = jnp.sum(jnp.where(lane == base[0], acc[r, 0], 0.0), axis=-1, keepdims=True)
        l1 = jnp.sum(jnp.where(lane == base[1], acc[r, 1], 0.0), axis=-1, keepdims=True)
        o = jnp.where(h0, acc[r, 0] / l0, acc[r, 1] / l1)
        o2 = o * o
        ms0 = jnp.sum(jnp.where(h0, o2, 0.0), axis=-1, keepdims=True)
        ms1 = jnp.sum(jnp.where(h0, 0.0, o2), axis=-1, keepdims=True)
        ms = jnp.where(h0, ms0, ms1) * (1.0 / HEAD_DIM)
        o_ref[0, keys, :] = (o * lax.rsqrt(ms + EPS) * gn_ref[0]).astype(BF16)


def _fox_call(qkv, fcol, gn, n_pairs, tq):
    bsz, seq, _ = qkv.shape
    return pl.pallas_call(
        functools.partial(_fox_kernel, tq=tq),
        out_shape=jax.ShapeDtypeStruct((bsz, seq, n_pairs * LANES), BF16),
        grid=(bsz, n_pairs),
        in_specs=[
            pl.BlockSpec((1, seq, LANES), lambda b, p: (b, 0, p)),
            pl.BlockSpec((1, seq, LANES), lambda b, p: (b, 0, n_pairs + p)),
            pl.BlockSpec((1, seq, LANES), lambda b, p: (b, 0, 2 * n_pairs + p)),
            pl.BlockSpec((1, seq, LANES), lambda b, p: (b, 0, 0)),
            pl.BlockSpec((1, 1, LANES), lambda b, p: (p, 0, 0)),
        ],
        out_specs=pl.BlockSpec((1, seq, LANES), lambda b, p: (b, 0, p)),
        scratch_shapes=[pltpu.VMEM((seq, LANES), BF16)] * 4,
        compiler_params=_cparams(("arbitrary", "arbitrary")),
        name="fox_attention",
    )(qkv, qkv, qkv, fcol, gn)


def _hgrn_kernel(hq_ref, hf_ref, hi_ref, hg_ref, lb_ref, gn_ref, o_ref):
    ch = HGRN_CHUNK
    seq = hq_ref.shape[1]
    lane = _iota((1, LANES), 1)
    h0 = lane < HEAD_DIM
    r = _iota((ch, ch), 0)
    c = _iota((ch, ch), 1)
    tri = (r >= c).astype(BF16)
    same_head = (r < HEAD_DIM) == (c < HEAD_DIM)
    seg = same_head.astype(BF16)
    r2 = _iota((ch, 2 * ch), 0)
    c2 = _iota((ch, 2 * ch), 1) % ch
    level_masks = [(r2 // (2 * m) == c2 // (2 * m)) & (r2 % (2 * m) >= m) & (c2 % (2 * m) < m)
                   for m in HGRN_LEVELS]
    small_levels = [m for m in HGRN_LEVELS if m < SUBLANES]
    pick = jnp.concatenate([(c == (r // (2 * m)) * (2 * m) + (m - 1)) for m in small_levels], axis=0).astype(BF16)
    lb = lb_ref[0]
    nt = (((1,), (1,)), ((), ()))
    zero_b = jnp.zeros((ch, LANES), BF16)

    def both_heads(x):
        return jnp.concatenate([jnp.where(h0, x, zero_b), jnp.where(h0, zero_b, x)], axis=0)

    def sum3(x):
        return x[:, 0:LANES] + x[:, LANES:2 * LANES] + x[:, 2 * LANES:]

    def parts3(x):
        return jnp.concatenate([part.astype(BF16) for part in _split3(x)], axis=1)

    def group(gi, state_t):
        n = HGRN_GROUP
        sls = [pl.ds(pl.multiple_of((gi * n + i) * ch, ch), ch) for i in range(n)]
        k, q, v, vb, b, picked = [], [], [], [], [], []
        for sl in sls:
            f = lb + (1.0 - lb) * jax.nn.sigmoid(hf_ref[0, sl, :])
            k.append(1.0 - f)
            q.append(_silu(hq_ref[0, sl, :]))
            v.append(hi_ref[0, sl, :])
            vb.append(v[-1].astype(BF16))
            b.append(sum3(jnp.dot(tri, parts3(jnp.log(f) * LOG2_E), preferred_element_type=F32)))
        for i in range(n):
            picked.append(jnp.dot(pick, parts3(b[i]), preferred_element_type=F32))

        sc = [jnp.zeros((ch, 2 * ch), F32) for _ in range(n)]
        for li, m in enumerate(HGRN_LEVELS):
            for i in range(n):
                if m >= SUBLANES:
                    ref_l = jnp.broadcast_to(b[i].reshape(ch // (2 * m), 2 * m, LANES)[:, m - 1:m, :],
                                             (ch // (2 * m), 2 * m, LANES)).reshape(ch, LANES)
                else:
                    at = small_levels.index(m) * ch
                    ref_l = sum3(picked[i][at:at + ch])
                e = jnp.exp2(-jnp.abs(b[i] - ref_l))
                s = lax.dot_general((q[i] * e).astype(BF16), both_heads((k[i] * e).astype(BF16)), nt,
                                    preferred_element_type=F32)
                sc[i] = jnp.where(level_masks[li], s, sc[i])

        o, upd_t, dec = [], [], []
        for i in range(n):
            oi = jnp.dot(sc[i].astype(BF16), both_heads(vb[i]), preferred_element_type=F32)
            o.append(oi + jnp.dot((q[i] * k[i]).astype(BF16), seg, preferred_element_type=F32) * v[i])
            b_last = b[i][ch - 1:ch, :]
            k_dec = (k[i] * jnp.exp2(b_last - b[i])).astype(BF16)
            upd = jnp.dot(v[i].T.astype(BF16), k_dec, preferred_element_type=F32)
            upd_t.append(jnp.where(same_head, upd, 0.0))
            dec.append(jnp.exp2(b_last))

        states = [state_t]
        for i in range(n):
            states.append(dec[i] * states[i] + upd_t[i])

        for i, sl in enumerate(sls):
            oi = o[i] + lax.dot_general((q[i] * jnp.exp2(b[i])).astype(BF16), states[i].astype(BF16), nt,
                                        preferred_element_type=F32)
            o2 = oi * oi
            ms0 = jnp.sum(jnp.where(h0, o2, 0.0), axis=-1, keepdims=True)
            ms1 = jnp.sum(jnp.where(h0, 0.0, o2), axis=-1, keepdims=True)
            ms = jnp.where(h0, ms0, ms1) * (1.0 / HEAD_DIM)
            out = oi * lax.rsqrt(ms + EPS) * gn_ref[0] * _silu(hg_ref[0, sl, :])
            o_ref[0, sl, :] = out.astype(BF16)
        return states[n]

    lax.fori_loop(0, seq // (HGRN_GROUP * ch), group, jnp.zeros((LANES, LANES), F32))


def _hgrn_call(hg4, lb, gn, n_pairs):
    bsz, seq, _ = hg4.shape

    def spec(off):
        return pl.BlockSpec((1, seq, LANES), lambda b, p: (b, 0, off + p))

    return pl.pallas_call(
        _hgrn_kernel,
        out_shape=jax.ShapeDtypeStruct((bsz, seq, n_pairs * LANES), BF16),
        grid=(bsz, n_pairs),
        in_specs=[spec(0), spec(n_pairs), spec(2 * n_pairs), spec(3 * n_pairs),
                  pl.BlockSpec((1, 1, LANES), lambda b, p: (p, 0, 0)),
                  pl.BlockSpec((1, 1, LANES), lambda b, p: (p, 0, 0))],
        out_specs=pl.BlockSpec((1, seq, LANES), lambda b, p: (b, 0, p)),
        compiler_params=_cparams(("arbitrary", "arbitrary")),
        name="hgrn2",
    )(hg4, hg4, hg4, hg4, lb, gn)


def _pool_kernel(u_ref, w_ref, s_ref, o_ref):
    u = u_ref[0]
    seq, n = u.shape
    t = _iota((seq, 1), 0)
    lane = _iota((1, n), 1)

    def shifted(x, k):
        return jnp.where(t >= k, pltpu.roll(x, k, axis=0), 0.0)

    sums = []
    s = u
    for w in POOL_WINDOWS:
        s = s + shifted(s, w // 2)
        sums.append(s)
    pos1 = (t + 1).astype(F32)
    group = len(POOL_WINDOWS) - 1
    mean = sums[group] / jnp.minimum(pos1, float(POOL_WINDOWS[group]))
    group_dim = n // len(POOL_WINDOWS)
    for gi in range(group - 1, -1, -1):
        mean = jnp.where(lane < (gi + 1) * group_dim,
                         sums[gi] / jnp.minimum(pos1, float(POOL_WINDOWS[gi])), mean)
    pooled = (mean - u).astype(BF16)
    o_ref[0] = (jnp.dot(pooled, w_ref[...], preferred_element_type=F32) * s_ref[...]).astype(BF16)


def _pool_call(pu, w_bd, scale):
    bsz, seq, n = pu.shape
    return pl.pallas_call(
        _pool_kernel,
        out_shape=jax.ShapeDtypeStruct((bsz, seq, n), BF16),
        grid=(bsz,),
        in_specs=[pl.BlockSpec((1, seq, n), lambda b: (b, 0, 0)),
                  pl.BlockSpec((n, n), lambda b: (0, 0)),
                  pl.BlockSpec((1, n), lambda b: (0, 0))],
        out_specs=pl.BlockSpec((1, seq, n), lambda b: (b, 0, 0)),
        compiler_params=_cparams(("arbitrary",)),
        name="multiscale_pool",
    )(pu, w_bd, scale)


def _route(logits):
    lane = _iota(logits.shape, 1)
    lane_f = lane.astype(F32)
    big = float(LANES)
    is_g = lane < N_GROUPS
    gl = jnp.where(is_g, logits, -jnp.inf)
    gmax = jnp.max(gl, axis=-1, keepdims=True)
    gsum = jnp.sum(jnp.where(is_g, jnp.exp(gl - gmax), 0.0), axis=-1, keepdims=True)
    p_group = 1.0 / gsum
    g_sel = jnp.min(jnp.where(gl == gmax, lane_f, big), axis=-1, keepdims=True)
    lo = ROUTER_EXPERT_LANE0 + EXPERTS_PER_GROUP * g_sel
    in_group = (lane_f >= lo) & (lane_f < lo + EXPERTS_PER_GROUP)
    el = jnp.where(in_group, logits, -jnp.inf)
    m1 = jnp.max(el, axis=-1, keepdims=True)
    i1 = jnp.min(jnp.where(el == m1, lane_f, big), axis=-1, keepdims=True)
    el2 = jnp.where(lane_f == i1, -jnp.inf, el)
    m2 = jnp.max(el2, axis=-1, keepdims=True)
    i2 = jnp.min(jnp.where(el2 == m2, lane_f, big), axis=-1, keepdims=True)
    e2 = jnp.exp(m2 - m1)
    w1 = p_group / (1.0 + e2)
    w2 = p_group * e2 / (1.0 + e2)
    a = jnp.minimum(i1, i2) - lo
    b = jnp.maximum(i1, i2) - lo
    pair = jnp.where(a == 0.0, b - 1.0, jnp.where(a == 1.0, 6.0 - b, float(N_PAIRS - 1)))
    top1_in_a = (i1 < i2) != (pair == float(N_PAIRS - 1))
    cls_lane = ROUTE_CLASS_LANE0 + g_sel * N_PAIRS + pair
    return (jnp.where(lane == 0, jnp.where(top1_in_a, w1, w2), 0.0)
            + jnp.where(lane == 1, jnp.where(top1_in_a, w2, w1), 0.0)
            + jnp.where(lane_f == cls_lane, 1.0, 0.0))


def _outproj_kernel(of_ref, oh_ref, op_ref, x_ref, w_ref, mod_ref, g_ref, wr_ref, br_ref,
                    xn_ref, h2_ref, *, n_fox, n_hgrn):
    d = x_ref.shape[-1]
    y = jnp.dot(of_ref[...], w_ref[0:n_fox, :], preferred_element_type=F32)
    y = y + jnp.dot(oh_ref[...], w_ref[n_fox:n_fox + n_hgrn, :], preferred_element_type=F32)
    y = y + jnp.dot(op_ref[...], w_ref[n_fox + n_hgrn:, :], preferred_element_type=F32)
    xn = x_ref[...] + (1.0 + mod_ref[0, 2:3, :]) * y
    xn_ref[...] = xn
    h2 = _norm_mod(xn, g_ref[...], mod_ref[0, 3:4, :], mod_ref[0, 4:5, :])
    h2_ref[:, 0:d] = h2
    h_hi = h2.astype(BF16)
    h_lo = (h2 - h_hi.astype(F32)).astype(BF16)
    logits = (jnp.dot(h_hi, wr_ref[0], preferred_element_type=F32)
              + jnp.dot(h_lo, wr_ref[0], preferred_element_type=F32)
              + jnp.dot(h_hi, wr_ref[1], preferred_element_type=F32)) + br_ref[...]
    h2_ref[:, d:] = _route(logits)


def _outproj_call(o_fox, o_hgrn, o_pool, x, w_out, mod, g2, w_router, b_router, tm, seq):
    t, d = x.shape
    n_fox, n_hgrn, n_pool = o_fox.shape[-1], o_hgrn.shape[-1], o_pool.shape[-1]
    per_seq = seq // tm
    return pl.pallas_call(
        functools.partial(_outproj_kernel, n_fox=n_fox, n_hgrn=n_hgrn),
        out_shape=(jax.ShapeDtypeStruct((t, d), F32),
                   jax.ShapeDtypeStruct((t, d + LANES), F32)),
        grid=(t // tm,),
        in_specs=[
            pl.BlockSpec((tm, n_fox), lambda i: (i, 0)),
            pl.BlockSpec((tm, n_hgrn), lambda i: (i, 0)),
            pl.BlockSpec((tm, n_pool), lambda i: (i, 0)),
            pl.BlockSpec((tm, d), lambda i: (i, 0)),
            pl.BlockSpec((d, d), lambda i: (0, 0)),
            pl.BlockSpec((1, N_MOD, d), lambda i: (i // per_seq, 0, 0)),
            pl.BlockSpec((1, d), lambda i: (0, 0)),
            pl.BlockSpec((2, d, LANES), lambda i: (0, 0, 0)),
            pl.BlockSpec((1, LANES), lambda i: (0, 0)),
        ],
        out_specs=(pl.BlockSpec((tm, d), lambda i: (i, 0)),
                   pl.BlockSpec((tm, d + LANES), lambda i: (i, 0))),
        compiler_params=_cparams(("arbitrary",)),
        name="outproj_router",
    )(o_fox, o_hgrn, o_pool, x, w_out, mod, g2, w_router, b_router)


def _sort_kernel(route_ref, pos_ref, meta_ref, *, tile_rows, cb):
    t = route_ref.shape[0]
    lane = _iota((1, LANES), 1)
    is_cls = (lane >= ROUTE_CLASS_LANE0) & (lane < ROUTE_CLASS_LANE0 + N_CLASSES)
    nblk = t // cb
    zero_row = jnp.zeros((1, LANES), F32)

    def onehot(i):
        return jnp.where(is_cls, route_ref[pl.ds(pl.multiple_of(i * cb, cb), cb), :], 0.0)

    counts = lax.fori_loop(0, nblk, lambda i, acc: acc + jnp.sum(onehot(i), axis=0, keepdims=True), zero_row)
    padded = jnp.floor((counts + float(tile_rows - 1)) * (1.0 / tile_rows)) * float(tile_rows)
    before = (_iota((LANES, LANES), 0) < _iota((LANES, LANES), 1)).astype(F32)
    offs = jnp.dot(jnp.broadcast_to(padded, (8, LANES)), before, precision=HIGHEST,
                   preferred_element_type=F32)[0:1]
    ends = offs + padded
    strict = (_iota((cb, cb), 0) > _iota((cb, cb), 1)).astype(BF16)
    ones8 = jnp.ones((8, LANES), F32)
    nt = (((1,), (1,)), ((), ()))

    def place(i, seen):
        oh = onehot(i)
        rank = jnp.dot(strict, oh.astype(BF16), preferred_element_type=F32) + seen
        dest = oh * (rank + offs)
        rows = lax.dot_general(ones8, dest, nt, precision=HIGHEST, preferred_element_type=F32)
        pos_ref[:, pl.ds(pl.multiple_of(i * cb, cb), cb)] = rows.astype(jnp.int32)
        return seen + jnp.sum(oh, axis=0, keepdims=True)

    lax.fori_loop(0, nblk, place, zero_row)
    tile_start = _iota((LANES, LANES), 0).astype(F32) * float(tile_rows)
    tile_cls = jnp.sum(jnp.where(is_cls & (ends <= tile_start), 1.0, 0.0), axis=-1, keepdims=True)
    n_used = jnp.sum(jnp.where(lane == ROUTE_CLASS_LANE0 + N_CLASSES - 1, ends, 0.0), axis=-1,
                     keepdims=True) * (1.0 / tile_rows)
    meta_ref[...] = jnp.where(lane == 0, tile_cls, jnp.where(lane == 1, n_used, 0.0))


def _sort_call(h2ext, d, tile_rows):
    t = h2ext.shape[0]
    cb = min(512, t)
    return pl.pallas_call(
        functools.partial(_sort_kernel, tile_rows=tile_rows, cb=cb),
        out_shape=(jax.ShapeDtypeStruct((8, t), jnp.int32),
                   jax.ShapeDtypeStruct((LANES, LANES), F32)),
        grid=(1,),
        in_specs=[pl.BlockSpec((t, LANES), lambda i: (0, d // LANES))],
        out_specs=(pl.BlockSpec((8, t), lambda i: (0, 0)),
                   pl.BlockSpec((LANES, LANES), lambda i: (0, 0))),
        compiler_params=_cparams(("arbitrary",)),
        name="route_sort",
    )(h2ext)


def _dispatch_kernel(pos_ref, h_ref, init_ref, out_ref, sem, *, tm):
    del init_ref
    base = pl.program_id(0) * tm

    def issue(g, carry):
        for u in range(SUBLANES):
            dst = pos_ref[base + g * SUBLANES + u]
            pltpu.make_async_copy(h_ref.at[g, pl.ds(u, 1), :], out_ref.at[pl.ds(dst, 1), :], sem).start()
        return carry

    lax.fori_loop(0, tm // SUBLANES, issue, 0)
    pltpu.make_async_copy(h_ref, h_ref, sem).wait()


def _dispatch_call(pos, h2ext, n_rows, tm):
    t, w = h2ext.shape
    return pl.pallas_call(
        functools.partial(_dispatch_kernel, tm=tm),
        out_shape=jax.ShapeDtypeStruct((n_rows, w), F32),
        grid_spec=pltpu.PrefetchScalarGridSpec(
            num_scalar_prefetch=1,
            grid=(t // tm,),
            in_specs=[pl.BlockSpec((tm // SUBLANES, SUBLANES, w), lambda i, pos: (i, 0, 0)),
                      pl.BlockSpec(memory_space=pl.ANY)],
            out_specs=pl.BlockSpec(memory_space=pl.ANY),
            scratch_shapes=[pltpu.SemaphoreType.DMA],
        ),
        input_output_aliases={2: 0},
        compiler_params=_cparams(("arbitrary",)),
        name="moe_dispatch",
    )(pos, h2ext.reshape(t // SUBLANES, SUBLANES, w), jnp.zeros((n_rows, w), F32))


def _moe_kernel(ea_ref, eb_ref, nu_ref, h_ref, wga_ref, wua_ref, wda_ref, wgb_ref, wub_ref, wdb_ref, o_ref):
    del ea_ref, eb_ref
    d = o_ref.shape[-1]

    @pl.when(pl.program_id(0) < nu_ref[0])
    def _():
        h = h_ref[:, 0:d].astype(BF16)
        tail = h_ref[:, d:]
        lane = _iota((1, LANES), 1)
        w_a = jnp.sum(jnp.where(lane == 0, tail, 0.0), axis=-1, keepdims=True)
        w_b = jnp.sum(jnp.where(lane == 1, tail, 0.0), axis=-1, keepdims=True)

        def expert(wg_ref, wu_ref, wd_ref):
            hidden = _silu(jnp.dot(h, wg_ref[0, 0], preferred_element_type=F32)) * jnp.dot(
                h, wu_ref[0, 0], preferred_element_type=F32)
            return jnp.dot(hidden.astype(BF16), wd_ref[0, 0], preferred_element_type=F32)

        o_ref[...] = w_a * expert(wga_ref, wua_ref, wda_ref) + w_b * expert(wgb_ref, wub_ref, wdb_ref)

    @pl.when(pl.program_id(0) >= nu_ref[0])
    def _():
        o_ref[...] = jnp.zeros_like(o_ref)


def _moe_call(ea, eb, n_used, h_sorted, wg, wu, wd, layer, tile_rows):
    n_rows, w = h_sorted.shape
    _, _, d, f = wg.shape

    def rows(j, ea, eb, nu):
        return (jnp.maximum(jnp.minimum(j, nu[0] - 1), 0), 0)

    def wa(j, ea, eb, nu):
        return (layer, ea[j], 0, 0)

    def wb(j, ea, eb, nu):
        return (layer, eb[j], 0, 0)

    return pl.pallas_call(
        _moe_kernel,
        out_shape=jax.ShapeDtypeStruct((n_rows, d), F32),
        grid_spec=pltpu.PrefetchScalarGridSpec(
            num_scalar_prefetch=3,
            grid=(n_rows // tile_rows,),
            in_specs=[pl.BlockSpec((tile_rows, w), rows),
                      pl.BlockSpec((1, 1, d, f), wa), pl.BlockSpec((1, 1, d, f), wa),
                      pl.BlockSpec((1, 1, f, d), wa),
                      pl.BlockSpec((1, 1, d, f), wb), pl.BlockSpec((1, 1, d, f), wb),
                      pl.BlockSpec((1, 1, f, d), wb)],
            out_specs=pl.BlockSpec((tile_rows, d), lambda j, ea, eb, nu: (j, 0)),
        ),
        compiler_params=_cparams(("arbitrary",)),
        name="moe_experts",
    )(ea, eb, n_used, h_sorted, wg, wu, wd, wg, wu, wd)


def _combine_kernel(pos_ref, y_ref, x_ref, mod_ref, fg_ref, o_ref, buf, sem, *, tm):
    i = pl.program_id(0)

    def issue(tile, slot):
        def eight(g, carry):
            for u in range(SUBLANES):
                src = pos_ref[tile * tm + g * SUBLANES + u]
                pltpu.make_async_copy(y_ref.at[pl.ds(src, 1), :], buf.at[slot, g, pl.ds(u, 1), :],
                                      sem.at[slot]).start()
            return carry
        lax.fori_loop(0, tm // SUBLANES, eight, 0)

    @pl.when(i == 0)
    def _():
        issue(0, 0)

    @pl.when(i + 1 < pl.num_programs(0))
    def _():
        issue(i + 1, (i + 1) % 2)

    slot = i % 2
    pltpu.make_async_copy(buf.at[slot], buf.at[slot], sem.at[slot]).wait()
    out = x_ref[...] + (1.0 + mod_ref[0, 5:6, :]) * buf[slot].reshape(x_ref.shape)
    ms = jnp.mean(out * out, axis=-1, keepdims=True)
    o_ref[...] = out * lax.rsqrt(ms + EPS) * fg_ref[...]


def _combine_call(pos, y_sorted, x, mod, final_g, tm, seq):
    t, d = x.shape
    per_seq = seq // tm
    return pl.pallas_call(
        functools.partial(_combine_kernel, tm=tm),
        out_shape=jax.ShapeDtypeStruct((t, d), F32),
        grid_spec=pltpu.PrefetchScalarGridSpec(
            num_scalar_prefetch=1,
            grid=(t // tm,),
            in_specs=[pl.BlockSpec(memory_space=pl.ANY),
                      pl.BlockSpec((tm, d), lambda i, pos: (i, 0)),
                      pl.BlockSpec((1, N_MOD, d), lambda i, pos: (i // per_seq, 0, 0)),
                      pl.BlockSpec((1, d), lambda i, pos: (0, 0))],
            out_specs=pl.BlockSpec((tm, d), lambda i, pos: (i, 0)),
            scratch_shapes=[pltpu.VMEM((2, tm // SUBLANES, SUBLANES, d), F32), pltpu.SemaphoreType.DMA((2,))],
        ),
        compiler_params=_cparams(("arbitrary",)),
        name="moe_combine",
    )(pos, y_sorted, x, mod, final_g)


def _combine_inproj_kernel(pos_ref, y_ref, x2_ref, modp_ref, g_ref, mod_ref, wq_ref, wh_ref, wf_ref,
                           x_ref, qkv_ref, hg_ref, pu_ref, ff_ref, buf, sem, *, tm, n_hg):
    i = pl.program_id(0)
    last = pl.num_programs(0) - 1
    groups = tm // SUBLANES

    def row_copy(tile, slot, g, u):
        src = pos_ref[tile * tm + g * SUBLANES + u]
        return pltpu.make_async_copy(y_ref.at[pl.ds(src, 1), :], buf.at[slot, g, pl.ds(u, 1), :], sem.at[slot])

    def wait_slot(slot):
        pltpu.make_async_copy(buf.at[slot], buf.at[slot], sem.at[slot]).wait()

    @pl.when(i == 0)
    def _():
        def eight(g, carry):
            for u in range(SUBLANES):
                row_copy(0, 0, g, u).start()
            return carry
        lax.fori_loop(0, groups, eight, 0)

    slot = i % 2
    wait_slot(slot)
    x = x2_ref[...] + (1.0 + modp_ref[0, 5:6, :]) * buf[slot].reshape(x2_ref.shape)
    x_ref[...] = x
    h = _norm_mod(x, g_ref[...], mod_ref[0, 0:1, :], mod_ref[0, 1:2, :]).astype(BF16)

    nxt = jnp.minimum(i + 1, last)
    for g in range(groups):
        for u in range(SUBLANES):
            row_copy(nxt, 1 - slot, g, u).start()

    n_qkv = qkv_ref.shape[-1]
    q_scale = jnp.where(_iota((1, n_qkv), 1) < n_qkv // 3, FOX_Q_SCALE, 1.0)
    qkv_ref[...] = (jnp.dot(h, wq_ref[...], preferred_element_type=F32) * q_scale).astype(BF16)
    hg_ref[...] = jnp.dot(h, wh_ref[:, 0:n_hg], preferred_element_type=F32)
    pu_ref[...] = jnp.dot(h, wh_ref[:, n_hg:], preferred_element_type=F32)
    ff_ref[...] = jnp.dot(h, wf_ref[...], preferred_element_type=F32)

    @pl.when(i == last)
    def _():
        wait_slot(1 - slot)


def _combine_inproj_call(pos, y_sorted, x2, mod_prev, g, mod, w_qkv, w_hp, w_ff, n_hg, tm, seq):
    t, d = x2.shape
    per_seq = seq // tm
    n_qkv, n_ff = w_qkv.shape[1], w_ff.shape[1]
    n_pu = w_hp.shape[1] - n_hg

    def tile(i, pos):
        return (i, 0)

    def fixed(i, pos):
        return (0, 0)

    def per_batch(i, pos):
        return (i // per_seq, 0, 0)

    return pl.pallas_call(
        functools.partial(_combine_inproj_kernel, tm=tm, n_hg=n_hg),
        out_shape=(
            jax.ShapeDtypeStruct((t, d), F32),
            jax.ShapeDtypeStruct((t, n_qkv), BF16),
            jax.ShapeDtypeStruct((t, n_hg), F32),
            jax.ShapeDtypeStruct((t, n_pu), F32),
            jax.ShapeDtypeStruct((t, n_ff), F32),
        ),
        grid_spec=pltpu.PrefetchScalarGridSpec(
            num_scalar_prefetch=1,
            grid=(t // tm,),
            in_specs=[pl.BlockSpec(memory_space=pl.ANY),
                      pl.BlockSpec((tm, d), tile),
                      pl.BlockSpec((1, N_MOD, d), per_batch),
                      pl.BlockSpec((1, d), fixed),
                      pl.BlockSpec((1, N_MOD, d), per_batch),
                      pl.BlockSpec((d, n_qkv), fixed),
                      pl.BlockSpec((d, n_hg + n_pu), fixed),
                      pl.BlockSpec((d, n_ff), fixed)],
            out_specs=(pl.BlockSpec((tm, d), tile), pl.BlockSpec((tm, n_qkv), tile), pl.BlockSpec((tm, n_hg), tile),
                       pl.BlockSpec((tm, n_pu), tile), pl.BlockSpec((tm, n_ff), tile)),
            scratch_shapes=[pltpu.VMEM((2, tm // SUBLANES, SUBLANES, d), F32), pltpu.SemaphoreType.DMA((2,))],
        ),
        compiler_params=_cparams(("arbitrary",)),
        name="combine_norm_inproj",
    )(pos, y_sorted, x2, mod_prev, g, mod, w_qkv, w_hp, w_ff)


def _pad_lanes(a, n=LANES):
    return jnp.pad(a, [(0, 0)] * (a.ndim - 1) + [(0, n - a.shape[-1])])


def kernel(x, c, w_ada, b_ada, norm1_g, w_in, fox_f_bias, fox_norm_g, hgrn_lb_logits, hgrn_norm_g, pool_w, pool_scale, w_out, norm2_g, router_group_w, router_group_b, router_expert_w, router_expert_b, expert_w_gate, expert_w_up, expert_w_down, final_norm_g):
    bsz, seq, d = x.shape
    depth = w_ada.shape[0]
    fox_heads = fox_f_bias.shape[1]
    fox_dim = fox_heads * HEAD_DIM
    hgrn_dim = hgrn_lb_logits.shape[1]
    pool_dim = pool_scale.shape[1]
    n_fox_pairs = fox_dim // LANES
    n_hgrn_pairs = hgrn_dim // LANES
    t = bsz * seq
    tm = min(512, seq)
    tq = min(512, seq)
    n_tiles = t // MOE_TILE_ROWS + N_CLASSES
    assert n_tiles <= LANES and t % MOE_TILE_ROWS == 0

    o_ff = 3 * fox_dim
    w_qkv = w_in[:, :, :o_ff].astype(BF16)
    w_hp = w_in[:, :, o_ff + fox_heads:].astype(BF16)
    w_ff = _pad_lanes(w_in[:, :, o_ff:o_ff + fox_heads]).astype(BF16)
    f_bias = _pad_lanes(fox_f_bias)
    groups = pool_w.shape[1]
    pool_bd = jnp.einsum('lgcd,gh->lgchd', pool_w, jnp.eye(groups, dtype=pool_w.dtype)).reshape(
        depth, pool_dim, pool_dim).astype(BF16)
    w_router = _pad_lanes(jnp.concatenate([router_group_w, router_expert_w], axis=-1))
    w_router_hi = w_router.astype(BF16)
    w_router = jnp.stack([w_router_hi, (w_router - w_router_hi.astype(F32)).astype(BF16)], axis=1)
    b_router = _pad_lanes(jnp.concatenate([router_group_b, router_expert_b], axis=-1))
    w_out_b = w_out.astype(BF16)
    wg_b = expert_w_gate.astype(BF16)
    wu_b = expert_w_up.astype(BF16)
    wd_b = expert_w_down.astype(BF16)
    pair_a = jnp.array(PAIR_SLOT_A, jnp.int32)
    pair_b = jnp.array(PAIR_SLOT_B, jnp.int32)
    final_g = final_norm_g.reshape(1, d)

    mod_all = _ada_call(c, w_ada, b_ada).reshape(depth, bsz, N_MOD, d)
    lower = _lb_call(hgrn_lb_logits)

    pending = None
    for l in range(depth):
        mod = mod_all[l]
        if pending is None:
            qkv, hg4, pu, ff = _inproj_call(x, norm1_g[l:l + 1], mod, w_qkv[l], w_hp[l], w_ff[l], 4 * hgrn_dim, tm)
        else:
            x, qkv, hg4, pu, ff = _combine_inproj_call(*pending, norm1_g[l:l + 1], mod, w_qkv[l], w_hp[l], w_ff[l],
                                                       4 * hgrn_dim, tm, seq)
            x, qkv, hg4, pu, ff = (a.reshape(bsz, seq, -1) for a in (x, qkv, hg4, pu, ff))
        fcol = _fbias_call(ff, f_bias[l:l + 1])
        o_fox = _fox_call(qkv, fcol, fox_norm_g[l].reshape(n_fox_pairs, 1, LANES), n_fox_pairs, tq)
        o_hgrn = _hgrn_call(hg4, lower[l].reshape(n_hgrn_pairs, 1, LANES),
                            hgrn_norm_g[l].reshape(n_hgrn_pairs, 1, LANES), n_hgrn_pairs)
        o_pool = _pool_call(pu, pool_bd[l], pool_scale[l:l + 1])
        x2, h2ext = _outproj_call(o_fox.reshape(t, -1), o_hgrn.reshape(t, -1), o_pool.reshape(t, -1),
                                  x.reshape(t, d), w_out_b[l], mod, norm2_g[l:l + 1],
                                  w_router[l], b_router[l:l + 1], tm, seq)
        pos8, meta = _sort_call(h2ext, d, MOE_TILE_ROWS)
        pos = pos8[0]
        n_used = meta[0, 1].astype(jnp.int32).reshape(1)
        tile = jnp.minimum(jnp.arange(n_tiles, dtype=jnp.int32), n_used[0] - 1)
        tile_cls = jnp.minimum(meta[:, 0].astype(jnp.int32)[tile], N_CLASSES - 1)
        e0 = (tile_cls // N_PAIRS) * EXPERTS_PER_GROUP
        ea = e0 + pair_a[tile_cls % N_PAIRS]
        eb = e0 + pair_b[tile_cls % N_PAIRS]
        h_sorted = _dispatch_call(pos, h2ext, n_tiles * MOE_TILE_ROWS, tm)
        y_sorted = _moe_call(ea, eb, n_used, h_sorted, wg_b, wu_b, wd_b, l, MOE_TILE_ROWS)
        pending = (pos, y_sorted, x2, mod)
    return _combine_call(*pending, final_g, tm, seq).reshape(bsz, seq, d)
```

```python
import functools

import jax
import jax.numpy as jnp
from jax import lax
from jax.experimental import pallas as pl
from jax.experimental.pallas import tpu as pltpu

F32 = jnp.float32
BF16 = jnp.bfloat16
HIGHEST = lax.Precision.HIGHEST

HEAD_DIM = 64
LANES = 128
SUBLANES = 8
EPS = 1e-6
N_MOD = 6
POOL_WINDOWS = (2, 4, 8, 16)
N_GROUPS = 4
EXPERTS_PER_GROUP = 4
N_EXPERTS = N_GROUPS * EXPERTS_PER_GROUP
ROUTER_EXPERT_LANE0 = N_GROUPS
N_PAIRS = 6
N_CLASSES = N_GROUPS * N_PAIRS
PAIR_SLOT_A = (0, 0, 0, 1, 1, 3)
PAIR_SLOT_B = (1, 2, 3, 3, 2, 2)
ROUTE_CLASS_LANE0 = 32
MOE_TILE_ROWS = 256
HGRN_CHUNK = 128
HGRN_LEVELS = (64, 32, 16, 8, 4, 2, 1)
HGRN_GROUP = 4
VMEM_LIMIT = 48 * 1024 * 1024
LOG2_E = 1.4426950408889634
FOX_Q_SCALE = HEAD_DIM ** -0.5 * LOG2_E


def _cparams(sem):
    return pltpu.CompilerParams(dimension_semantics=sem, vmem_limit_bytes=VMEM_LIMIT)


def _silu(x):
    return x * jax.nn.sigmoid(x)


def _iota(shape, dim):
    return lax.broadcasted_iota(jnp.int32, shape, dim)


def _ada_kernel(c_ref, w_ref, b_ref, o_ref):
    sc = _silu(c_ref[...]).astype(BF16)
    o_ref[0] = jnp.dot(sc, w_ref[0].astype(BF16), preferred_element_type=F32) + b_ref[0]


def _ada_call(c, w_ada, b_ada):
    depth, d, n = w_ada.shape
    bsz = c.shape[0]
    tn = 1536
    return pl.pallas_call(
        _ada_kernel,
        out_shape=jax.ShapeDtypeStruct((depth, bsz, n), F32),
        grid=(depth, n // tn),
        in_specs=[
            pl.BlockSpec((bsz, d), lambda l, j: (0, 0)),
            pl.BlockSpec((1, d, tn), lambda l, j: (l, 0, j)),
            pl.BlockSpec((1, 1, tn), lambda l, j: (l, 0, j)),
        ],
        out_specs=pl.BlockSpec((1, bsz, tn), lambda l, j: (l, 0, j)),
        compiler_params=_cparams(("arbitrary", "arbitrary")),
        name="ada_mod",
    )(c, w_ada, b_ada.reshape(depth, 1, n))


def _lb_kernel(x_ref, o_ref):
    x = x_ref[...]
    depth = x.shape[0]
    e = jnp.exp(x - jnp.max(x, axis=0, keepdims=True))
    p = e / jnp.sum(e, axis=0, keepdims=True)
    acc = jnp.zeros_like(p[0:1])
    o_ref[0:1, :] = acc
    for l in range(1, depth):
        acc = acc + p[l:l + 1]
        o_ref[l:l + 1, :] = acc


def _lb_call(lb_logits):
    return pl.pallas_call(
        _lb_kernel,
        out_shape=jax.ShapeDtypeStruct(lb_logits.shape, F32),
        name="hgrn_lower_bounds",
    )(lb_logits)


def _norm_mod(x, g, shift, scale):
    ms = jnp.mean(x * x, axis=-1, keepdims=True)
    return (x * lax.rsqrt(ms + EPS) * g) * (1.0 + scale) + shift


def _inproj_kernel(x_ref, g_ref, mod_ref, wq_ref, wh_ref, wf_ref, qkv_ref, hg_ref, pu_ref, ff_ref, *, n_hg):
    h = _norm_mod(x_ref[0], g_ref[...], mod_ref[0, 0:1, :], mod_ref[0, 1:2, :]).astype(BF16)
    n_qkv = qkv_ref.shape[-1]
    q_scale = jnp.where(_iota((1, n_qkv), 1) < n_qkv // 3, FOX_Q_SCALE, 1.0)
    qkv_ref[0] = (jnp.dot(h, wq_ref[...], preferred_element_type=F32) * q_scale).astype(BF16)
    hg_ref[0] = jnp.dot(h, wh_ref[:, 0:n_hg], preferred_element_type=F32)
    pu_ref[0] = jnp.dot(h, wh_ref[:, n_hg:], preferred_element_type=F32)
    ff_ref[0] = jnp.dot(h, wf_ref[...], preferred_element_type=F32)


def _inproj_call(x, g, mod, w_qkv, w_hp, w_ff, n_hg, tm):
    bsz, seq, d = x.shape
    n_qkv, n_ff = w_qkv.shape[1], w_ff.shape[1]
    n_pu = w_hp.shape[1] - n_hg
    return pl.pallas_call(
        functools.partial(_inproj_kernel, n_hg=n_hg),
        out_shape=(
            jax.ShapeDtypeStruct((bsz, seq, n_qkv), BF16),
            jax.ShapeDtypeStruct((bsz, seq, n_hg), F32),
            jax.ShapeDtypeStruct((bsz, seq, n_pu), F32),
            jax.ShapeDtypeStruct((bsz, seq, n_ff), F32),
        ),
        grid=(bsz, seq // tm),
        in_specs=[
            pl.BlockSpec((1, tm, d), lambda b, i: (b, i, 0)),
            pl.BlockSpec((1, d), lambda b, i: (0, 0)),
            pl.BlockSpec((1, N_MOD, d), lambda b, i: (b, 0, 0)),
            pl.BlockSpec((d, n_qkv), lambda b, i: (0, 0)),
            pl.BlockSpec((d, n_hg + n_pu), lambda b, i: (0, 0)),
            pl.BlockSpec((d, n_ff), lambda b, i: (0, 0)),
        ],
        out_specs=(
            pl.BlockSpec((1, tm, n_qkv), lambda b, i: (b, i, 0)),
            pl.BlockSpec((1, tm, n_hg), lambda b, i: (b, i, 0)),
            pl.BlockSpec((1, tm, n_pu), lambda b, i: (b, i, 0)),
            pl.BlockSpec((1, tm, n_ff), lambda b, i: (b, i, 0)),
        ),
        compiler_params=_cparams(("arbitrary", "arbitrary")),
        name="norm_inproj",
    )(x, g, mod, w_qkv, w_hp, w_ff)


def _fbias_kernel(ff_ref, bias_ref, fcol_ref, *, cb):
    seq = ff_ref.shape[1]
    tri = (_iota((cb, cb), 0) >= _iota((cb, cb), 1)).astype(F32)
    carry = jnp.zeros((1, LANES), F32)
    for blk in range(seq // cb):
        x = ff_ref[0, blk * cb:(blk + 1) * cb, :] + bias_ref[...]
        log_f = jnp.minimum(x, 0.0) - jnp.log1p(jnp.exp(-jnp.abs(x)))
        cs = jnp.dot(tri, log_f, precision=HIGHEST, preferred_element_type=F32) + carry
        fcol_ref[0, blk * cb:(blk + 1) * cb, :] = cs * LOG2_E
        carry = cs[cb - 1:cb, :]


def _fbias_call(ff, bias):
    bsz, seq, _ = ff.shape
    cb = min(256, seq)
    return pl.pallas_call(
        functools.partial(_fbias_kernel, cb=cb),
        out_shape=jax.ShapeDtypeStruct((bsz, seq, LANES), F32),
        grid=(bsz,),
        in_specs=[pl.BlockSpec((1, seq, LANES), lambda b: (b, 0, 0)),
                  pl.BlockSpec((1, LANES), lambda b: (0, 0))],
        out_specs=pl.BlockSpec((1, seq, LANES), lambda b: (b, 0, 0)),
        compiler_params=_cparams(("arbitrary",)),
        name="fox_forget_bias",
    )(ff, bias)


def _split3(f):
    hi = f.astype(BF16).astype(F32)
    rest = f - hi
    mid = rest.astype(BF16).astype(F32)
    return hi, mid, rest - mid


def _fox_kernel(q_ref, k_ref, v_ref, fc_ref, gn_ref, o_ref, k0_sc, k1_sc, v0_sc, v1_sc, *, tq):
    p = pl.program_id(1)
    seq = k_ref.shape[1]
    nb = seq // tq
    lane = _iota((1, LANES), 1)
    h0 = lane < HEAD_DIM
    data = (h0, jnp.logical_not(h0))
    base = (HEAD_DIM, 0)
    k_sc = (k0_sc, k1_sc)
    v_sc = (v0_sc, v1_sc)
    nt = (((1,), (1,)), ((), ()))
    heads = (0, 1)

    def head_f(fc, which):
        return jnp.sum(jnp.where(lane == 2 * p + which, fc, 0.0), axis=-1, keepdims=True)

    def with_bias_lanes(x, which, first3, last3):
        out = jnp.where(data[which], x, 0.0)
        for n in range(3):
            out = jnp.where(lane == base[which] + n, first3[n], out)
            out = jnp.where(lane == base[which] + 3 + n, last3[n], out)
        return out.astype(BF16)

    qs = {}
    for blk in range(nb):
        rows = slice(blk * tq, (blk + 1) * tq)
        fc = fc_ref[0, rows, :]
        qf = q_ref[0, rows, :].astype(F32)
        kf = k_ref[0, rows, :].astype(F32)
        vf = v_ref[0, rows, :].astype(F32)
        for which in heads:
            hi, mid, lo = _split3(head_f(fc, which))
            qs[blk, which] = with_bias_lanes(qf, which, (hi, mid, lo), (1.0, 1.0, 1.0))
            k_sc[which][rows, :] = with_bias_lanes(kf, which, (1.0, 1.0, 1.0), (-hi, -mid, -lo))
            v_sc[which][rows, :] = jnp.where(data[which], vf,
                                             jnp.where(lane == base[which], 1.0, 0.0)).astype(BF16)

    causal = _iota((tq, tq), 0) >= _iota((tq, tq), 1)
    m, acc = {}, {}
    for r in range(nb):
        keys = slice(r * tq, (r + 1) * tq)
        tasks = [(i, which) for i in range(r, nb) for which in heads]
        s = {}
        for i, which in tasks:
            s[i, which] = lax.dot_general(qs[i, which], k_sc[which][keys, :], nt, preferred_element_type=F32)
            if i == r:
                s[i, which] = jnp.where(causal, s[i, which], -jnp.inf)
        n = {}
        for t in tasks:
            n[t] = jnp.max(s[t], axis=-1, keepdims=True)
            if r > 0:
                n[t] = jnp.maximum(m[t], n[t])
        pv = {}
        for i, which in tasks:
            pr = jnp.exp2(s[i, which] - n[i, which]).astype(BF16)
            pv[i, which] = jnp.dot(pr, v_sc[which][keys, :], preferred_element_type=F32)
        for t in tasks:
            acc[t] = pv[t] if r == 0 else jnp.exp2(m[t] - n[t]) * acc[t] + pv[t]
            m[t] = n[t]

        l0 = jnp.sum(jnp.where(lane == base[0], acc[r, 0], 0.0), axis=-1, keepdims=True)
        l1 = jnp.sum(jnp.where(lane == base[1], acc[r, 1], 0.0), axis=-1, keepdims=True)
        o = jnp.where(h0, acc[r, 0] / l0, acc[r, 1] / l1)
        o2 = o * o
        ms0 = jnp.sum(jnp.where(h0, o2, 0.0), axis=-1, keepdims=True)
        ms1 = jnp.sum(jnp.where(h0, 0.0, o2), axis=-1, keepdims=True)
        ms = jnp.where(h0, ms0, ms1) * (1.0 / HEAD_DIM)
        o_ref[0, keys, :] = (o * lax.rsqrt(ms + EPS) * gn_ref[0]).astype(BF16)


def _fox_call(qkv, fcol, gn, n_pairs, tq):
    bsz, seq, _ = qkv.shape
    return pl.pallas_call(
        functools.partial(_fox_kernel, tq=tq),
        out_shape=jax.ShapeDtypeStruct((bsz, seq, n_pairs * LANES), BF16),
        grid=(bsz, n_pairs),
        in_specs=[
            pl.BlockSpec((1, seq, LANES), lambda b, p: (b, 0, p)),
            pl.BlockSpec((1, seq, LANES), lambda b, p: (b, 0, n_pairs + p)),
            pl.BlockSpec((1, seq, LANES), lambda b, p: (b, 0, 2 * n_pairs + p)),
            pl.BlockSpec((1, seq, LANES), lambda b, p: (b, 0, 0)),
            pl.BlockSpec((1, 1, LANES), lambda b, p: (p, 0, 0)),
        ],
        out_specs=pl.BlockSpec((1, seq, LANES), lambda b, p: (b, 0, p)),
        scratch_shapes=[pltpu.VMEM((seq, LANES), BF16)] * 4,
        compiler_params=_cparams(("arbitrary", "arbitrary")),
        name="fox_attention",
    )(qkv, qkv, qkv, fcol, gn)


def _hgrn_kernel(hq_ref, hf_ref, hi_ref, hg_ref, lb_ref, gn_ref, o_ref):
    ch = HGRN_CHUNK
    seq = hq_ref.shape[1]
    lane = _iota((1, LANES), 1)
    h0 = lane < HEAD_DIM
    r = _iota((ch, ch), 0)
    c = _iota((ch, ch), 1)
    tri = (r >= c).astype(BF16)
    same_head = (r < HEAD_DIM) == (c < HEAD_DIM)
    seg = same_head.astype(BF16)
    r2 = _iota((ch, 2 * ch), 0)
    c2 = _iota((ch, 2 * ch), 1) % ch
    level_masks = [(r2 // (2 * m) == c2 // (2 * m)) & (r2 % (2 * m) >= m) & (c2 % (2 * m) < m)
                   for m in HGRN_LEVELS]
    small_levels = [m for m in HGRN_LEVELS if m < SUBLANES]
    pick = jnp.concatenate([(c == (r // (2 * m)) * (2 * m) + (m - 1)) for m in small_levels], axis=0).astype(BF16)
    lb = lb_ref[0]
    nt = (((1,), (1,)), ((), ()))
    zero_b = jnp.zeros((ch, LANES), BF16)

    def both_heads(x):
        return jnp.concatenate([jnp.where(h0, x, zero_b), jnp.where(h0, zero_b, x)], axis=0)

    def sum3(x):
        return x[:, 0:LANES] + x[:, LANES:2 * LANES] + x[:, 2 * LANES:]

    def parts3(x):
        return jnp.concatenate([part.astype(BF16) for part in _split3(x)], axis=1)

    def group(gi, state_t):
        n = HGRN_GROUP
        sls = [pl.ds(pl.multiple_of((gi * n + i) * ch, ch), ch) for i in range(n)]
        k, q, v, vb, b, picked = [], [], [], [], [], []
        for sl in sls:
            f = lb + (1.0 - lb) * jax.nn.sigmoid(hf_ref[0, sl, :])
            k.append(1.0 - f)
            q.append(_silu(hq_ref[0, sl, :]))
            v.append(hi_ref[0, sl, :])
            vb.append(v[-1].astype(BF16))
            b.append(sum3(jnp.dot(tri, parts3(jnp.log(f) * LOG2_E), preferred_element_type=F32)))
        for i in range(n):
            picked.append(jnp.dot(pick, parts3(b[i]), preferred_element_type=F32))

        sc = [jnp.zeros((ch, 2 * ch), F32) for _ in range(n)]
        for li, m in enumerate(HGRN_LEVELS):
            for i in range(n):
                if m >= SUBLANES:
                    ref_l = jnp.broadcast_to(b[i].reshape(ch // (2 * m), 2 * m, LANES)[:, m - 1:m, :],
                                             (ch // (2 * m), 2 * m, LANES)).reshape(ch, LANES)
                else:
                    at = small_levels.index(m) * ch
                    ref_l = sum3(picked[i][at:at + ch])
                e = jnp.exp2(-jnp.abs(b[i] - ref_l))
                s = lax.dot_general((q[i] * e).astype(BF16), both_heads((k[i] * e).astype(BF16)), nt,
                                    preferred_element_type=F32)
                sc[i] = jnp.where(level_masks[li], s, sc[i])

        o, upd_t, dec = [], [], []
        for i in range(n):
            oi = jnp.dot(sc[i].astype(BF16), both_heads(vb[i]), preferred_element_type=F32)
            o.append(oi + jnp.dot((q[i] * k[i]).astype(BF16), seg, preferred_element_type=F32) * v[i])
            b_last = b[i][ch - 1:ch, :]
            k_dec = (k[i] * jnp.exp2(b_last - b[i])).astype(BF16)
            upd = jnp.dot(v[i].T.astype(BF16), k_dec, preferred_element_type=F32)
            upd_t.append(jnp.where(same_head, upd, 0.0))
            dec.append(jnp.exp2(b_last))

        states = [state_t]
        for i in range(n):
            states.append(dec[i] * states[i] + upd_t[i])

        for i, sl in enumerate(sls):
            oi = o[i] + lax.dot_general((q[i] * jnp.exp2(b[i])).astype(BF16), states[i].astype(BF16), nt,
                                        preferred_element_type=F32)
            o2 = oi * oi
            ms0 = jnp.sum(jnp.where(h0, o2, 0.0), axis=-1, keepdims=True)
            ms1 = jnp.sum(jnp.where(h0, 0.0, o2), axis=-1, keepdims=True)
            ms = jnp.where(h0, ms0, ms1) * (1.0 / HEAD_DIM)
            out = oi * lax.rsqrt(ms + EPS) * gn_ref[0] * _silu(hg_ref[0, sl, :])
            o_ref[0, sl, :] = out.astype(BF16)
        return states[n]

    lax.fori_loop(0, seq // (HGRN_GROUP * ch), group, jnp.zeros((LANES, LANES), F32))


def _hgrn_call(hg4, lb, gn, n_pairs):
    bsz, seq, _ = hg4.shape

    def spec(off):
        return pl.BlockSpec((1, seq, LANES), lambda b, p: (b, 0, off + p))

    return pl.pallas_call(
        _hgrn_kernel,
        out_shape=jax.ShapeDtypeStruct((bsz, seq, n_pairs * LANES), BF16),
        grid=(bsz, n_pairs),
        in_specs=[spec(0), spec(n_pairs), spec(2 * n_pairs), spec(3 * n_pairs),
                  pl.BlockSpec((1, 1, LANES), lambda b, p: (p, 0, 0)),
                  pl.BlockSpec((1, 1, LANES), lambda b, p: (p, 0, 0))],
        out_specs=pl.BlockSpec((1, seq, LANES), lambda b, p: (b, 0, p)),
        compiler_params=_cparams(("arbitrary", "arbitrary")),
        name="hgrn2",
    )(hg4, hg4, hg4, hg4, lb, gn)


def _pool_kernel(u_ref, w_ref, s_ref, o_ref):
    u = u_ref[0]
    seq, n = u.shape
    t = _iota((seq, 1), 0)
    lane = _iota((1, n), 1)

    def shifted(x, k):
        return jnp.where(t >= k, pltpu.roll(x, k, axis=0), 0.0)

    sums = []
    s = u
    for w in POOL_WINDOWS:
        s = s + shifted(s, w // 2)
        sums.append(s)
    pos1 = (t + 1).astype(F32)
    group = len(POOL_WINDOWS) - 1
    mean = sums[group] / jnp.minimum(pos1, float(POOL_WINDOWS[group]))
    group_dim = n // len(POOL_WINDOWS)
    for gi in range(group - 1, -1, -1):
        mean = jnp.where(lane < (gi + 1) * group_dim,
                         sums[gi] / jnp.minimum(pos1, float(POOL_WINDOWS[gi])), mean)
    pooled = (mean - u).astype(BF16)
    o_ref[0] = (jnp.dot(pooled, w_ref[...], preferred_element_type=F32) * s_ref[...]).astype(BF16)


def _pool_call(pu, w_bd, scale):
    bsz, seq, n = pu.shape
    return pl.pallas_call(
        _pool_kernel,
        out_shape=jax.ShapeDtypeStruct((bsz, seq, n), BF16),
        grid=(bsz,),
        in_specs=[pl.BlockSpec((1, seq, n), lambda b: (b, 0, 0)),
                  pl.BlockSpec((n, n), lambda b: (0, 0)),
                  pl.BlockSpec((1, n), lambda b: (0, 0))],
        out_specs=pl.BlockSpec((1, seq, n), lambda b: (b, 0, 0)),
        compiler_params=_cparams(("arbitrary",)),
        name="multiscale_pool",
    )(pu, w_bd, scale)


def _route(logits):
    lane = _iota(logits.shape, 1)
    lane_f = lane.astype(F32)
    big = float(LANES)
    is_g = lane < N_GROUPS
    gl = jnp.where(is_g, logits, -jnp.inf)
    gmax = jnp.max(gl, axis=-1, keepdims=True)
    gsum = jnp.sum(jnp.where(is_g, jnp.exp(gl - gmax), 0.0), axis=-1, keepdims=True)
    p_group = 1.0 / gsum
    g_sel = jnp.min(jnp.where(gl == gmax, lane_f, big), axis=-1, keepdims=True)
    lo = ROUTER_EXPERT_LANE0 + EXPERTS_PER_GROUP * g_sel
    in_group = (lane_f >= lo) & (lane_f < lo + EXPERTS_PER_GROUP)
    el = jnp.where(in_group, logits, -jnp.inf)
    m1 = jnp.max(el, axis=-1, keepdims=True)
    i1 = jnp.min(jnp.where(el == m1, lane_f, big), axis=-1, keepdims=True)
    el2 = jnp.where(lane_f == i1, -jnp.inf, el)
    m2 = jnp.max(el2, axis=-1, keepdims=True)
    i2 = jnp.min(jnp.where(el2 == m2, lane_f, big), axis=-1, keepdims=True)
    e2 = jnp.exp(m2 - m1)
    w1 = p_group / (1.0 + e2)
    w2 = p_group * e2 / (1.0 + e2)
    a = jnp.minimum(i1, i2) - lo
    b = jnp.maximum(i1, i2) - lo
    pair = jnp.where(a == 0.0, b - 1.0, jnp.where(a == 1.0, 6.0 - b, float(N_PAIRS - 1)))
    top1_in_a = (i1 < i2) != (pair == float(N_PAIRS - 1))
    cls_lane = ROUTE_CLASS_LANE0 + g_sel * N_PAIRS + pair
    return (jnp.where(lane == 0, jnp.where(top1_in_a, w1, w2), 0.0)
            + jnp.where(lane == 1, jnp.where(top1_in_a, w2, w1), 0.0)
            + jnp.where(lane_f == cls_lane, 1.0, 0.0))


def _outproj_kernel(of_ref, oh_ref, op_ref, x_ref, w_ref, mod_ref, g_ref, wr_ref, br_ref,
                    xn_ref, h2_ref, *, n_fox, n_hgrn):
    d = x_ref.shape[-1]
    y = jnp.dot(of_ref[...], w_ref[0:n_fox, :], preferred_element_type=F32)
    y = y + jnp.dot(oh_ref[...], w_ref[n_fox:n_fox + n_hgrn, :], preferred_element_type=F32)
    y = y + jnp.dot(op_ref[...], w_ref[n_fox + n_hgrn:, :], preferred_element_type=F32)
    xn = x_ref[...] + (1.0 + mod_ref[0, 2:3, :]) * y
    xn_ref[...] = xn
    h2 = _norm_mod(xn, g_ref[...], mod_ref[0, 3:4, :], mod_ref[0, 4:5, :])
    h2_ref[:, 0:d] = h2
    h_hi = h2.astype(BF16)
    h_lo = (h2 - h_hi.astype(F32)).astype(BF16)
    logits = (jnp.dot(h_hi, wr_ref[0], preferred_element_type=F32)
              + jnp.dot(h_lo, wr_ref[0], preferred_element_type=F32)
              + jnp.dot(h_hi, wr_ref[1], preferred_element_type=F32)) + br_ref[...]
    h2_ref[:, d:] = _route(logits)


def _outproj_call(o_fox, o_hgrn, o_pool, x, w_out, mod, g2, w_router, b_router, tm, seq):
    t, d = x.shape
    n_fox, n_hgrn, n_pool = o_fox.shape[-1], o_hgrn.shape[-1], o_pool.shape[-1]
    per_seq = seq // tm
    return pl.pallas_call(
        functools.partial(_outproj_kernel, n_fox=n_fox, n_hgrn=n_hgrn),
        out_shape=(jax.ShapeDtypeStruct((t, d), F32),
                   jax.ShapeDtypeStruct((t, d + LANES), F32)),
        grid=(t // tm,),
        in_specs=[
            pl.BlockSpec((tm, n_fox), lambda i: (i, 0)),
            pl.BlockSpec((tm, n_hgrn), lambda i: (i, 0)),
            pl.BlockSpec((tm, n_pool), lambda i: (i, 0)),
            pl.BlockSpec((tm, d), lambda i: (i, 0)),
            pl.BlockSpec((d, d), lambda i: (0, 0)),
            pl.BlockSpec((1, N_MOD, d), lambda i: (i // per_seq, 0, 0)),
            pl.BlockSpec((1, d), lambda i: (0, 0)),
            pl.BlockSpec((2, d, LANES), lambda i: (0, 0, 0)),
            pl.BlockSpec((1, LANES), lambda i: (0, 0)),
        ],
        out_specs=(pl.BlockSpec((tm, d), lambda i: (i, 0)),
                   pl.BlockSpec((tm, d + LANES), lambda i: (i, 0))),
        compiler_params=_cparams(("arbitrary",)),
        name="outproj_router",
    )(o_fox, o_hgrn, o_pool, x, w_out, mod, g2, w_router, b_router)


def _sort_kernel(route_ref, pos_ref, meta_ref, *, tile_rows, cb):
    t = route_ref.shape[0]
    lane = _iota((1, LANES), 1)
    is_cls = (lane >= ROUTE_CLASS_LANE0) & (lane < ROUTE_CLASS_LANE0 + N_CLASSES)
    nblk = t // cb
    zero_row = jnp.zeros((1, LANES), F32)

    def onehot(i):
        return jnp.where(is_cls, route_ref[pl.ds(pl.multiple_of(i * cb, cb), cb), :], 0.0)

    counts = lax.fori_loop(0, nblk, lambda i, acc: acc + jnp.sum(onehot(i), axis=0, keepdims=True), zero_row)
    padded = jnp.floor((counts + float(tile_rows - 1)) * (1.0 / tile_rows)) * float(tile_rows)
    before = (_iota((LANES, LANES), 0) < _iota((LANES, LANES), 1)).astype(F32)
    offs = jnp.dot(jnp.broadcast_to(padded, (8, LANES)), before, precision=HIGHEST,
                   preferred_element_type=F32)[0:1]
    ends = offs + padded
    strict = (_iota((cb, cb), 0) > _iota((cb, cb), 1)).astype(BF16)
    ones8 = jnp.ones((8, LANES), F32)
    nt = (((1,), (1,)), ((), ()))

    def place(i, seen):
        oh = onehot(i)
        rank = jnp.dot(strict, oh.astype(BF16), preferred_element_type=F32) + seen
        dest = oh * (rank + offs)
        rows = lax.dot_general(ones8, dest, nt, precision=HIGHEST, preferred_element_type=F32)
        pos_ref[:, pl.ds(pl.multiple_of(i * cb, cb), cb)] = rows.astype(jnp.int32)
        return seen + jnp.sum(oh, axis=0, keepdims=True)

    lax.fori_loop(0, nblk, place, zero_row)
    tile_start = _iota((LANES, LANES), 0).astype(F32) * float(tile_rows)
    tile_cls = jnp.sum(jnp.where(is_cls & (ends <= tile_start), 1.0, 0.0), axis=-1, keepdims=True)
    n_used = jnp.sum(jnp.where(lane == ROUTE_CLASS_LANE0 + N_CLASSES - 1, ends, 0.0), axis=-1,
                     keepdims=True) * (1.0 / tile_rows)
    meta_ref[...] = jnp.where(lane == 0, tile_cls, jnp.where(lane == 1, n_used, 0.0))


def _sort_call(h2ext, d, tile_rows):
    t = h2ext.shape[0]
    cb = min(512, t)
    return pl.pallas_call(
        functools.partial(_sort_kernel, tile_rows=tile_rows, cb=cb),
        out_shape=(jax.ShapeDtypeStruct((8, t), jnp.int32),
                   jax.ShapeDtypeStruct((LANES, LANES), F32)),
        grid=(1,),
        in_specs=[pl.BlockSpec((t, LANES), lambda i: (0, d // LANES))],
        out_specs=(pl.BlockSpec((8, t), lambda i: (0, 0)),
                   pl.BlockSpec((LANES, LANES), lambda i: (0, 0))),
        compiler_params=_cparams(("arbitrary",)),
        name="route_sort",
    )(h2ext)


def _dispatch_kernel(pos_ref, h_ref, init_ref, wg_ref, wu_ref, wd_ref, out_ref, wgb_ref, wub_ref, wdb_ref, sem,
                     *, tm):
    del init_ref
    base = pl.program_id(0) * tm
    for g in range(tm // SUBLANES):
        for u in range(SUBLANES):
            dst = pos_ref[base + g * SUBLANES + u]
            pltpu.make_async_copy(h_ref.at[g, pl.ds(u, 1), :], out_ref.at[pl.ds(dst, 1), :], sem).start()
    wgb_ref[...] = wg_ref[0].astype(BF16)
    wub_ref[...] = wu_ref[0].astype(BF16)
    wdb_ref[...] = wd_ref[0].astype(BF16)
    pltpu.make_async_copy(h_ref, h_ref, sem).wait()


def _dispatch_call(pos, h2ext, n_rows, wg, wu, wd, layer, tm):
    t, w = h2ext.shape
    _, n_e, d, f = wg.shape
    steps = t // tm
    parts = steps // n_e
    assert steps == parts * n_e and d % (parts * SUBLANES) == 0 and f % (parts * SUBLANES) == 0

    def w_in(i, pos):
        return (layer, i // parts, i % parts, 0)

    def w_out(i, pos):
        return (i // parts, i % parts, 0)

    return pl.pallas_call(
        functools.partial(_dispatch_kernel, tm=tm),
        out_shape=(jax.ShapeDtypeStruct((n_rows, w), F32),
                   jax.ShapeDtypeStruct((n_e, d, f), BF16),
                   jax.ShapeDtypeStruct((n_e, d, f), BF16),
                   jax.ShapeDtypeStruct((n_e, f, d), BF16)),
        grid_spec=pltpu.PrefetchScalarGridSpec(
            num_scalar_prefetch=1,
            grid=(steps,),
            in_specs=[pl.BlockSpec((tm // SUBLANES, SUBLANES, w), lambda i, pos: (i, 0, 0)),
                      pl.BlockSpec(memory_space=pl.ANY),
                      pl.BlockSpec((1, 1, d // parts, f), w_in),
                      pl.BlockSpec((1, 1, d // parts, f), w_in),
                      pl.BlockSpec((1, 1, f // parts, d), w_in)],
            out_specs=(pl.BlockSpec(memory_space=pl.ANY),
                       pl.BlockSpec((1, d // parts, f), w_out),
                       pl.BlockSpec((1, d // parts, f), w_out),
                       pl.BlockSpec((1, f // parts, d), w_out)),
            scratch_shapes=[pltpu.SemaphoreType.DMA],
        ),
        input_output_aliases={2: 0},
        compiler_params=_cparams(("arbitrary",)),
        name="moe_dispatch",
    )(pos, h2ext.reshape(t // SUBLANES, SUBLANES, w), jnp.zeros((n_rows, w), F32), wg, wu, wd)


def _moe_kernel(ea_ref, eb_ref, nu_ref, h_ref, wga_ref, wua_ref, wda_ref, wgb_ref, wub_ref, wdb_ref, o_ref):
    del ea_ref, eb_ref
    d = o_ref.shape[-1]

    @pl.when(pl.program_id(0) < nu_ref[0])
    def _():
        h = h_ref[:, 0:d].astype(BF16)
        tail = h_ref[:, d:]
        lane = _iota((1, LANES), 1)
        w_a = jnp.sum(jnp.where(lane == 0, tail, 0.0), axis=-1, keepdims=True)
        w_b = jnp.sum(jnp.where(lane == 1, tail, 0.0), axis=-1, keepdims=True)

        def expert(wg_ref, wu_ref, wd_ref):
            hidden = _silu(jnp.dot(h, wg_ref[0], preferred_element_type=F32)) * jnp.dot(
                h, wu_ref[0], preferred_element_type=F32)
            return jnp.dot(hidden.astype(BF16), wd_ref[0], preferred_element_type=F32)

        o_ref[...] = w_a * expert(wga_ref, wua_ref, wda_ref) + w_b * expert(wgb_ref, wub_ref, wdb_ref)

    @pl.when(pl.program_id(0) >= nu_ref[0])
    def _():
        o_ref[...] = jnp.zeros_like(o_ref)


def _moe_call(ea, eb, n_used, h_sorted, wg, wu, wd, tile_rows):
    n_rows, w = h_sorted.shape
    _, d, f = wg.shape

    def rows(j, ea, eb, nu):
        return (jnp.maximum(jnp.minimum(j, nu[0] - 1), 0), 0)

    def wa(j, ea, eb, nu):
        return (ea[j], 0, 0)

    def wb(j, ea, eb, nu):
        return (eb[j], 0, 0)

    return pl.pallas_call(
        _moe_kernel,
        out_shape=jax.ShapeDtypeStruct((n_rows, d), F32),
        grid_spec=pltpu.PrefetchScalarGridSpec(
            num_scalar_prefetch=3,
            grid=(n_rows // tile_rows,),
            in_specs=[pl.BlockSpec((tile_rows, w), rows),
                      pl.BlockSpec((1, d, f), wa), pl.BlockSpec((1, d, f), wa), pl.BlockSpec((1, f, d), wa),
                      pl.BlockSpec((1, d, f), wb), pl.BlockSpec((1, d, f), wb), pl.BlockSpec((1, f, d), wb)],
            out_specs=pl.BlockSpec((tile_rows, d), lambda j, ea, eb, nu: (j, 0)),
        ),
        compiler_params=_cparams(("arbitrary",)),
        name="moe_experts",
    )(ea, eb, n_used, h_sorted, wg, wu, wd, wg, wu, wd)


def _combine_kernel(pos_ref, y_ref, x_ref, mod_ref, fg_ref, o_ref, buf, sem, *, tm):
    i = pl.program_id(0)

    def issue(tile, slot):
        def eight(g, carry):
            for u in range(SUBLANES):
                src = pos_ref[tile * tm + g * SUBLANES + u]
                pltpu.make_async_copy(y_ref.at[pl.ds(src, 1), :], buf.at[slot, g, pl.ds(u, 1), :],
                                      sem.at[slot]).start()
            return carry
        lax.fori_loop(0, tm // SUBLANES, eight, 0)

    @pl.when(i == 0)
    def _():
        issue(0, 0)

    @pl.when(i + 1 < pl.num_programs(0))
    def _():
        issue(i + 1, (i + 1) % 2)

    slot = i % 2
    pltpu.make_async_copy(buf.at[slot], buf.at[slot], sem.at[slot]).wait()
    out = x_ref[...] + (1.0 + mod_ref[0, 5:6, :]) * buf[slot].reshape(x_ref.shape)
    ms = jnp.mean(out * out, axis=-1, keepdims=True)
    o_ref[...] = out * lax.rsqrt(ms + EPS) * fg_ref[...]


def _combine_call(pos, y_sorted, x, mod, final_g, tm, seq):
    t, d = x.shape
    per_seq = seq // tm
    return pl.pallas_call(
        functools.partial(_combine_kernel, tm=tm),
        out_shape=jax.ShapeDtypeStruct((t, d), F32),
        grid_spec=pltpu.PrefetchScalarGridSpec(
            num_scalar_prefetch=1,
            grid=(t // tm,),
            in_specs=[pl.BlockSpec(memory_space=pl.ANY),
                      pl.BlockSpec((tm, d), lambda i, pos: (i, 0)),
                      pl.BlockSpec((1, N_MOD, d), lambda i, pos: (i // per_seq, 0, 0)),
                      pl.BlockSpec((1, d), lambda i, pos: (0, 0))],
            out_specs=pl.BlockSpec((tm, d), lambda i, pos: (i, 0)),
            scratch_shapes=[pltpu.VMEM((2, tm // SUBLANES, SUBLANES, d), F32), pltpu.SemaphoreType.DMA((2,))],
        ),
        compiler_params=_cparams(("arbitrary",)),
        name="moe_combine",
    )(pos, y_sorted, x, mod, final_g)


def _combine_inproj_kernel(pos_ref, y_ref, x2_ref, modp_ref, g_ref, mod_ref, wq_ref, wh_ref, wf_ref,
                           x_ref, qkv_ref, hg_ref, pu_ref, ff_ref, buf, sem, *, tm, n_hg):
    i = pl.program_id(0)
    last = pl.num_programs(0) - 1
    groups = tm // SUBLANES

    def row_copy(tile, slot, g, u):
        src = pos_ref[tile * tm + g * SUBLANES + u]
        return pltpu.make_async_copy(y_ref.at[pl.ds(src, 1), :], buf.at[slot, g, pl.ds(u, 1), :], sem.at[slot])

    def wait_slot(slot):
        pltpu.make_async_copy(buf.at[slot], buf.at[slot], sem.at[slot]).wait()

    @pl.when(i == 0)
    def _():
        def eight(g, carry):
            for u in range(SUBLANES):
                row_copy(0, 0, g, u).start()
            return carry
        lax.fori_loop(0, groups, eight, 0)

    slot = i % 2
    wait_slot(slot)
    x = x2_ref[...] + (1.0 + modp_ref[0, 5:6, :]) * buf[slot].reshape(x2_ref.shape)
    x_ref[...] = x
    h = _norm_mod(x, g_ref[...], mod_ref[0, 0:1, :], mod_ref[0, 1:2, :]).astype(BF16)

    nxt = jnp.minimum(i + 1, last)
    for g in range(groups):
        for u in range(SUBLANES):
            row_copy(nxt, 1 - slot, g, u).start()

    n_qkv = qkv_ref.shape[-1]
    q_scale = jnp.where(_iota((1, n_qkv), 1) < n_qkv // 3, FOX_Q_SCALE, 1.0)
    qkv_ref[...] = (jnp.dot(h, wq_ref[...], preferred_element_type=F32) * q_scale).astype(BF16)
    hg_ref[...] = jnp.dot(h, wh_ref[:, 0:n_hg], preferred_element_type=F32)
    pu_ref[...] = jnp.dot(h, wh_ref[:, n_hg:], preferred_element_type=F32)
    ff_ref[...] = jnp.dot(h, wf_ref[...], preferred_element_type=F32)

    @pl.when(i == last)
    def _():
        wait_slot(1 - slot)


def _combine_inproj_call(pos, y_sorted, x2, mod_prev, g, mod, w_qkv, w_hp, w_ff, n_hg, tm, seq):
    t, d = x2.shape
    per_seq = seq // tm
    n_qkv, n_ff = w_qkv.shape[1], w_ff.shape[1]
    n_pu = w_hp.shape[1] - n_hg

    def tile(i, pos):
        return (i, 0)

    def fixed(i, pos):
        return (0, 0)

    def per_batch(i, pos):
        return (i // per_seq, 0, 0)

    return pl.pallas_call(
        functools.partial(_combine_inproj_kernel, tm=tm, n_hg=n_hg),
        out_shape=(
            jax.ShapeDtypeStruct((t, d), F32),
            jax.ShapeDtypeStruct((t, n_qkv), BF16),
            jax.ShapeDtypeStruct((t, n_hg), F32),
            jax.ShapeDtypeStruct((t, n_pu), F32),
            jax.ShapeDtypeStruct((t, n_ff), F32),
        ),
        grid_spec=pltpu.PrefetchScalarGridSpec(
            num_scalar_prefetch=1,
            grid=(t // tm,),
            in_specs=[pl.BlockSpec(memory_space=pl.ANY),
                      pl.BlockSpec((tm, d), tile),
                      pl.BlockSpec((1, N_MOD, d), per_batch),
                      pl.BlockSpec((1, d), fixed),
                      pl.BlockSpec((1, N_MOD, d), per_batch),
                      pl.BlockSpec((d, n_qkv), fixed),
                      pl.BlockSpec((d, n_hg + n_pu), fixed),
                      pl.BlockSpec((d, n_ff), fixed)],
            out_specs=(pl.BlockSpec((tm, d), tile), pl.BlockSpec((tm, n_qkv), tile), pl.BlockSpec((tm, n_hg), tile),
                       pl.BlockSpec((tm, n_pu), tile), pl.BlockSpec((tm, n_ff), tile)),
            scratch_shapes=[pltpu.VMEM((2, tm // SUBLANES, SUBLANES, d), F32), pltpu.SemaphoreType.DMA((2,))],
        ),
        compiler_params=_cparams(("arbitrary",)),
        name="combine_norm_inproj",
    )(pos, y_sorted, x2, mod_prev, g, mod, w_qkv, w_hp, w_ff)


def _pad_lanes(a, n=LANES):
    return jnp.pad(a, [(0, 0)] * (a.ndim - 1) + [(0, n - a.shape[-1])])


def kernel(x, c, w_ada, b_ada, norm1_g, w_in, fox_f_bias, fox_norm_g, hgrn_lb_logits, hgrn_norm_g, pool_w, pool_scale, w_out, norm2_g, router_group_w, router_group_b, router_expert_w, router_expert_b, expert_w_gate, expert_w_up, expert_w_down, final_norm_g):
    bsz, seq, d = x.shape
    depth = w_ada.shape[0]
    fox_heads = fox_f_bias.shape[1]
    fox_dim = fox_heads * HEAD_DIM
    hgrn_dim = hgrn_lb_logits.shape[1]
    pool_dim = pool_scale.shape[1]
    n_fox_pairs = fox_dim // LANES
    n_hgrn_pairs = hgrn_dim // LANES
    t = bsz * seq
    tm = min(512, seq)
    tq = min(512, seq)
    n_tiles = t // MOE_TILE_ROWS + N_CLASSES
    assert n_tiles <= LANES and t % MOE_TILE_ROWS == 0

    o_ff = 3 * fox_dim
    w_qkv = w_in[:, :, :o_ff].astype(BF16)
    w_hp = w_in[:, :, o_ff + fox_heads:].astype(BF16)
    w_ff = _pad_lanes(w_in[:, :, o_ff:o_ff + fox_heads]).astype(BF16)
    f_bias = _pad_lanes(fox_f_bias)
    groups = pool_w.shape[1]
    pool_bd = jnp.einsum('lgcd,gh->lgchd', pool_w, jnp.eye(groups, dtype=pool_w.dtype)).reshape(
        depth, pool_dim, pool_dim).astype(BF16)
    w_router = _pad_lanes(jnp.concatenate([router_group_w, router_expert_w], axis=-1))
    w_router_hi = w_router.astype(BF16)
    w_router = jnp.stack([w_router_hi, (w_router - w_router_hi.astype(F32)).astype(BF16)], axis=1)
    b_router = _pad_lanes(jnp.concatenate([router_group_b, router_expert_b], axis=-1))
    w_out_b = w_out.astype(BF16)
    pair_a = jnp.array(PAIR_SLOT_A, jnp.int32)
    pair_b = jnp.array(PAIR_SLOT_B, jnp.int32)
    final_g = final_norm_g.reshape(1, d)

    mod_all = _ada_call(c, w_ada, b_ada).reshape(depth, bsz, N_MOD, d)
    lower = _lb_call(hgrn_lb_logits)

    pending = None
    for l in range(depth):
        mod = mod_all[l]
        if pending is None:
            qkv, hg4, pu, ff = _inproj_call(x, norm1_g[l:l + 1], mod, w_qkv[l], w_hp[l], w_ff[l], 4 * hgrn_dim, tm)
        else:
            x, qkv, hg4, pu, ff = _combine_inproj_call(*pending, norm1_g[l:l + 1], mod, w_qkv[l], w_hp[l], w_ff[l],
                                                       4 * hgrn_dim, tm, seq)
            x, qkv, hg4, pu, ff = (a.reshape(bsz, seq, -1) for a in (x, qkv, hg4, pu, ff))
        fcol = _fbias_call(ff, f_bias[l:l + 1])
        o_fox = _fox_call(qkv, fcol, fox_norm_g[l].reshape(n_fox_pairs, 1, LANES), n_fox_pairs, tq)
        o_hgrn = _hgrn_call(hg4, lower[l].reshape(n_hgrn_pairs, 1, LANES),
                            hgrn_norm_g[l].reshape(n_hgrn_pairs, 1, LANES), n_hgrn_pairs)
        o_pool = _pool_call(pu, pool_bd[l], pool_scale[l:l + 1])
        x2, h2ext = _outproj_call(o_fox.reshape(t, -1), o_hgrn.reshape(t, -1), o_pool.reshape(t, -1),
                                  x.reshape(t, d), w_out_b[l], mod, norm2_g[l:l + 1],
                                  w_router[l], b_router[l:l + 1], tm, seq)
        pos8, meta = _sort_call(h2ext, d, MOE_TILE_ROWS)
        pos = pos8[0]
        n_used = meta[0, 1].astype(jnp.int32).reshape(1)
        tile = jnp.minimum(jnp.arange(n_tiles, dtype=jnp.int32), n_used[0] - 1)
        tile_cls = jnp.minimum(meta[:, 0].astype(jnp.int32)[tile], N_CLASSES - 1)
        e0 = (tile_cls // N_PAIRS) * EXPERTS_PER_GROUP
        ea = e0 + pair_a[tile_cls % N_PAIRS]
        eb = e0 + pair_b[tile_cls % N_PAIRS]
        h_sorted, wg_l, wu_l, wd_l = _dispatch_call(pos, h2ext, n_tiles * MOE_TILE_ROWS, expert_w_gate,
                                                    expert_w_up, expert_w_down, l, tm)
        y_sorted = _moe_call(ea, eb, n_used, h_sorted, wg_l, wu_l, wd_l, MOE_TILE_ROWS)
        pending = (pos, y_sorted, x2, mod)
    return _combine_call(*pending, final_g, tm, seq).reshape(bsz, seq, d)
```

```python
import functools

import jax
import jax.numpy as jnp
from jax import lax
from jax.experimental import pallas as pl
from jax.experimental.pallas import tpu as pltpu

F32 = jnp.float32
BF16 = jnp.bfloat16
HIGHEST = lax.Precision.HIGHEST

HEAD_DIM = 64
LANES = 128
SUBLANES = 8
EPS = 1e-6
N_MOD = 6
POOL_WINDOWS = (2, 4, 8, 16)
N_GROUPS = 4
EXPERTS_PER_GROUP = 4
N_EXPERTS = N_GROUPS * EXPERTS_PER_GROUP
ROUTER_EXPERT_LANE0 = N_GROUPS
N_PAIRS = 6
N_CLASSES = N_GROUPS * N_PAIRS
PAIR_SLOT_A = (0, 0, 0, 1, 1, 3)
PAIR_SLOT_B = (1, 2, 3, 3, 2, 2)
ROUTE_CLASS_LANE0 = 32
MOE_TILE_ROWS = 256
HGRN_CHUNK = 128
HGRN_LEVELS = (64, 32, 16, 8, 4, 2, 1)
HGRN_GROUP = 4
VMEM_LIMIT = 48 * 1024 * 1024
LOG2_E = 1.4426950408889634
FOX_Q_SCALE = HEAD_DIM ** -0.5 * LOG2_E


def _cparams(sem):
    return pltpu.CompilerParams(dimension_semantics=sem, vmem_limit_bytes=VMEM_LIMIT)


def _silu(x):
    return x * jax.nn.sigmoid(x)


def _iota(shape, dim):
    return lax.broadcasted_iota(jnp.int32, shape, dim)


def _ada_kernel(c_ref, w_ref, b_ref, o_ref):
    sc = _silu(c_ref[...]).astype(BF16)
    o_ref[0] = jnp.dot(sc, w_ref[0].astype(BF16), preferred_element_type=F32) + b_ref[0]


def _ada_call(c, w_ada, b_ada):
    depth, d, n = w_ada.shape
    bsz = c.shape[0]
    tn = 1536
    return pl.pallas_call(
        _ada_kernel,
        out_shape=jax.ShapeDtypeStruct((depth, bsz, n), F32),
        grid=(depth, n // tn),
        in_specs=[
            pl.BlockSpec((bsz, d), lambda l, j: (0, 0)),
            pl.BlockSpec((1, d, tn), lambda l, j: (l, 0, j)),
            pl.BlockSpec((1, 1, tn), lambda l, j: (l, 0, j)),
        ],
        out_specs=pl.BlockSpec((1, bsz, tn), lambda l, j: (l, 0, j)),
        compiler_params=_cparams(("arbitrary", "arbitrary")),
        name="ada_mod",
    )(c, w_ada, b_ada.reshape(depth, 1, n))


def _lb_kernel(x_ref, o_ref):
    x = x_ref[...]
    depth = x.shape[0]
    e = jnp.exp(x - jnp.max(x, axis=0, keepdims=True))
    p = e / jnp.sum(e, axis=0, keepdims=True)
    acc = jnp.zeros_like(p[0:1])
    o_ref[0:1, :] = acc
    for l in range(1, depth):
        acc = acc + p[l:l + 1]
        o_ref[l:l + 1, :] = acc


def _lb_call(lb_logits):
    return pl.pallas_call(
        _lb_kernel,
        out_shape=jax.ShapeDtypeStruct(lb_logits.shape, F32),
        name="hgrn_lower_bounds",
    )(lb_logits)


def _norm_mod(x, g, shift, scale):
    ms = jnp.mean(x * x, axis=-1, keepdims=True)
    return (x * lax.rsqrt(ms + EPS) * g) * (1.0 + scale) + shift


def _inproj_kernel(x_ref, g_ref, mod_ref, wq_ref, wh_ref, wf_ref, qkv_ref, hg_ref, pu_ref, ff_ref, *, n_hg):
    h = _norm_mod(x_ref[0], g_ref[...], mod_ref[0, 0:1, :], mod_ref[0, 1:2, :]).astype(BF16)
    n_qkv = qkv_ref.shape[-1]
    q_scale = jnp.where(_iota((1, n_qkv), 1) < n_qkv // 3, FOX_Q_SCALE, 1.0)
    qkv_ref[0] = (jnp.dot(h, wq_ref[...], preferred_element_type=F32) * q_scale).astype(BF16)
    hg_ref[0] = jnp.dot(h, wh_ref[:, 0:n_hg], preferred_element_type=F32)
    pu_ref[0] = jnp.dot(h, wh_ref[:, n_hg:], preferred_element_type=F32)
    ff_ref[0] = jnp.dot(h, wf_ref[...], preferred_element_type=F32)


def _inproj_call(x, g, mod, w_qkv, w_hp, w_ff, n_hg, tm):
    bsz, seq, d = x.shape
    n_qkv, n_ff = w_qkv.shape[1], w_ff.shape[1]
    n_pu = w_hp.shape[1] - n_hg
    return pl.pallas_call(
        functools.partial(_inproj_kernel, n_hg=n_hg),
        out_shape=(
            jax.ShapeDtypeStruct((bsz, seq, n_qkv), BF16),
            jax.ShapeDtypeStruct((bsz, seq, n_hg), F32),
            jax.ShapeDtypeStruct((bsz, seq, n_pu), F32),
            jax.ShapeDtypeStruct((bsz, seq, n_ff), F32),
        ),
        grid=(bsz, seq // tm),
        in_specs=[
            pl.BlockSpec((1, tm, d), lambda b, i: (b, i, 0)),
            pl.BlockSpec((1, d), lambda b, i: (0, 0)),
            pl.BlockSpec((1, N_MOD, d), lambda b, i: (b, 0, 0)),
            pl.BlockSpec((d, n_qkv), lambda b, i: (0, 0)),
            pl.BlockSpec((d, n_hg + n_pu), lambda b, i: (0, 0)),
            pl.BlockSpec((d, n_ff), lambda b, i: (0, 0)),
        ],
        out_specs=(
            pl.BlockSpec((1, tm, n_qkv), lambda b, i: (b, i, 0)),
            pl.BlockSpec((1, tm, n_hg), lambda b, i: (b, i, 0)),
            pl.BlockSpec((1, tm, n_pu), lambda b, i: (b, i, 0)),
            pl.BlockSpec((1, tm, n_ff), lambda b, i: (b, i, 0)),
        ),
        compiler_params=_cparams(("arbitrary", "arbitrary")),
        name="norm_inproj",
    )(x, g, mod, w_qkv, w_hp, w_ff)


def _fbias_kernel(ff_ref, bias_ref, fcol_ref, *, cb):
    seq = ff_ref.shape[1]
    tri = (_iota((cb, cb), 0) >= _iota((cb, cb), 1)).astype(F32)
    carry = jnp.zeros((1, LANES), F32)
    for blk in range(seq // cb):
        x = ff_ref[0, blk * cb:(blk + 1) * cb, :] + bias_ref[...]
        log_f = jnp.minimum(x, 0.0) - jnp.log1p(jnp.exp(-jnp.abs(x)))
        cs = jnp.dot(tri, log_f, precision=HIGHEST, preferred_element_type=F32) + carry
        fcol_ref[0, blk * cb:(blk + 1) * cb, :] = cs * LOG2_E
        carry = cs[cb - 1:cb, :]


def _fbias_call(ff, bias):
    bsz, seq, _ = ff.shape
    cb = min(256, seq)
    return pl.pallas_call(
        functools.partial(_fbias_kernel, cb=cb),
        out_shape=jax.ShapeDtypeStruct((bsz, seq, LANES), F32),
        grid=(bsz,),
        in_specs=[pl.BlockSpec((1, seq, LANES), lambda b: (b, 0, 0)),
                  pl.BlockSpec((1, LANES), lambda b: (0, 0))],
        out_specs=pl.BlockSpec((1, seq, LANES), lambda b: (b, 0, 0)),
        compiler_params=_cparams(("arbitrary",)),
        name="fox_forget_bias",
    )(ff, bias)


def _split3(f):
    hi = f.astype(BF16).astype(F32)
    rest = f - hi
    mid = rest.astype(BF16).astype(F32)
    return hi, mid, rest - mid


def _fox_kernel(q_ref, k_ref, v_ref, fc_ref, gn_ref, o_ref, k0_sc, k1_sc, v0_sc, v1_sc, *, tq):
    p = pl.program_id(1)
    seq = k_ref.shape[1]
    nb = seq // tq
    lane = _iota((1, LANES), 1)
    h0 = lane < HEAD_DIM
    data = (h0, jnp.logical_not(h0))
    base = (HEAD_DIM, 0)
    k_sc = (k0_sc, k1_sc)
    v_sc = (v0_sc, v1_sc)
    nt = (((1,), (1,)), ((), ()))
    heads = (0, 1)

    def head_f(fc, which):
        return jnp.sum(jnp.where(lane == 2 * p + which, fc, 0.0), axis=-1, keepdims=True)

    def with_bias_lanes(x, which, first3, last3):
        out = jnp.where(data[which], x, 0.0)
        for n in range(3):
            out = jnp.where(lane == base[which] + n, first3[n], out)
            out = jnp.where(lane == base[which] + 3 + n, last3[n], out)
        return out.astype(BF16)

    qs = {}
    for blk in range(nb):
        rows = slice(blk * tq, (blk + 1) * tq)
        fc = fc_ref[0, rows, :]
        qf = q_ref[0, rows, :].astype(F32)
        kf = k_ref[0, rows, :].astype(F32)
        vf = v_ref[0, rows, :].astype(F32)
        for which in heads:
            hi, mid, lo = _split3(head_f(fc, which))
            qs[blk, which] = with_bias_lanes(qf, which, (hi, mid, lo), (1.0, 1.0, 1.0))
            k_sc[which][rows, :] = with_bias_lanes(kf, which, (1.0, 1.0, 1.0), (-hi, -mid, -lo))
            v_sc[which][rows, :] = jnp.where(data[which], vf,
                                             jnp.where(lane == base[which], 1.0, 0.0)).astype(BF16)

    causal = _iota((tq, tq), 0) >= _iota((tq, tq), 1)
    m, acc = {}, {}
    for r in range(nb):
        keys = slice(r * tq, (r + 1) * tq)
        tasks = [(i, which) for i in range(r, nb) for which in heads]
        s = {}
        for i, which in tasks:
            s[i, which] = lax.dot_general(qs[i, which], k_sc[which][keys, :], nt, preferred_element_type=F32)
            if i == r:
                s[i, which] = jnp.where(causal, s[i, which], -jnp.inf)
        n = {}
        for t in tasks:
            n[t] = jnp.max(s[t], axis=-1, keepdims=True)
            if r > 0:
                n[t] = jnp.maximum(m[t], n[t])
        pv = {}
        for i, which in tasks:
            pr = jnp.exp2(s[i, which] - n[i, which]).astype(BF16)
            pv[i, which] = jnp.dot(pr, v_sc[which][keys, :], preferred_element_type=F32)
        for t in tasks:
            acc[t] = pv[t] if r == 0 else jnp.exp2(m[t] - n[t]) * acc[t] + pv[t]
            m[t] = n[t]

        l0 = jnp.sum(jnp.where(lane == base[0], acc[r, 0], 0.0), axis=-1, keepdims=True)
        l1 = jnp.sum(jnp.where(lane == base[1], acc[r, 1], 0.0), axis=-1, keepdims=True)
        o = jnp.where(h0, acc[r, 0] / l0, acc[r, 1] / l1)
        o2 = o * o
        ms0 = jnp.sum(jnp.where(h0, o2, 0.0), axis=-1, keepdims=True)
        ms1 = jnp.sum(jnp.where(h0, 0.0, o2), axis=-1, keepdims=True)
        ms = jnp.where(h0, ms0, ms1) * (1.0 / HEAD_DIM)
        o_ref[0, keys, :] = (o * lax.rsqrt(ms + EPS) * gn_ref[0]).astype(BF16)


def _fox_call(qkv, fcol, gn, n_pairs, tq):
    bsz, seq, _ = qkv.shape
    return pl.pallas_call(
        functools.partial(_fox_kernel, tq=tq),
        out_shape=jax.ShapeDtypeStruct((bsz, seq, n_pairs * LANES), BF16),
        grid=(bsz, n_pairs),
        in_specs=[
            pl.BlockSpec((1, seq, LANES), lambda b, p: (b, 0, p)),
            pl.BlockSpec((1, seq, LANES), lambda b, p: (b, 0, n_pairs + p)),
            pl.BlockSpec((1, seq, LANES), lambda b, p: (b, 0, 2 * n_pairs + p)),
            pl.BlockSpec((1, seq, LANES), lambda b, p: (b, 0, 0)),
            pl.BlockSpec((1, 1, LANES), lambda b, p: (p, 0, 0)),
        ],
        out_specs=pl.BlockSpec((1, seq, LANES), lambda b, p: (b, 0, p)),
        scratch_shapes=[pltpu.VMEM((seq, LANES), BF16)] * 4,
        compiler_params=_cparams(("arbitrary", "arbitrary")),
        name="fox_attention",
    )(qkv, qkv, qkv, fcol, gn)


def _hgrn_kernel(hq_ref, hf_ref, hi_ref, hg_ref, lb_ref, gn_ref, o_ref, zero_ref):
    ch = HGRN_CHUNK
    seq = hq_ref.shape[1]
    lane = _iota((1, LANES), 1)
    h0 = lane < HEAD_DIM
    r = _iota((ch, ch), 0)
    c = _iota((ch, ch), 1)
    tri = (r >= c).astype(BF16)
    same_head = (r < HEAD_DIM) == (c < HEAD_DIM)
    seg = same_head.astype(BF16)
    r2 = _iota((ch, 2 * ch), 0)
    c2 = _iota((ch, 2 * ch), 1) % ch
    level_masks = [(r2 // (2 * m) == c2 // (2 * m)) & (r2 % (2 * m) >= m) & (c2 % (2 * m) < m)
                   for m in HGRN_LEVELS]
    small_levels = [m for m in HGRN_LEVELS if m < SUBLANES]
    pick = jnp.concatenate([(c == (r // (2 * m)) * (2 * m) + (m - 1)) for m in small_levels], axis=0).astype(BF16)
    lb = lb_ref[0]
    nt = (((1,), (1,)), ((), ()))
    zero_b = jnp.zeros((ch, LANES), BF16)
    zero_ref[...] = jnp.zeros(zero_ref.shape, F32)

    def both_heads(x):
        return jnp.concatenate([jnp.where(h0, x, zero_b), jnp.where(h0, zero_b, x)], axis=0)

    def sum3(x):
        return x[:, 0:LANES] + x[:, LANES:2 * LANES] + x[:, 2 * LANES:]

    def parts3(x):
        return jnp.concatenate([part.astype(BF16) for part in _split3(x)], axis=1)

    def group(gi, state_t):
        n = HGRN_GROUP
        sls = [pl.ds(pl.multiple_of((gi * n + i) * ch, ch), ch) for i in range(n)]
        k, q, v, vb, b, picked = [], [], [], [], [], []
        for sl in sls:
            f = lb + (1.0 - lb) * jax.nn.sigmoid(hf_ref[0, sl, :])
            k.append(1.0 - f)
            q.append(_silu(hq_ref[0, sl, :]))
            v.append(hi_ref[0, sl, :])
            vb.append(v[-1].astype(BF16))
            b.append(sum3(jnp.dot(tri, parts3(jnp.log(f) * LOG2_E), preferred_element_type=F32)))
        for i in range(n):
            picked.append(jnp.dot(pick, parts3(b[i]), preferred_element_type=F32))

        sc = [jnp.zeros((ch, 2 * ch), F32) for _ in range(n)]
        for li, m in enumerate(HGRN_LEVELS):
            for i in range(n):
                if m >= SUBLANES:
                    ref_l = jnp.broadcast_to(b[i].reshape(ch // (2 * m), 2 * m, LANES)[:, m - 1:m, :],
                                             (ch // (2 * m), 2 * m, LANES)).reshape(ch, LANES)
                else:
                    at = small_levels.index(m) * ch
                    ref_l = sum3(picked[i][at:at + ch])
                e = jnp.exp2(-jnp.abs(b[i] - ref_l))
                s = lax.dot_general((q[i] * e).astype(BF16), both_heads((k[i] * e).astype(BF16)), nt,
                                    preferred_element_type=F32)
                sc[i] = jnp.where(level_masks[li], s, sc[i])

        o, upd_t, dec = [], [], []
        for i in range(n):
            oi = jnp.dot(sc[i].astype(BF16), both_heads(vb[i]), preferred_element_type=F32)
            o.append(oi + jnp.dot((q[i] * k[i]).astype(BF16), seg, preferred_element_type=F32) * v[i])
            b_last = b[i][ch - 1:ch, :]
            k_dec = (k[i] * jnp.exp2(b_last - b[i])).astype(BF16)
            upd = jnp.dot(v[i].T.astype(BF16), k_dec, preferred_element_type=F32)
            upd_t.append(jnp.where(same_head, upd, 0.0))
            dec.append(jnp.exp2(b_last))

        states = [state_t]
        for i in range(n):
            states.append(dec[i] * states[i] + upd_t[i])

        for i, sl in enumerate(sls):
            oi = o[i] + lax.dot_general((q[i] * jnp.exp2(b[i])).astype(BF16), states[i].astype(BF16), nt,
                                        preferred_element_type=F32)
            o2 = oi * oi
            ms0 = jnp.sum(jnp.where(h0, o2, 0.0), axis=-1, keepdims=True)
            ms1 = jnp.sum(jnp.where(h0, 0.0, o2), axis=-1, keepdims=True)
            ms = jnp.where(h0, ms0, ms1) * (1.0 / HEAD_DIM)
            out = oi * lax.rsqrt(ms + EPS) * gn_ref[0] * _silu(hg_ref[0, sl, :])
            o_ref[0, sl, :] = out.astype(BF16)
        return states[n]

    lax.fori_loop(0, seq // (HGRN_GROUP * ch), group, jnp.zeros((LANES, LANES), F32))


def _hgrn_call(hg4, lb, gn, n_pairs, zero_rows, zero_width):
    bsz, seq, _ = hg4.shape
    steps = bsz * n_pairs
    zb = next(k * MOE_TILE_ROWS for k in range(1, zero_rows // MOE_TILE_ROWS + 1)
              if zero_rows % (k * MOE_TILE_ROWS) == 0 and zero_rows // (k * MOE_TILE_ROWS) <= steps)
    last_zero_block = zero_rows // zb - 1

    def spec(off):
        return pl.BlockSpec((1, seq, LANES), lambda b, p: (b, 0, off + p))

    return pl.pallas_call(
        _hgrn_kernel,
        out_shape=(jax.ShapeDtypeStruct((bsz, seq, n_pairs * LANES), BF16),
                   jax.ShapeDtypeStruct((zero_rows, zero_width), F32)),
        grid=(bsz, n_pairs),
        in_specs=[spec(0), spec(n_pairs), spec(2 * n_pairs), spec(3 * n_pairs),
                  pl.BlockSpec((1, 1, LANES), lambda b, p: (p, 0, 0)),
                  pl.BlockSpec((1, 1, LANES), lambda b, p: (p, 0, 0))],
        out_specs=(pl.BlockSpec((1, seq, LANES), lambda b, p: (b, 0, p)),
                   pl.BlockSpec((zb, zero_width), lambda b, p: (jnp.minimum(b * n_pairs + p, last_zero_block), 0))),
        compiler_params=_cparams(("arbitrary", "arbitrary")),
        name="hgrn2",
    )(hg4, hg4, hg4, hg4, lb, gn)


def _pool_kernel(u_ref, w_ref, s_ref, o_ref):
    u = u_ref[0]
    seq, n = u.shape
    t = _iota((seq, 1), 0)
    lane = _iota((1, n), 1)

    def shifted(x, k):
        return jnp.where(t >= k, pltpu.roll(x, k, axis=0), 0.0)

    sums = []
    s = u
    for w in POOL_WINDOWS:
        s = s + shifted(s, w // 2)
        sums.append(s)
    pos1 = (t + 1).astype(F32)
    group = len(POOL_WINDOWS) - 1
    mean = sums[group] / jnp.minimum(pos1, float(POOL_WINDOWS[group]))
    group_dim = n // len(POOL_WINDOWS)
    for gi in range(group - 1, -1, -1):
        mean = jnp.where(lane < (gi + 1) * group_dim,
                         sums[gi] / jnp.minimum(pos1, float(POOL_WINDOWS[gi])), mean)
    pooled = (mean - u).astype(BF16)
    o_ref[0] = (jnp.dot(pooled, w_ref[...], preferred_element_type=F32) * s_ref[...]).astype(BF16)


def _pool_call(pu, w_bd, scale):
    bsz, seq, n = pu.shape
    return pl.pallas_call(
        _pool_kernel,
        out_shape=jax.ShapeDtypeStruct((bsz, seq, n), BF16),
        grid=(bsz,),
        in_specs=[pl.BlockSpec((1, seq, n), lambda b: (b, 0, 0)),
                  pl.BlockSpec((n, n), lambda b: (0, 0)),
                  pl.BlockSpec((1, n), lambda b: (0, 0))],
        out_specs=pl.BlockSpec((1, seq, n), lambda b: (b, 0, 0)),
        compiler_params=_cparams(("arbitrary",)),
        name="multiscale_pool",
    )(pu, w_bd, scale)


def _route(logits):
    lane = _iota(logits.shape, 1)
    lane_f = lane.astype(F32)
    big = float(LANES)
    is_g = lane < N_GROUPS
    gl = jnp.where(is_g, logits, -jnp.inf)
    gmax = jnp.max(gl, axis=-1, keepdims=True)
    gsum = jnp.sum(jnp.where(is_g, jnp.exp(gl - gmax), 0.0), axis=-1, keepdims=True)
    p_group = 1.0 / gsum
    g_sel = jnp.min(jnp.where(gl == gmax, lane_f, big), axis=-1, keepdims=True)
    lo = ROUTER_EXPERT_LANE0 + EXPERTS_PER_GROUP * g_sel
    in_group = (lane_f >= lo) & (lane_f < lo + EXPERTS_PER_GROUP)
    el = jnp.where(in_group, logits, -jnp.inf)
    m1 = jnp.max(el, axis=-1, keepdims=True)
    i1 = jnp.min(jnp.where(el == m1, lane_f, big), axis=-1, keepdims=True)
    el2 = jnp.where(lane_f == i1, -jnp.inf, el)
    m2 = jnp.max(el2, axis=-1, keepdims=True)
    i2 = jnp.min(jnp.where(el2 == m2, lane_f, big), axis=-1, keepdims=True)
    e2 = jnp.exp(m2 - m1)
    w1 = p_group / (1.0 + e2)
    w2 = p_group * e2 / (1.0 + e2)
    a = jnp.minimum(i1, i2) - lo
    b = jnp.maximum(i1, i2) - lo
    pair = jnp.where(a == 0.0, b - 1.0, jnp.where(a == 1.0, 6.0 - b, float(N_PAIRS - 1)))
    top1_in_a = (i1 < i2) != (pair == float(N_PAIRS - 1))
    cls_lane = ROUTE_CLASS_LANE0 + g_sel * N_PAIRS + pair
    return (jnp.where(lane == 0, jnp.where(top1_in_a, w1, w2), 0.0)
            + jnp.where(lane == 1, jnp.where(top1_in_a, w2, w1), 0.0)
            + jnp.where(lane_f == cls_lane, 1.0, 0.0))


def _outproj_kernel(of_ref, oh_ref, op_ref, x_ref, w_ref, mod_ref, g_ref, wr_ref, br_ref,
                    xn_ref, h2_ref, *, n_fox, n_hgrn):
    d = x_ref.shape[-1]
    y = jnp.dot(of_ref[...], w_ref[0:n_fox, :], preferred_element_type=F32)
    y = y + jnp.dot(oh_ref[...], w_ref[n_fox:n_fox + n_hgrn, :], preferred_element_type=F32)
    y = y + jnp.dot(op_ref[...], w_ref[n_fox + n_hgrn:, :], preferred_element_type=F32)
    xn = x_ref[...] + (1.0 + mod_ref[0, 2:3, :]) * y
    xn_ref[...] = xn
    h2 = _norm_mod(xn, g_ref[...], mod_ref[0, 3:4, :], mod_ref[0, 4:5, :])
    h2_ref[:, 0:d] = h2
    h_hi = h2.astype(BF16)
    h_lo = (h2 - h_hi.astype(F32)).astype(BF16)
    logits = (jnp.dot(h_hi, wr_ref[0], preferred_element_type=F32)
              + jnp.dot(h_lo, wr_ref[0], preferred_element_type=F32)
              + jnp.dot(h_hi, wr_ref[1], preferred_element_type=F32)) + br_ref[...]
    h2_ref[:, d:] = _route(logits)


def _outproj_call(o_fox, o_hgrn, o_pool, x, w_out, mod, g2, w_router, b_router, tm, seq):
    t, d = x.shape
    n_fox, n_hgrn, n_pool = o_fox.shape[-1], o_hgrn.shape[-1], o_pool.shape[-1]
    per_seq = seq // tm
    return pl.pallas_call(
        functools.partial(_outproj_kernel, n_fox=n_fox, n_hgrn=n_hgrn),
        out_shape=(jax.ShapeDtypeStruct((t, d), F32),
                   jax.ShapeDtypeStruct((t, d + LANES), F32)),
        grid=(t // tm,),
        in_specs=[
            pl.BlockSpec((tm, n_fox), lambda i: (i, 0)),
            pl.BlockSpec((tm, n_hgrn), lambda i: (i, 0)),
            pl.BlockSpec((tm, n_pool), lambda i: (i, 0)),
            pl.BlockSpec((tm, d), lambda i: (i, 0)),
            pl.BlockSpec((d, d), lambda i: (0, 0)),
            pl.BlockSpec((1, N_MOD, d), lambda i: (i // per_seq, 0, 0)),
            pl.BlockSpec((1, d), lambda i: (0, 0)),
            pl.BlockSpec((2, d, LANES), lambda i: (0, 0, 0)),
            pl.BlockSpec((1, LANES), lambda i: (0, 0)),
        ],
        out_specs=(pl.BlockSpec((tm, d), lambda i: (i, 0)),
                   pl.BlockSpec((tm, d + LANES), lambda i: (i, 0))),
        compiler_params=_cparams(("arbitrary",)),
        name="outproj_router",
    )(o_fox, o_hgrn, o_pool, x, w_out, mod, g2, w_router, b_router)


def _sort_kernel(route_ref, pos_ref, meta_ref, *, tile_rows, cb):
    t = route_ref.shape[0]
    lane = _iota((1, LANES), 1)
    is_cls = (lane >= ROUTE_CLASS_LANE0) & (lane < ROUTE_CLASS_LANE0 + N_CLASSES)
    nblk = t // cb
    zero_row = jnp.zeros((1, LANES), F32)

    def onehot(i):
        return jnp.where(is_cls, route_ref[pl.ds(pl.multiple_of(i * cb, cb), cb), :], 0.0)

    counts = lax.fori_loop(0, nblk, lambda i, acc: acc + jnp.sum(onehot(i), axis=0, keepdims=True), zero_row)
    padded = jnp.floor((counts + float(tile_rows - 1)) * (1.0 / tile_rows)) * float(tile_rows)
    before = (_iota((LANES, LANES), 0) < _iota((LANES, LANES), 1)).astype(F32)
    offs = jnp.dot(jnp.broadcast_to(padded, (8, LANES)), before, precision=HIGHEST,
                   preferred_element_type=F32)[0:1]
    ends = offs + padded
    strict = (_iota((cb, cb), 0) > _iota((cb, cb), 1)).astype(BF16)
    ones8 = jnp.ones((8, LANES), F32)
    nt = (((1,), (1,)), ((), ()))

    def place(i, seen):
        oh = onehot(i)
        rank = jnp.dot(strict, oh.astype(BF16), preferred_element_type=F32) + seen
        dest = oh * (rank + offs)
        rows = lax.dot_general(ones8, dest, nt, precision=HIGHEST, preferred_element_type=F32)
        pos_ref[:, pl.ds(pl.multiple_of(i * cb, cb), cb)] = rows.astype(jnp.int32)
        return seen + jnp.sum(oh, axis=0, keepdims=True)

    lax.fori_loop(0, nblk, place, zero_row)
    tile_start = _iota((LANES, LANES), 0).astype(F32) * float(tile_rows)
    tile_cls = jnp.sum(jnp.where(is_cls & (ends <= tile_start), 1.0, 0.0), axis=-1, keepdims=True)
    n_used = jnp.sum(jnp.where(lane == ROUTE_CLASS_LANE0 + N_CLASSES - 1, ends, 0.0), axis=-1,
                     keepdims=True) * (1.0 / tile_rows)
    meta_ref[...] = jnp.where(lane == 0, tile_cls, jnp.where(lane == 1, n_used, 0.0))


def _sort_call(h2ext, d, tile_rows):
    t = h2ext.shape[0]
    cb = min(512, t)
    return pl.pallas_call(
        functools.partial(_sort_kernel, tile_rows=tile_rows, cb=cb),
        out_shape=(jax.ShapeDtypeStruct((8, t), jnp.int32),
                   jax.ShapeDtypeStruct((LANES, LANES), F32)),
        grid=(1,),
        in_specs=[pl.BlockSpec((t, LANES), lambda i: (0, d // LANES))],
        out_specs=(pl.BlockSpec((8, t), lambda i: (0, 0)),
                   pl.BlockSpec((LANES, LANES), lambda i: (0, 0))),
        compiler_params=_cparams(("arbitrary",)),
        name="route_sort",
    )(h2ext)


def _dispatch_kernel(pos_ref, h_ref, init_ref, wg_ref, wu_ref, wd_ref, out_ref, wgb_ref, wub_ref, wdb_ref, sem,
                     *, tm):
    del init_ref
    base = pl.program_id(0) * tm
    for g in range(tm // SUBLANES):
        for u in range(SUBLANES):
            dst = pos_ref[base + g * SUBLANES + u]
            pltpu.make_async_copy(h_ref.at[g, pl.ds(u, 1), :], out_ref.at[pl.ds(dst, 1), :], sem).start()
    wgb_ref[...] = wg_ref[0].astype(BF16)
    wub_ref[...] = wu_ref[0].astype(BF16)
    wdb_ref[...] = wd_ref[0].astype(BF16)
    pltpu.make_async_copy(h_ref, h_ref, sem).wait()


def _dispatch_call(pos, h2ext, zeros, wg, wu, wd, layer, tm):
    t, w = h2ext.shape
    n_rows = zeros.shape[0]
    _, n_e, d, f = wg.shape
    steps = t // tm
    parts = steps // n_e
    assert steps == parts * n_e and d % (parts * SUBLANES) == 0 and f % (parts * SUBLANES) == 0

    def w_in(i, pos):
        return (layer, i // parts, i % parts, 0)

    def w_out(i, pos):
        return (i // parts, i % parts, 0)

    return pl.pallas_call(
        functools.partial(_dispatch_kernel, tm=tm),
        out_shape=(jax.ShapeDtypeStruct((n_rows, w), F32),
                   jax.ShapeDtypeStruct((n_e, d, f), BF16),
                   jax.ShapeDtypeStruct((n_e, d, f), BF16),
                   jax.ShapeDtypeStruct((n_e, f, d), BF16)),
        grid_spec=pltpu.PrefetchScalarGridSpec(
            num_scalar_prefetch=1,
            grid=(steps,),
            in_specs=[pl.BlockSpec((tm // SUBLANES, SUBLANES, w), lambda i, pos: (i, 0, 0)),
                      pl.BlockSpec(memory_space=pl.ANY),
                      pl.BlockSpec((1, 1, d // parts, f), w_in),
                      pl.BlockSpec((1, 1, d // parts, f), w_in),
                      pl.BlockSpec((1, 1, f // parts, d), w_in)],
            out_specs=(pl.BlockSpec(memory_space=pl.ANY),
                       pl.BlockSpec((1, d // parts, f), w_out),
                       pl.BlockSpec((1, d // parts, f), w_out),
                       pl.BlockSpec((1, f // parts, d), w_out)),
            scratch_shapes=[pltpu.SemaphoreType.DMA],
        ),
        input_output_aliases={2: 0},
        compiler_params=_cparams(("arbitrary",)),
        name="moe_dispatch",
    )(pos, h2ext.reshape(t // SUBLANES, SUBLANES, w), zeros, wg, wu, wd)


def _moe_kernel(ea_ref, eb_ref, nu_ref, h_ref, wga_ref, wua_ref, wda_ref, wgb_ref, wub_ref, wdb_ref, o_ref):
    del ea_ref, eb_ref
    d = o_ref.shape[-1]

    @pl.when(pl.program_id(0) < nu_ref[0])
    def _():
        h = h_ref[:, 0:d].astype(BF16)
        tail = h_ref[:, d:]
        lane = _iota((1, LANES), 1)
        w_a = jnp.sum(jnp.where(lane == 0, tail, 0.0), axis=-1, keepdims=True)
        w_b = jnp.sum(jnp.where(lane == 1, tail, 0.0), axis=-1, keepdims=True)

        def expert(wg_ref, wu_ref, wd_ref):
            hidden = _silu(jnp.dot(h, wg_ref[0], preferred_element_type=F32)) * jnp.dot(
                h, wu_ref[0], preferred_element_type=F32)
            return jnp.dot(hidden.astype(BF16), wd_ref[0], preferred_element_type=F32)

        o_ref[...] = w_a * expert(wga_ref, wua_ref, wda_ref) + w_b * expert(wgb_ref, wub_ref, wdb_ref)

    @pl.when(pl.program_id(0) >= nu_ref[0])
    def _():
        o_ref[...] = jnp.zeros_like(o_ref)


def _moe_call(ea, eb, n_used, h_sorted, wg, wu, wd, tile_rows):
    n_rows, w = h_sorted.shape
    _, d, f = wg.shape

    def rows(j, ea, eb, nu):
        return (jnp.maximum(jnp.minimum(j, nu[0] - 1), 0), 0)

    def wa(j, ea, eb, nu):
        return (ea[j], 0, 0)

    def wb(j, ea, eb, nu):
        return (eb[j], 0, 0)

    return pl.pallas_call(
        _moe_kernel,
        out_shape=jax.ShapeDtypeStruct((n_rows, d), F32),
        grid_spec=pltpu.PrefetchScalarGridSpec(
            num_scalar_prefetch=3,
            grid=(n_rows // tile_rows,),
            in_specs=[pl.BlockSpec((tile_rows, w), rows),
                      pl.BlockSpec((1, d, f), wa), pl.BlockSpec((1, d, f), wa), pl.BlockSpec((1, f, d), wa),
                      pl.BlockSpec((1, d, f), wb), pl.BlockSpec((1, d, f), wb), pl.BlockSpec((1, f, d), wb)],
            out_specs=pl.BlockSpec((tile_rows, d), lambda j, ea, eb, nu: (j, 0)),
        ),
        compiler_params=_cparams(("arbitrary",)),
        name="moe_experts",
    )(ea, eb, n_used, h_sorted, wg, wu, wd, wg, wu, wd)


def _combine_kernel(pos_ref, y_ref, x_ref, mod_ref, fg_ref, o_ref, buf, sem, *, tm):
    i = pl.program_id(0)

    def issue(tile, slot):
        def eight(g, carry):
            for u in range(SUBLANES):
                src = pos_ref[tile * tm + g * SUBLANES + u]
                pltpu.make_async_copy(y_ref.at[pl.ds(src, 1), :], buf.at[slot, g, pl.ds(u, 1), :],
                                      sem.at[slot]).start()
            return carry
        lax.fori_loop(0, tm // SUBLANES, eight, 0)

    @pl.when(i == 0)
    def _():
        issue(0, 0)

    @pl.when(i + 1 < pl.num_programs(0))
    def _():
        issue(i + 1, (i + 1) % 2)

    slot = i % 2
    pltpu.make_async_copy(buf.at[slot], buf.at[slot], sem.at[slot]).wait()
    out = x_ref[...] + (1.0 + mod_ref[0, 5:6, :]) * buf[slot].reshape(x_ref.shape)
    ms = jnp.mean(out * out, axis=-1, keepdims=True)
    o_ref[...] = out * lax.rsqrt(ms + EPS) * fg_ref[...]


def _combine_call(pos, y_sorted, x, mod, final_g, tm, seq):
    t, d = x.shape
    per_seq = seq // tm
    return pl.pallas_call(
        functools.partial(_combine_kernel, tm=tm),
        out_shape=jax.ShapeDtypeStruct((t, d), F32),
        grid_spec=pltpu.PrefetchScalarGridSpec(
            num_scalar_prefetch=1,
            grid=(t // tm,),
            in_specs=[pl.BlockSpec(memory_space=pl.ANY),
                      pl.BlockSpec((tm, d), lambda i, pos: (i, 0)),
                      pl.BlockSpec((1, N_MOD, d), lambda i, pos: (i // per_seq, 0, 0)),
                      pl.BlockSpec((1, d), lambda i, pos: (0, 0))],
            out_specs=pl.BlockSpec((tm, d), lambda i, pos: (i, 0)),
            scratch_shapes=[pltpu.VMEM((2, tm // SUBLANES, SUBLANES, d), F32), pltpu.SemaphoreType.DMA((2,))],
        ),
        compiler_params=_cparams(("arbitrary",)),
        name="moe_combine",
    )(pos, y_sorted, x, mod, final_g)


def _combine_inproj_kernel(pos_ref, y_ref, x2_ref, modp_ref, g_ref, mod_ref, wq_ref, wh_ref, wf_ref,
                           x_ref, qkv_ref, hg_ref, pu_ref, ff_ref, buf, sem, *, tm, n_hg):
    i = pl.program_id(0)
    last = pl.num_programs(0) - 1
    groups = tm // SUBLANES

    def row_copy(tile, slot, g, u):
        src = pos_ref[tile * tm + g * SUBLANES + u]
        return pltpu.make_async_copy(y_ref.at[pl.ds(src, 1), :], buf.at[slot, g, pl.ds(u, 1), :], sem.at[slot])

    def wait_slot(slot):
        pltpu.make_async_copy(buf.at[slot], buf.at[slot], sem.at[slot]).wait()

    @pl.when(i == 0)
    def _():
        def eight(g, carry):
            for u in range(SUBLANES):
                row_copy(0, 0, g, u).start()
            return carry
        lax.fori_loop(0, groups, eight, 0)

    slot = i % 2
    wait_slot(slot)
    x = x2_ref[...] + (1.0 + modp_ref[0, 5:6, :]) * buf[slot].reshape(x2_ref.shape)
    x_ref[...] = x
    h = _norm_mod(x, g_ref[...], mod_ref[0, 0:1, :], mod_ref[0, 1:2, :]).astype(BF16)

    nxt = jnp.minimum(i + 1, last)
    for g in range(groups):
        for u in range(SUBLANES):
            row_copy(nxt, 1 - slot, g, u).start()

    n_qkv = qkv_ref.shape[-1]
    q_scale = jnp.where(_iota((1, n_qkv), 1) < n_qkv // 3, FOX_Q_SCALE, 1.0)
    qkv_ref[...] = (jnp.dot(h, wq_ref[...], preferred_element_type=F32) * q_scale).astype(BF16)
    hg_ref[...] = jnp.dot(h, wh_ref[:, 0:n_hg], preferred_element_type=F32)
    pu_ref[...] = jnp.dot(h, wh_ref[:, n_hg:], preferred_element_type=F32)
    ff_ref[...] = jnp.dot(h, wf_ref[...], preferred_element_type=F32)

    @pl.when(i == last)
    def _():
        wait_slot(1 - slot)


def _combine_inproj_call(pos, y_sorted, x2, mod_prev, g, mod, w_qkv, w_hp, w_ff, n_hg, tm, seq):
    t, d = x2.shape
    per_seq = seq // tm
    n_qkv, n_ff = w_qkv.shape[1], w_ff.shape[1]
    n_pu = w_hp.shape[1] - n_hg

    def tile(i, pos):
        return (i, 0)

    def fixed(i, pos):
        return (0, 0)

    def per_batch(i, pos):
        return (i // per_seq, 0, 0)

    return pl.pallas_call(
        functools.partial(_combine_inproj_kernel, tm=tm, n_hg=n_hg),
        out_shape=(
            jax.ShapeDtypeStruct((t, d), F32),
            jax.ShapeDtypeStruct((t, n_qkv), BF16),
            jax.ShapeDtypeStruct((t, n_hg), F32),
            jax.ShapeDtypeStruct((t, n_pu), F32),
            jax.ShapeDtypeStruct((t, n_ff), F32),
        ),
        grid_spec=pltpu.PrefetchScalarGridSpec(
            num_scalar_prefetch=1,
            grid=(t // tm,),
            in_specs=[pl.BlockSpec(memory_space=pl.ANY),
                      pl.BlockSpec((tm, d), tile),
                      pl.BlockSpec((1, N_MOD, d), per_batch),
                      pl.BlockSpec((1, d), fixed),
                      pl.BlockSpec((1, N_MOD, d), per_batch),
                      pl.BlockSpec((d, n_qkv), fixed),
                      pl.BlockSpec((d, n_hg + n_pu), fixed),
                      pl.BlockSpec((d, n_ff), fixed)],
            out_specs=(pl.BlockSpec((tm, d), tile), pl.BlockSpec((tm, n_qkv), tile), pl.BlockSpec((tm, n_hg), tile),
                       pl.BlockSpec((tm, n_pu), tile), pl.BlockSpec((tm, n_ff), tile)),
            scratch_shapes=[pltpu.VMEM((2, tm // SUBLANES, SUBLANES, d), F32), pltpu.SemaphoreType.DMA((2,))],
        ),
        compiler_params=_cparams(("arbitrary",)),
        name="combine_norm_inproj",
    )(pos, y_sorted, x2, mod_prev, g, mod, w_qkv, w_hp, w_ff)


def _pad_lanes(a, n=LANES):
    return jnp.pad(a, [(0, 0)] * (a.ndim - 1) + [(0, n - a.shape[-1])])


def kernel(x, c, w_ada, b_ada, norm1_g, w_in, fox_f_bias, fox_norm_g, hgrn_lb_logits, hgrn_norm_g, pool_w, pool_scale, w_out, norm2_g, router_group_w, router_group_b, router_expert_w, router_expert_b, expert_w_gate, expert_w_up, expert_w_down, final_norm_g):
    bsz, seq, d = x.shape
    depth = w_ada.shape[0]
    fox_heads = fox_f_bias.shape[1]
    fox_dim = fox_heads * HEAD_DIM
    hgrn_dim = hgrn_lb_logits.shape[1]
    pool_dim = pool_scale.shape[1]
    n_fox_pairs = fox_dim // LANES
    n_hgrn_pairs = hgrn_dim // LANES
    t = bsz * seq
    tm = min(512, seq)
    tq = min(512, seq)
    n_tiles = t // MOE_TILE_ROWS + N_CLASSES
    assert n_tiles <= LANES and t % MOE_TILE_ROWS == 0

    o_ff = 3 * fox_dim
    w_qkv = w_in[:, :, :o_ff].astype(BF16)
    w_hp = w_in[:, :, o_ff + fox_heads:].astype(BF16)
    w_ff = _pad_lanes(w_in[:, :, o_ff:o_ff + fox_heads]).astype(BF16)
    f_bias = _pad_lanes(fox_f_bias)
    groups = pool_w.shape[1]
    pool_bd = jnp.einsum('lgcd,gh->lgchd', pool_w, jnp.eye(groups, dtype=pool_w.dtype)).reshape(
        depth, pool_dim, pool_dim).astype(BF16)
    w_router = _pad_lanes(jnp.concatenate([router_group_w, router_expert_w], axis=-1))
    w_router_hi = w_router.astype(BF16)
    w_router = jnp.stack([w_router_hi, (w_router - w_router_hi.astype(F32)).astype(BF16)], axis=1)
    b_router = _pad_lanes(jnp.concatenate([router_group_b, router_expert_b], axis=-1))
    w_out_b = w_out.astype(BF16)
    pair_a = jnp.array(PAIR_SLOT_A, jnp.int32)
    pair_b = jnp.array(PAIR_SLOT_B, jnp.int32)
    final_g = final_norm_g.reshape(1, d)

    mod_all = _ada_call(c, w_ada, b_ada).reshape(depth, bsz, N_MOD, d)
    lower = _lb_call(hgrn_lb_logits)

    pending = None
    for l in range(depth):
        mod = mod_all[l]
        if pending is None:
            qkv, hg4, pu, ff = _inproj_call(x, norm1_g[l:l + 1], mod, w_qkv[l], w_hp[l], w_ff[l], 4 * hgrn_dim, tm)
        else:
            x, qkv, hg4, pu, ff = _combine_inproj_call(*pending, norm1_g[l:l + 1], mod, w_qkv[l], w_hp[l], w_ff[l],
                                                       4 * hgrn_dim, tm, seq)
            x, qkv, hg4, pu, ff = (a.reshape(bsz, seq, -1) for a in (x, qkv, hg4, pu, ff))
        fcol = _fbias_call(ff, f_bias[l:l + 1])
        o_fox = _fox_call(qkv, fcol, fox_norm_g[l].reshape(n_fox_pairs, 1, LANES), n_fox_pairs, tq)
        o_hgrn, zeros = _hgrn_call(hg4, lower[l].reshape(n_hgrn_pairs, 1, LANES),
                                   hgrn_norm_g[l].reshape(n_hgrn_pairs, 1, LANES), n_hgrn_pairs,
                                   n_tiles * MOE_TILE_ROWS, d + LANES)
        o_pool = _pool_call(pu, pool_bd[l], pool_scale[l:l + 1])
        x2, h2ext = _outproj_call(o_fox.reshape(t, -1), o_hgrn.reshape(t, -1), o_pool.reshape(t, -1),
                                  x.reshape(t, d), w_out_b[l], mod, norm2_g[l:l + 1],
                                  w_router[l], b_router[l:l + 1], tm, seq)
        pos8, meta = _sort_call(h2ext, d, MOE_TILE_ROWS)
        pos = pos8[0]
        n_used = meta[0, 1].astype(jnp.int32).reshape(1)
        tile = jnp.minimum(jnp.arange(n_tiles, dtype=jnp.int32), n_used[0] - 1)
        tile_cls = jnp.minimum(meta[:, 0].astype(jnp.int32)[tile], N_CLASSES - 1)
        e0 = (tile_cls // N_PAIRS) * EXPERTS_PER_GROUP
        ea = e0 + pair_a[tile_cls % N_PAIRS]
        eb = e0 + pair_b[tile_cls % N_PAIRS]
        h_sorted, wg_l, wu_l, wd_l = _dispatch_call(pos, h2ext, zeros, expert_w_gate, expert_w_up, expert_w_down,
                                                    l, tm)
        y_sorted = _moe_call(ea, eb, n_used, h_sorted, wg_l, wu_l, wd_l, MOE_TILE_ROWS)
        pending = (pos, y_sorted, x2, mod)
    return _combine_call(*pending, final_g, tm, seq).reshape(bsz, seq, d)
```

```python
import functools

import jax
import jax.numpy as jnp
from jax import lax
from jax.experimental import pallas as pl
from jax.experimental.pallas import tpu as pltpu

F32 = jnp.float32
BF16 = jnp.bfloat16
HIGHEST = lax.Precision.HIGHEST

HEAD_DIM = 64
LANES = 128
SUBLANES = 8
EPS = 1e-6
N_MOD = 6
POOL_WINDOWS = (2, 4, 8, 16)
N_GROUPS = 4
EXPERTS_PER_GROUP = 4
N_EXPERTS = N_GROUPS * EXPERTS_PER_GROUP
ROUTER_EXPERT_LANE0 = N_GROUPS
N_PAIRS = 6
N_CLASSES = N_GROUPS * N_PAIRS
PAIR_SLOT_A = (0, 0, 0, 1, 1, 3)
PAIR_SLOT_B = (1, 2, 3, 3, 2, 2)
ROUTE_CLASS_LANE0 = 32
MOE_TILE_ROWS = 256
HGRN_CHUNK = 128
HGRN_LEVELS = (64, 32, 16, 8, 4, 2, 1)
HGRN_GROUP = 8
VMEM_LIMIT = 48 * 1024 * 1024
LOG2_E = 1.4426950408889634
FOX_Q_SCALE = HEAD_DIM ** -0.5 * LOG2_E


def _cparams(sem):
    return pltpu.CompilerParams(dimension_semantics=sem, vmem_limit_bytes=VMEM_LIMIT)


def _silu(x):
    return x * jax.nn.sigmoid(x)


def _iota(shape, dim):
    return lax.broadcasted_iota(jnp.int32, shape, dim)


def _ada_kernel(c_ref, w_ref, b_ref, o_ref):
    sc = _silu(c_ref[...]).astype(BF16)
    o_ref[0] = jnp.dot(sc, w_ref[0].astype(BF16), preferred_element_type=F32) + b_ref[0]


def _ada_call(c, w_ada, b_ada):
    depth, d, n = w_ada.shape
    bsz = c.shape[0]
    tn = 1536
    return pl.pallas_call(
        _ada_kernel,
        out_shape=jax.ShapeDtypeStruct((depth, bsz, n), F32),
        grid=(depth, n // tn),
        in_specs=[
            pl.BlockSpec((bsz, d), lambda l, j: (0, 0)),
            pl.BlockSpec((1, d, tn), lambda l, j: (l, 0, j)),
            pl.BlockSpec((1, 1, tn), lambda l, j: (l, 0, j)),
        ],
        out_specs=pl.BlockSpec((1, bsz, tn), lambda l, j: (l, 0, j)),
        compiler_params=_cparams(("arbitrary", "arbitrary")),
        name="ada_mod",
    )(c, w_ada, b_ada.reshape(depth, 1, n))


def _lb_kernel(x_ref, o_ref):
    x = x_ref[...]
    depth = x.shape[0]
    e = jnp.exp(x - jnp.max(x, axis=0, keepdims=True))
    p = e / jnp.sum(e, axis=0, keepdims=True)
    acc = jnp.zeros_like(p[0:1])
    o_ref[0:1, :] = acc
    for l in range(1, depth):
        acc = acc + p[l:l + 1]
        o_ref[l:l + 1, :] = acc


def _lb_call(lb_logits):
    return pl.pallas_call(
        _lb_kernel,
        out_shape=jax.ShapeDtypeStruct(lb_logits.shape, F32),
        name="hgrn_lower_bounds",
    )(lb_logits)


def _norm_mod(x, g, shift, scale):
    ms = jnp.mean(x * x, axis=-1, keepdims=True)
    return (x * lax.rsqrt(ms + EPS) * g) * (1.0 + scale) + shift


def _inproj_kernel(x_ref, g_ref, mod_ref, wq_ref, wh_ref, wf_ref, qkv_ref, hg_ref, pu_ref, ff_ref, *, n_hg):
    h = _norm_mod(x_ref[0], g_ref[...], mod_ref[0, 0:1, :], mod_ref[0, 1:2, :]).astype(BF16)
    n_qkv = qkv_ref.shape[-1]
    q_scale = jnp.where(_iota((1, n_qkv), 1) < n_qkv // 3, FOX_Q_SCALE, 1.0)
    qkv_ref[0] = (jnp.dot(h, wq_ref[...], preferred_element_type=F32) * q_scale).astype(BF16)
    hg_ref[0] = jnp.dot(h, wh_ref[:, 0:n_hg], preferred_element_type=F32)
    pu_ref[0] = jnp.dot(h, wh_ref[:, n_hg:], preferred_element_type=F32)
    ff_ref[0] = jnp.dot(h, wf_ref[...], preferred_element_type=F32)


def _inproj_call(x, g, mod, w_qkv, w_hp, w_ff, n_hg, tm):
    bsz, seq, d = x.shape
    n_qkv, n_ff = w_qkv.shape[1], w_ff.shape[1]
    n_pu = w_hp.shape[1] - n_hg
    return pl.pallas_call(
        functools.partial(_inproj_kernel, n_hg=n_hg),
        out_shape=(
            jax.ShapeDtypeStruct((bsz, seq, n_qkv), BF16),
            jax.ShapeDtypeStruct((bsz, seq, n_hg), F32),
            jax.ShapeDtypeStruct((bsz, seq, n_pu), F32),
            jax.ShapeDtypeStruct((bsz, seq, n_ff), F32),
        ),
        grid=(bsz, seq // tm),
        in_specs=[
            pl.BlockSpec((1, tm, d), lambda b, i: (b, i, 0)),
            pl.BlockSpec((1, d), lambda b, i: (0, 0)),
            pl.BlockSpec((1, N_MOD, d), lambda b, i: (b, 0, 0)),
            pl.BlockSpec((d, n_qkv), lambda b, i: (0, 0)),
            pl.BlockSpec((d, n_hg + n_pu), lambda b, i: (0, 0)),
            pl.BlockSpec((d, n_ff), lambda b, i: (0, 0)),
        ],
        out_specs=(
            pl.BlockSpec((1, tm, n_qkv), lambda b, i: (b, i, 0)),
            pl.BlockSpec((1, tm, n_hg), lambda b, i: (b, i, 0)),
            pl.BlockSpec((1, tm, n_pu), lambda b, i: (b, i, 0)),
            pl.BlockSpec((1, tm, n_ff), lambda b, i: (b, i, 0)),
        ),
        compiler_params=_cparams(("arbitrary", "arbitrary")),
        name="norm_inproj",
    )(x, g, mod, w_qkv, w_hp, w_ff)


def _fbias_kernel(ff_ref, bias_ref, fcol_ref, *, cb):
    seq = ff_ref.shape[1]
    tri = (_iota((cb, cb), 0) >= _iota((cb, cb), 1)).astype(F32)
    carry = jnp.zeros((1, LANES), F32)
    for blk in range(seq // cb):
        x = ff_ref[0, blk * cb:(blk + 1) * cb, :] + bias_ref[...]
        log_f = jnp.minimum(x, 0.0) - jnp.log1p(jnp.exp(-jnp.abs(x)))
        cs = jnp.dot(tri, log_f, precision=HIGHEST, preferred_element_type=F32) + carry
        fcol_ref[0, blk * cb:(blk + 1) * cb, :] = cs * LOG2_E
        carry = cs[cb - 1:cb, :]


def _fbias_call(ff, bias):
    bsz, seq, _ = ff.shape
    cb = min(256, seq)
    return pl.pallas_call(
        functools.partial(_fbias_kernel, cb=cb),
        out_shape=jax.ShapeDtypeStruct((bsz, seq, LANES), F32),
        grid=(bsz,),
        in_specs=[pl.BlockSpec((1, seq, LANES), lambda b: (b, 0, 0)),
                  pl.BlockSpec((1, LANES), lambda b: (0, 0))],
        out_specs=pl.BlockSpec((1, seq, LANES), lambda b: (b, 0, 0)),
        compiler_params=_cparams(("arbitrary",)),
        name="fox_forget_bias",
    )(ff, bias)


def _split3(f):
    hi = f.astype(BF16).astype(F32)
    rest = f - hi
    mid = rest.astype(BF16).astype(F32)
    return hi, mid, rest - mid


def _fox_kernel(q_ref, k_ref, v_ref, fc_ref, gn_ref, o_ref, k0_sc, k1_sc, v0_sc, v1_sc, *, tq):
    p = pl.program_id(1)
    seq = k_ref.shape[1]
    nb = seq // tq
    lane = _iota((1, LANES), 1)
    h0 = lane < HEAD_DIM
    data = (h0, jnp.logical_not(h0))
    base = (HEAD_DIM, 0)
    k_sc = (k0_sc, k1_sc)
    v_sc = (v0_sc, v1_sc)
    nt = (((1,), (1,)), ((), ()))
    heads = (0, 1)

    def head_f(fc, which):
        return jnp.sum(jnp.where(lane == 2 * p + which, fc, 0.0), axis=-1, keepdims=True)

    def with_bias_lanes(x, which, first3, last3):
        out = jnp.where(data[which], x, 0.0)
        for n in range(3):
            out = jnp.where(lane == base[which] + n, first3[n], out)
            out = jnp.where(lane == base[which] + 3 + n, last3[n], out)
        return out.astype(BF16)

    qs = {}
    for blk in range(nb):
        rows = slice(blk * tq, (blk + 1) * tq)
        fc = fc_ref[0, rows, :]
        qf = q_ref[0, rows, :].astype(F32)
        kf = k_ref[0, rows, :].astype(F32)
        vf = v_ref[0, rows, :].astype(F32)
        for which in heads:
            hi, mid, lo = _split3(head_f(fc, which))
            qs[blk, which] = with_bias_lanes(qf, which, (hi, mid, lo), (1.0, 1.0, 1.0))
            k_sc[which][rows, :] = with_bias_lanes(kf, which, (1.0, 1.0, 1.0), (-hi, -mid, -lo))
            v_sc[which][rows, :] = jnp.where(data[which], vf,
                                             jnp.where(lane == base[which], 1.0, 0.0)).astype(BF16)

    half = tq // 2
    top_mask = _iota((half, half), 0) >= _iota((half, half), 1)
    bottom_mask = _iota((half, tq), 1) <= _iota((half, tq), 0) + half
    m, acc = {}, {}
    for r in range(nb):
        k0 = r * tq
        tasks = {}
        for which in heads:
            tasks[r, which, 0] = (qs[r, which][:half], slice(k0, k0 + half), top_mask, (r, which), slice(0, half))
            tasks[r, which, 1] = (qs[r, which][half:], slice(k0, k0 + tq), bottom_mask, (r, which),
                                  slice(half, tq))
            for i in range(r + 1, nb):
                tasks[i, which] = (qs[i, which], slice(k0, k0 + tq), None, (i, which), slice(0, tq))
        s = {}
        for t, (q_rows, key_rows, mask, _, _) in tasks.items():
            s[t] = lax.dot_general(q_rows, k_sc[t[1]][key_rows, :], nt, preferred_element_type=F32)
            if mask is not None:
                s[t] = jnp.where(mask, s[t], -jnp.inf)
        n = {}
        for t, (_, _, _, prev, rows) in tasks.items():
            n[t] = jnp.max(s[t], axis=-1, keepdims=True)
            if r > 0:
                n[t] = jnp.maximum(m[prev][rows], n[t])
        pv = {}
        for t, (_, key_rows, _, _, _) in tasks.items():
            pr = jnp.exp2(s[t] - n[t]).astype(BF16)
            pv[t] = jnp.dot(pr, v_sc[t[1]][key_rows, :], preferred_element_type=F32)
        new_m, new_acc = {}, {}
        for t, (_, _, _, prev, rows) in tasks.items():
            new_acc[t] = pv[t] if r == 0 else jnp.exp2(m[prev][rows] - n[t]) * acc[prev][rows] + pv[t]
            new_m[t] = n[t]
        m, acc = new_m, new_acc

        for part in range(2):
            a0, a1 = acc[r, 0, part], acc[r, 1, part]
            l0 = jnp.sum(jnp.where(lane == base[0], a0, 0.0), axis=-1, keepdims=True)
            l1 = jnp.sum(jnp.where(lane == base[1], a1, 0.0), axis=-1, keepdims=True)
            o = jnp.where(h0, a0 / l0, a1 / l1)
            o2 = o * o
            ms0 = jnp.sum(jnp.where(h0, o2, 0.0), axis=-1, keepdims=True)
            ms1 = jnp.sum(jnp.where(h0, 0.0, o2), axis=-1, keepdims=True)
            ms = jnp.where(h0, ms0, ms1) * (1.0 / HEAD_DIM)
            rows = slice(k0 + part * half, k0 + (part + 1) * half)
            o_ref[0, rows, :] = (o * lax.rsqrt(ms + EPS) * gn_ref[0]).astype(BF16)


def _fox_call(qkv, fcol, gn, n_pairs, tq):
    bsz, seq, _ = qkv.shape
    return pl.pallas_call(
        functools.partial(_fox_kernel, tq=tq),
        out_shape=jax.ShapeDtypeStruct((bsz, seq, n_pairs * LANES), BF16),
        grid=(bsz, n_pairs),
        in_specs=[
            pl.BlockSpec((1, seq, LANES), lambda b, p: (b, 0, p)),
            pl.BlockSpec((1, seq, LANES), lambda b, p: (b, 0, n_pairs + p)),
            pl.BlockSpec((1, seq, LANES), lambda b, p: (b, 0, 2 * n_pairs + p)),
            pl.BlockSpec((1, seq, LANES), lambda b, p: (b, 0, 0)),
            pl.BlockSpec((1, 1, LANES), lambda b, p: (p, 0, 0)),
        ],
        out_specs=pl.BlockSpec((1, seq, LANES), lambda b, p: (b, 0, p)),
        scratch_shapes=[pltpu.VMEM((seq, LANES), BF16)] * 4,
        compiler_params=_cparams(("arbitrary", "arbitrary")),
        name="fox_attention",
    )(qkv, qkv, qkv, fcol, gn)


def _hgrn_kernel(hq_ref, hf_ref, hi_ref, hg_ref, lb_ref, gn_ref, o_ref, zero_ref):
    ch = HGRN_CHUNK
    seq = hq_ref.shape[1]
    lane = _iota((1, LANES), 1)
    h0 = lane < HEAD_DIM
    r = _iota((ch, ch), 0)
    c = _iota((ch, ch), 1)
    tri = (r >= c).astype(BF16)
    same_head = (r < HEAD_DIM) == (c < HEAD_DIM)
    seg = same_head.astype(BF16)
    r2 = _iota((ch, 2 * ch), 0)
    c2 = _iota((ch, 2 * ch), 1) % ch
    level_masks = [(r2 // (2 * m) == c2 // (2 * m)) & (r2 % (2 * m) >= m) & (c2 % (2 * m) < m)
                   for m in HGRN_LEVELS]
    small_levels = [m for m in HGRN_LEVELS if m < SUBLANES]
    pick = jnp.concatenate([(c == (r // (2 * m)) * (2 * m) + (m - 1)) for m in small_levels], axis=0).astype(BF16)
    lb = lb_ref[0]
    nt = (((1,), (1,)), ((), ()))
    zero_b = jnp.zeros((ch, LANES), BF16)
    zero_ref[...] = jnp.zeros(zero_ref.shape, F32)

    def both_heads(x):
        return jnp.concatenate([jnp.where(h0, x, zero_b), jnp.where(h0, zero_b, x)], axis=0)

    def sum3(x):
        return x[:, 0:LANES] + x[:, LANES:2 * LANES] + x[:, 2 * LANES:]

    def parts3(x):
        return jnp.concatenate([part.astype(BF16) for part in _split3(x)], axis=1)

    def group(gi, state_t):
        n = HGRN_GROUP
        sls = [pl.ds(pl.multiple_of((gi * n + i) * ch, ch), ch) for i in range(n)]
        k, q, v, vb, b, picked = [], [], [], [], [], []
        for sl in sls:
            f = lb + (1.0 - lb) * jax.nn.sigmoid(hf_ref[0, sl, :])
            k.append(1.0 - f)
            q.append(_silu(hq_ref[0, sl, :]))
            v.append(hi_ref[0, sl, :])
            vb.append(v[-1].astype(BF16))
            b.append(sum3(jnp.dot(tri, parts3(jnp.log(f) * LOG2_E), preferred_element_type=F32)))
        for i in range(n):
            picked.append(jnp.dot(pick, parts3(b[i]), preferred_element_type=F32))

        sc = [jnp.zeros((ch, 2 * ch), F32) for _ in range(n)]
        for li, m in enumerate(HGRN_LEVELS):
            for i in range(n):
                if m >= SUBLANES:
                    ref_l = jnp.broadcast_to(b[i].reshape(ch // (2 * m), 2 * m, LANES)[:, m - 1:m, :],
                                             (ch // (2 * m), 2 * m, LANES)).reshape(ch, LANES)
                else:
                    at = small_levels.index(m) * ch
                    ref_l = sum3(picked[i][at:at + ch])
                e = jnp.exp2(-jnp.abs(b[i] - ref_l))
                s = lax.dot_general((q[i] * e).astype(BF16), both_heads((k[i] * e).astype(BF16)), nt,
                                    preferred_element_type=F32)
                sc[i] = jnp.where(level_masks[li], s, sc[i])

        o, upd_t, dec = [], [], []
        for i in range(n):
            oi = jnp.dot(sc[i].astype(BF16), both_heads(vb[i]), preferred_element_type=F32)
            o.append(oi + jnp.dot((q[i] * k[i]).astype(BF16), seg, preferred_element_type=F32) * v[i])
            b_last = b[i][ch - 1:ch, :]
            k_dec = (k[i] * jnp.exp2(b_last - b[i])).astype(BF16)
            upd = jnp.dot(v[i].T.astype(BF16), k_dec, preferred_element_type=F32)
            upd_t.append(jnp.where(same_head, upd, 0.0))
            dec.append(jnp.exp2(b_last))

        states = [state_t]
        for i in range(n):
            states.append(dec[i] * states[i] + upd_t[i])

        for i, sl in enumerate(sls):
            oi = o[i] + lax.dot_general((q[i] * jnp.exp2(b[i])).astype(BF16), states[i].astype(BF16), nt,
                                        preferred_element_type=F32)
            o2 = oi * oi
            ms0 = jnp.sum(jnp.where(h0, o2, 0.0), axis=-1, keepdims=True)
            ms1 = jnp.sum(jnp.where(h0, 0.0, o2), axis=-1, keepdims=True)
            ms = jnp.where(h0, ms0, ms1) * (1.0 / HEAD_DIM)
            out = oi * lax.rsqrt(ms + EPS) * gn_ref[0] * _silu(hg_ref[0, sl, :])
            o_ref[0, sl, :] = out.astype(BF16)
        return states[n]

    lax.fori_loop(0, seq // (HGRN_GROUP * ch), group, jnp.zeros((LANES, LANES), F32))


def _hgrn_call(hg4, lb, gn, n_pairs, zero_rows, zero_width):
    bsz, seq, _ = hg4.shape
    steps = bsz * n_pairs
    zb = next(k * MOE_TILE_ROWS for k in range(1, zero_rows // MOE_TILE_ROWS + 1)
              if zero_rows % (k * MOE_TILE_ROWS) == 0 and zero_rows // (k * MOE_TILE_ROWS) <= steps)
    last_zero_block = zero_rows // zb - 1

    def spec(off):
        return pl.BlockSpec((1, seq, LANES), lambda b, p: (b, 0, off + p))

    return pl.pallas_call(
        _hgrn_kernel,
        out_shape=(jax.ShapeDtypeStruct((bsz, seq, n_pairs * LANES), BF16),
                   jax.ShapeDtypeStruct((zero_rows, zero_width), F32)),
        grid=(bsz, n_pairs),
        in_specs=[spec(0), spec(n_pairs), spec(2 * n_pairs), spec(3 * n_pairs),
                  pl.BlockSpec((1, 1, LANES), lambda b, p: (p, 0, 0)),
                  pl.BlockSpec((1, 1, LANES), lambda b, p: (p, 0, 0))],
        out_specs=(pl.BlockSpec((1, seq, LANES), lambda b, p: (b, 0, p)),
                   pl.BlockSpec((zb, zero_width), lambda b, p: (jnp.minimum(b * n_pairs + p, last_zero_block), 0))),
        compiler_params=_cparams(("arbitrary", "arbitrary")),
        name="hgrn2",
    )(hg4, hg4, hg4, hg4, lb, gn)


def _pool_kernel(u_ref, w_ref, s_ref, o_ref):
    u = u_ref[0]
    seq, n = u.shape
    t = _iota((seq, 1), 0)
    lane = _iota((1, n), 1)

    def shifted(x, k):
        return jnp.where(t >= k, pltpu.roll(x, k, axis=0), 0.0)

    sums = []
    s = u
    for w in POOL_WINDOWS:
        s = s + shifted(s, w // 2)
        sums.append(s)
    pos1 = (t + 1).astype(F32)
    group = len(POOL_WINDOWS) - 1
    mean = sums[group] / jnp.minimum(pos1, float(POOL_WINDOWS[group]))
    group_dim = n // len(POOL_WINDOWS)
    for gi in range(group - 1, -1, -1):
        mean = jnp.where(lane < (gi + 1) * group_dim,
                         sums[gi] / jnp.minimum(pos1, float(POOL_WINDOWS[gi])), mean)
    pooled = (mean - u).astype(BF16)
    o_ref[0] = (jnp.dot(pooled, w_ref[...], preferred_element_type=F32) * s_ref[...]).astype(BF16)


def _pool_call(pu, w_bd, scale):
    bsz, seq, n = pu.shape
    return pl.pallas_call(
        _pool_kernel,
        out_shape=jax.ShapeDtypeStruct((bsz, seq, n), BF16),
        grid=(bsz,),
        in_specs=[pl.BlockSpec((1, seq, n), lambda b: (b, 0, 0)),
                  pl.BlockSpec((n, n), lambda b: (0, 0)),
                  pl.BlockSpec((1, n), lambda b: (0, 0))],
        out_specs=pl.BlockSpec((1, seq, n), lambda b: (b, 0, 0)),
        compiler_params=_cparams(("arbitrary",)),
        name="multiscale_pool",
    )(pu, w_bd, scale)


def _route(logits):
    lane = _iota(logits.shape, 1)
    lane_f = lane.astype(F32)
    big = float(LANES)
    is_g = lane < N_GROUPS
    gl = jnp.where(is_g, logits, -jnp.inf)
    gmax = jnp.max(gl, axis=-1, keepdims=True)
    gsum = jnp.sum(jnp.where(is_g, jnp.exp(gl - gmax), 0.0), axis=-1, keepdims=True)
    p_group = 1.0 / gsum
    g_sel = jnp.min(jnp.where(gl == gmax, lane_f, big), axis=-1, keepdims=True)
    lo = ROUTER_EXPERT_LANE0 + EXPERTS_PER_GROUP * g_sel
    in_group = (lane_f >= lo) & (lane_f < lo + EXPERTS_PER_GROUP)
    el = jnp.where(in_group, logits, -jnp.inf)
    m1 = jnp.max(el, axis=-1, keepdims=True)
    i1 = jnp.min(jnp.where(el == m1, lane_f, big), axis=-1, keepdims=True)
    el2 = jnp.where(lane_f == i1, -jnp.inf, el)
    m2 = jnp.max(el2, axis=-1, keepdims=True)
    i2 = jnp.min(jnp.where(el2 == m2, lane_f, big), axis=-1, keepdims=True)
    e2 = jnp.exp(m2 - m1)
    w1 = p_group / (1.0 + e2)
    w2 = p_group * e2 / (1.0 + e2)
    a = jnp.minimum(i1, i2) - lo
    b = jnp.maximum(i1, i2) - lo
    pair = jnp.where(a == 0.0, b - 1.0, jnp.where(a == 1.0, 6.0 - b, float(N_PAIRS - 1)))
    top1_in_a = (i1 < i2) != (pair == float(N_PAIRS - 1))
    cls_lane = ROUTE_CLASS_LANE0 + g_sel * N_PAIRS + pair
    return (jnp.where(lane == 0, jnp.where(top1_in_a, w1, w2), 0.0)
            + jnp.where(lane == 1, jnp.where(top1_in_a, w2, w1), 0.0)
            + jnp.where(lane_f == cls_lane, 1.0, 0.0))


def _outproj_kernel(of_ref, oh_ref, op_ref, x_ref, w_ref, mod_ref, g_ref, wr_ref, br_ref,
                    xn_ref, h2_ref, *, n_fox, n_hgrn):
    d = x_ref.shape[-1]
    y = jnp.dot(of_ref[...], w_ref[0:n_fox, :], preferred_element_type=F32)
    y = y + jnp.dot(oh_ref[...], w_ref[n_fox:n_fox + n_hgrn, :], preferred_element_type=F32)
    y = y + jnp.dot(op_ref[...], w_ref[n_fox + n_hgrn:, :], preferred_element_type=F32)
    xn = x_ref[...] + (1.0 + mod_ref[0, 2:3, :]) * y
    xn_ref[...] = xn
    h2 = _norm_mod(xn, g_ref[...], mod_ref[0, 3:4, :], mod_ref[0, 4:5, :])
    h2_ref[:, 0:d] = h2
    h_hi = h2.astype(BF16)
    h_lo = (h2 - h_hi.astype(F32)).astype(BF16)
    logits = (jnp.dot(h_hi, wr_ref[0], preferred_element_type=F32)
              + jnp.dot(h_lo, wr_ref[0], preferred_element_type=F32)
              + jnp.dot(h_hi, wr_ref[1], preferred_element_type=F32)) + br_ref[...]
    h2_ref[:, d:] = _route(logits)


def _outproj_call(o_fox, o_hgrn, o_pool, x, w_out, mod, g2, w_router, b_router, tm, seq):
    t, d = x.shape
    n_fox, n_hgrn, n_pool = o_fox.shape[-1], o_hgrn.shape[-1], o_pool.shape[-1]
    per_seq = seq // tm
    return pl.pallas_call(
        functools.partial(_outproj_kernel, n_fox=n_fox, n_hgrn=n_hgrn),
        out_shape=(jax.ShapeDtypeStruct((t, d), F32),
                   jax.ShapeDtypeStruct((t, d + LANES), F32)),
        grid=(t // tm,),
        in_specs=[
            pl.BlockSpec((tm, n_fox), lambda i: (i, 0)),
            pl.BlockSpec((tm, n_hgrn), lambda i: (i, 0)),
            pl.BlockSpec((tm, n_pool), lambda i: (i, 0)),
            pl.BlockSpec((tm, d), lambda i: (i, 0)),
            pl.BlockSpec((d, d), lambda i: (0, 0)),
            pl.BlockSpec((1, N_MOD, d), lambda i: (i // per_seq, 0, 0)),
            pl.BlockSpec((1, d), lambda i: (0, 0)),
            pl.BlockSpec((2, d, LANES), lambda i: (0, 0, 0)),
            pl.BlockSpec((1, LANES), lambda i: (0, 0)),
        ],
        out_specs=(pl.BlockSpec((tm, d), lambda i: (i, 0)),
                   pl.BlockSpec((tm, d + LANES), lambda i: (i, 0))),
        compiler_params=_cparams(("arbitrary",)),
        name="outproj_router",
    )(o_fox, o_hgrn, o_pool, x, w_out, mod, g2, w_router, b_router)


def _sort_kernel(route_ref, pos_ref, meta_ref, *, tile_rows, cb):
    t = route_ref.shape[0]
    lane = _iota((1, LANES), 1)
    is_cls = (lane >= ROUTE_CLASS_LANE0) & (lane < ROUTE_CLASS_LANE0 + N_CLASSES)
    nblk = t // cb
    zero_row = jnp.zeros((1, LANES), F32)

    def onehot(i):
        return jnp.where(is_cls, route_ref[pl.ds(pl.multiple_of(i * cb, cb), cb), :], 0.0)

    counts = lax.fori_loop(0, nblk, lambda i, acc: acc + jnp.sum(onehot(i), axis=0, keepdims=True), zero_row)
    padded = jnp.floor((counts + float(tile_rows - 1)) * (1.0 / tile_rows)) * float(tile_rows)
    before = (_iota((LANES, LANES), 0) < _iota((LANES, LANES), 1)).astype(F32)
    offs = jnp.dot(jnp.broadcast_to(padded, (8, LANES)), before, precision=HIGHEST,
                   preferred_element_type=F32)[0:1]
    ends = offs + padded
    strict = (_iota((cb, cb), 0) > _iota((cb, cb), 1)).astype(BF16)
    ones8 = jnp.ones((8, LANES), F32)
    nt = (((1,), (1,)), ((), ()))

    def place(i, seen):
        oh = onehot(i)
        rank = jnp.dot(strict, oh.astype(BF16), preferred_element_type=F32) + seen
        dest = oh * (rank + offs)
        rows = lax.dot_general(ones8, dest, nt, precision=HIGHEST, preferred_element_type=F32)
        pos_ref[:, pl.ds(pl.multiple_of(i * cb, cb), cb)] = rows.astype(jnp.int32)
        return seen + jnp.sum(oh, axis=0, keepdims=True)

    lax.fori_loop(0, nblk, place, zero_row)
    tile_start = _iota((LANES, LANES), 0).astype(F32) * float(tile_rows)
    tile_cls = jnp.sum(jnp.where(is_cls & (ends <= tile_start), 1.0, 0.0), axis=-1, keepdims=True)
    n_used = jnp.sum(jnp.where(lane == ROUTE_CLASS_LANE0 + N_CLASSES - 1, ends, 0.0), axis=-1,
                     keepdims=True) * (1.0 / tile_rows)
    meta_ref[...] = jnp.where(lane == 0, tile_cls, jnp.where(lane == 1, n_used, 0.0))


def _sort_call(h2ext, d, tile_rows):
    t = h2ext.shape[0]
    cb = min(512, t)
    return pl.pallas_call(
        functools.partial(_sort_kernel, tile_rows=tile_rows, cb=cb),
        out_shape=(jax.ShapeDtypeStruct((8, t), jnp.int32),
                   jax.ShapeDtypeStruct((LANES, LANES), F32)),
        grid=(1,),
        in_specs=[pl.BlockSpec((t, LANES), lambda i: (0, d // LANES))],
        out_specs=(pl.BlockSpec((8, t), lambda i: (0, 0)),
                   pl.BlockSpec((LANES, LANES), lambda i: (0, 0))),
        compiler_params=_cparams(("arbitrary",)),
        name="route_sort",
    )(h2ext)


def _dispatch_kernel(pos_ref, h_ref, init_ref, wg_ref, wu_ref, wd_ref, out_ref, wgb_ref, wub_ref, wdb_ref, sem,
                     *, tm):
    del init_ref
    base = pl.program_id(0) * tm
    for g in range(tm // SUBLANES):
        for u in range(SUBLANES):
            dst = pos_ref[base + g * SUBLANES + u]
            pltpu.make_async_copy(h_ref.at[g, pl.ds(u, 1), :], out_ref.at[pl.ds(dst, 1), :], sem).start()
    wgb_ref[...] = wg_ref[0].astype(BF16)
    wub_ref[...] = wu_ref[0].astype(BF16)
    wdb_ref[...] = wd_ref[0].astype(BF16)
    pltpu.make_async_copy(h_ref, h_ref, sem).wait()


def _dispatch_call(pos, h2ext, zeros, wg, wu, wd, layer, tm):
    t, w = h2ext.shape
    n_rows = zeros.shape[0]
    _, n_e, d, f = wg.shape
    steps = t // tm
    parts = steps // n_e
    assert steps == parts * n_e and d % (parts * SUBLANES) == 0 and f % (parts * SUBLANES) == 0

    def w_in(i, pos):
        return (layer, i // parts, i % parts, 0)

    def w_out(i, pos):
        return (i // parts, i % parts, 0)

    return pl.pallas_call(
        functools.partial(_dispatch_kernel, tm=tm),
        out_shape=(jax.ShapeDtypeStruct((n_rows, w), F32),
                   jax.ShapeDtypeStruct((n_e, d, f), BF16),
                   jax.ShapeDtypeStruct((n_e, d, f), BF16),
                   jax.ShapeDtypeStruct((n_e, f, d), BF16)),
        grid_spec=pltpu.PrefetchScalarGridSpec(
            num_scalar_prefetch=1,
            grid=(steps,),
            in_specs=[pl.BlockSpec((tm // SUBLANES, SUBLANES, w), lambda i, pos: (i, 0, 0)),
                      pl.BlockSpec(memory_space=pl.ANY),
                      pl.BlockSpec((1, 1, d // parts, f), w_in),
                      pl.BlockSpec((1, 1, d // parts, f), w_in),
                      pl.BlockSpec((1, 1, f // parts, d), w_in)],
            out_specs=(pl.BlockSpec(memory_space=pl.ANY),
                       pl.BlockSpec((1, d // parts, f), w_out),
                       pl.BlockSpec((1, d // parts, f), w_out),
                       pl.BlockSpec((1, f // parts, d), w_out)),
            scratch_shapes=[pltpu.SemaphoreType.DMA],
        ),
        input_output_aliases={2: 0},
        compiler_params=_cparams(("arbitrary",)),
        name="moe_dispatch",
    )(pos, h2ext.reshape(t // SUBLANES, SUBLANES, w), zeros, wg, wu, wd)


def _moe_kernel(ea_ref, eb_ref, nu_ref, h_ref, wga_ref, wua_ref, wda_ref, wgb_ref, wub_ref, wdb_ref, o_ref):
    del ea_ref, eb_ref
    d = o_ref.shape[-1]

    @pl.when(pl.program_id(0) < nu_ref[0])
    def _():
        h = h_ref[:, 0:d].astype(BF16)
        tail = h_ref[:, d:]
        lane = _iota((1, LANES), 1)
        w_a = jnp.sum(jnp.where(lane == 0, tail, 0.0), axis=-1, keepdims=True)
        w_b = jnp.sum(jnp.where(lane == 1, tail, 0.0), axis=-1, keepdims=True)

        def expert(wg_ref, wu_ref, wd_ref):
            hidden = _silu(jnp.dot(h, wg_ref[0], preferred_element_type=F32)) * jnp.dot(
                h, wu_ref[0], preferred_element_type=F32)
            return jnp.dot(hidden.astype(BF16), wd_ref[0], preferred_element_type=F32)

        o_ref[...] = w_a * expert(wga_ref, wua_ref, wda_ref) + w_b * expert(wgb_ref, wub_ref, wdb_ref)

    @pl.when(pl.program_id(0) >= nu_ref[0])
    def _():
        o_ref[...] = jnp.zeros_like(o_ref)


def _moe_call(ea, eb, n_used, h_sorted, wg, wu, wd, tile_rows):
    n_rows, w = h_sorted.shape
    _, d, f = wg.shape

    def rows(j, ea, eb, nu):
        return (jnp.maximum(jnp.minimum(j, nu[0] - 1), 0), 0)

    def wa(j, ea, eb, nu):
        return (ea[j], 0, 0)

    def wb(j, ea, eb, nu):
        return (eb[j], 0, 0)

    return pl.pallas_call(
        _moe_kernel,
        out_shape=jax.ShapeDtypeStruct((n_rows, d), F32),
        grid_spec=pltpu.PrefetchScalarGridSpec(
            num_scalar_prefetch=3,
            grid=(n_rows // tile_rows,),
            in_specs=[pl.BlockSpec((tile_rows, w), rows),
                      pl.BlockSpec((1, d, f), wa), pl.BlockSpec((1, d, f), wa), pl.BlockSpec((1, f, d), wa),
                      pl.BlockSpec((1, d, f), wb), pl.BlockSpec((1, d, f), wb), pl.BlockSpec((1, f, d), wb)],
            out_specs=pl.BlockSpec((tile_rows, d), lambda j, ea, eb, nu: (j, 0)),
        ),
        compiler_params=_cparams(("arbitrary",)),
        name="moe_experts",
    )(ea, eb, n_used, h_sorted, wg, wu, wd, wg, wu, wd)


def _combine_kernel(pos_ref, y_ref, x_ref, mod_ref, fg_ref, o_ref, buf, sem, *, tm):
    i = pl.program_id(0)

    def issue(tile, slot):
        def eight(g, carry):
            for u in range(SUBLANES):
                src = pos_ref[tile * tm + g * SUBLANES + u]
                pltpu.make_async_copy(y_ref.at[pl.ds(src, 1), :], buf.at[slot, g, pl.ds(u, 1), :],
                                      sem.at[slot]).start()
            return carry
        lax.fori_loop(0, tm // SUBLANES, eight, 0)

    @pl.when(i == 0)
    def _():
        issue(0, 0)

    @pl.when(i + 1 < pl.num_programs(0))
    def _():
        issue(i + 1, (i + 1) % 2)

    slot = i % 2
    pltpu.make_async_copy(buf.at[slot], buf.at[slot], sem.at[slot]).wait()
    out = x_ref[...] + (1.0 + mod_ref[0, 5:6, :]) * buf[slot].reshape(x_ref.shape)
    ms = jnp.mean(out * out, axis=-1, keepdims=True)
    o_ref[...] = out * lax.rsqrt(ms + EPS) * fg_ref[...]


def _combine_call(pos, y_sorted, x, mod, final_g, tm, seq):
    t, d = x.shape
    per_seq = seq // tm
    return pl.pallas_call(
        functools.partial(_combine_kernel, tm=tm),
        out_shape=jax.ShapeDtypeStruct((t, d), F32),
        grid_spec=pltpu.PrefetchScalarGridSpec(
            num_scalar_prefetch=1,
            grid=(t // tm,),
            in_specs=[pl.BlockSpec(memory_space=pl.ANY),
                      pl.BlockSpec((tm, d), lambda i, pos: (i, 0)),
                      pl.BlockSpec((1, N_MOD, d), lambda i, pos: (i // per_seq, 0, 0)),
                      pl.BlockSpec((1, d), lambda i, pos: (0, 0))],
            out_specs=pl.BlockSpec((tm, d), lambda i, pos: (i, 0)),
            scratch_shapes=[pltpu.VMEM((2, tm // SUBLANES, SUBLANES, d), F32), pltpu.SemaphoreType.DMA((2,))],
        ),
        compiler_params=_cparams(("arbitrary",)),
        name="moe_combine",
    )(pos, y_sorted, x, mod, final_g)


def _combine_inproj_kernel(pos_ref, y_ref, x2_ref, modp_ref, g_ref, mod_ref, wq_ref, wh_ref, wf_ref,
                           x_ref, qkv_ref, hg_ref, pu_ref, ff_ref, buf, sem, *, tm, n_hg):
    i = pl.program_id(0)
    last = pl.num_programs(0) - 1
    groups = tm // SUBLANES

    def row_copy(tile, slot, g, u):
        src = pos_ref[tile * tm + g * SUBLANES + u]
        return pltpu.make_async_copy(y_ref.at[pl.ds(src, 1), :], buf.at[slot, g, pl.ds(u, 1), :], sem.at[slot])

    def wait_slot(slot):
        pltpu.make_async_copy(buf.at[slot], buf.at[slot], sem.at[slot]).wait()

    @pl.when(i == 0)
    def _():
        def eight(g, carry):
            for u in range(SUBLANES):
                row_copy(0, 0, g, u).start()
            return carry
        lax.fori_loop(0, groups, eight, 0)

    slot = i % 2
    wait_slot(slot)
    x = x2_ref[...] + (1.0 + modp_ref[0, 5:6, :]) * buf[slot].reshape(x2_ref.shape)
    x_ref[...] = x
    h = _norm_mod(x, g_ref[...], mod_ref[0, 0:1, :], mod_ref[0, 1:2, :]).astype(BF16)

    nxt = jnp.minimum(i + 1, last)
    for g in range(groups):
        for u in range(SUBLANES):
            row_copy(nxt, 1 - slot, g, u).start()

    n_qkv = qkv_ref.shape[-1]
    q_scale = jnp.where(_iota((1, n_qkv), 1) < n_qkv // 3, FOX_Q_SCALE, 1.0)
    qkv_ref[...] = (jnp.dot(h, wq_ref[...], preferred_element_type=F32) * q_scale).astype(BF16)
    hg_ref[...] = jnp.dot(h, wh_ref[:, 0:n_hg], preferred_element_type=F32)
    pu_ref[...] = jnp.dot(h, wh_ref[:, n_hg:], preferred_element_type=F32)
    ff_ref[...] = jnp.dot(h, wf_ref[...], preferred_element_type=F32)

    @pl.when(i == last)
    def _():
        wait_slot(1 - slot)


def _combine_inproj_call(pos, y_sorted, x2, mod_prev, g, mod, w_qkv, w_hp, w_ff, n_hg, tm, seq):
    t, d = x2.shape
    per_seq = seq // tm
    n_qkv, n_ff = w_qkv.shape[1], w_ff.shape[1]
    n_pu = w_hp.shape[1] - n_hg

    def tile(i, pos):
        return (i, 0)

    def fixed(i, pos):
        return (0, 0)

    def per_batch(i, pos):
        return (i // per_seq, 0, 0)

    return pl.pallas_call(
        functools.partial(_combine_inproj_kernel, tm=tm, n_hg=n_hg),
        out_shape=(
            jax.ShapeDtypeStruct((t, d), F32),
            jax.ShapeDtypeStruct((t, n_qkv), BF16),
            jax.ShapeDtypeStruct((t, n_hg), F32),
            jax.ShapeDtypeStruct((t, n_pu), F32),
            jax.ShapeDtypeStruct((t, n_ff), F32),
        ),
        grid_spec=pltpu.PrefetchScalarGridSpec(
            num_scalar_prefetch=1,
            grid=(t // tm,),
            in_specs=[pl.BlockSpec(memory_space=pl.ANY),
                      pl.BlockSpec((tm, d), tile),
                      pl.BlockSpec((1, N_MOD, d), per_batch),
                      pl.BlockSpec((1, d), fixed),
                      pl.BlockSpec((1, N_MOD, d), per_batch),
                      pl.BlockSpec((d, n_qkv), fixed),
                      pl.BlockSpec((d, n_hg + n_pu), fixed),
                      pl.BlockSpec((d, n_ff), fixed)],
            out_specs=(pl.BlockSpec((tm, d), tile), pl.BlockSpec((tm, n_qkv), tile), pl.BlockSpec((tm, n_hg), tile),
                       pl.BlockSpec((tm, n_pu), tile), pl.BlockSpec((tm, n_ff), tile)),
            scratch_shapes=[pltpu.VMEM((2, tm // SUBLANES, SUBLANES, d), F32), pltpu.SemaphoreType.DMA((2,))],
        ),
        compiler_params=_cparams(("arbitrary",)),
        name="combine_norm_inproj",
    )(pos, y_sorted, x2, mod_prev, g, mod, w_qkv, w_hp, w_ff)


def _pad_lanes(a, n=LANES):
    return jnp.pad(a, [(0, 0)] * (a.ndim - 1) + [(0, n - a.shape[-1])])


def kernel(x, c, w_ada, b_ada, norm1_g, w_in, fox_f_bias, fox_norm_g, hgrn_lb_logits, hgrn_norm_g, pool_w, pool_scale, w_out, norm2_g, router_group_w, router_group_b, router_expert_w, router_expert_b, expert_w_gate, expert_w_up, expert_w_down, final_norm_g):
    bsz, seq, d = x.shape
    depth = w_ada.shape[0]
    fox_heads = fox_f_bias.shape[1]
    fox_dim = fox_heads * HEAD_DIM
    hgrn_dim = hgrn_lb_logits.shape[1]
    pool_dim = pool_scale.shape[1]
    n_fox_pairs = fox_dim // LANES
    n_hgrn_pairs = hgrn_dim // LANES
    t = bsz * seq
    tm = min(512, seq)
    tq = min(512, seq)
    n_tiles = t // MOE_TILE_ROWS + N_CLASSES
    assert n_tiles <= LANES and t % MOE_TILE_ROWS == 0

    o_ff = 3 * fox_dim
    w_qkv = w_in[:, :, :o_ff].astype(BF16)
    w_hp = w_in[:, :, o_ff + fox_heads:].astype(BF16)
    w_ff = _pad_lanes(w_in[:, :, o_ff:o_ff + fox_heads]).astype(BF16)
    f_bias = _pad_lanes(fox_f_bias)
    groups = pool_w.shape[1]
    pool_bd = jnp.einsum('lgcd,gh->lgchd', pool_w, jnp.eye(groups, dtype=pool_w.dtype)).reshape(
        depth, pool_dim, pool_dim).astype(BF16)
    w_router = _pad_lanes(jnp.concatenate([router_group_w, router_expert_w], axis=-1))
    w_router_hi = w_router.astype(BF16)
    w_router = jnp.stack([w_router_hi, (w_router - w_router_hi.astype(F32)).astype(BF16)], axis=1)
    b_router = _pad_lanes(jnp.concatenate([router_group_b, router_expert_b], axis=-1))
    w_out_b = w_out.astype(BF16)
    pair_a = jnp.array(PAIR_SLOT_A, jnp.int32)
    pair_b = jnp.array(PAIR_SLOT_B, jnp.int32)
    final_g = final_norm_g.reshape(1, d)

    mod_all = _ada_call(c, w_ada, b_ada).reshape(depth, bsz, N_MOD, d)
    lower = _lb_call(hgrn_lb_logits)

    pending = None
    for l in range(depth):
        mod = mod_all[l]
        if pending is None:
            qkv, hg4, pu, ff = _inproj_call(x, norm1_g[l:l + 1], mod, w_qkv[l], w_hp[l], w_ff[l], 4 * hgrn_dim, tm)
        else:
            x, qkv, hg4, pu, ff = _combine_inproj_call(*pending, norm1_g[l:l + 1], mod, w_qkv[l], w_hp[l], w_ff[l],
                                                       4 * hgrn_dim, tm, seq)
            x, qkv, hg4, pu, ff = (a.reshape(bsz, seq, -1) for a in (x, qkv, hg4, pu, ff))
        fcol = _fbias_call(ff, f_bias[l:l + 1])
        o_fox = _fox_call(qkv, fcol, fox_norm_g[l].reshape(n_fox_pairs, 1, LANES), n_fox_pairs, tq)
        o_hgrn, zeros = _hgrn_call(hg4, lower[l].reshape(n_hgrn_pairs, 1, LANES),
                                   hgrn_norm_g[l].reshape(n_hgrn_pairs, 1, LANES), n_hgrn_pairs,
                                   n_tiles * MOE_TILE_ROWS, d + LANES)
        o_pool = _pool_call(pu, pool_bd[l], pool_scale[l:l + 1])
        x2, h2ext = _outproj_call(o_fox.reshape(t, -1), o_hgrn.reshape(t, -1), o_pool.reshape(t, -1),
                                  x.reshape(t, d), w_out_b[l], mod, norm2_g[l:l + 1],
                                  w_router[l], b_router[l:l + 1], tm, seq)
        pos8, meta = _sort_call(h2ext, d, MOE_TILE_ROWS)
        pos = pos8[0]
        n_used = meta[0, 1].astype(jnp.int32).reshape(1)
        tile = jnp.minimum(jnp.arange(n_tiles, dtype=jnp.int32), n_used[0] - 1)
        tile_cls = jnp.minimum(meta[:, 0].astype(jnp.int32)[tile], N_CLASSES - 1)
        e0 = (tile_cls // N_PAIRS) * EXPERTS_PER_GROUP
        ea = e0 + pair_a[tile_cls % N_PAIRS]
        eb = e0 + pair_b[tile_cls % N_PAIRS]
        h_sorted, wg_l, wu_l, wd_l = _dispatch_call(pos, h2ext, zeros, expert_w_gate, expert_w_up, expert_w_down,
                                                    l, tm)
        y_sorted = _moe_call(ea, eb, n_used, h_sorted, wg_l, wu_l, wd_l, MOE_TILE_ROWS)
        pending = (pos, y_sorted, x2, mod)
    return _combine_call(*pending, final_g, tm, seq).reshape(bsz, seq, d)
```

```python
import functools

import jax
import jax.numpy as jnp
from jax import lax
from jax.experimental import pallas as pl
from jax.experimental.pallas import tpu as pltpu

F32 = jnp.float32
BF16 = jnp.bfloat16
HIGHEST = lax.Precision.HIGHEST

HEAD_DIM = 64
LANES = 128
SUBLANES = 8
EPS = 1e-6
N_MOD = 6
POOL_WINDOWS = (2, 4, 8, 16)
N_GROUPS = 4
EXPERTS_PER_GROUP = 4
N_EXPERTS = N_GROUPS * EXPERTS_PER_GROUP
ROUTER_EXPERT_LANE0 = N_GROUPS
N_PAIRS = 6
N_CLASSES = N_GROUPS * N_PAIRS
PAIR_SLOT_A = (0, 0, 0, 1, 1, 3)
PAIR_SLOT_B = (1, 2, 3, 3, 2, 2)
ROUTE_CLASS_LANE0 = 32
MOE_TILE_ROWS = 256
HGRN_CHUNK = 128
HGRN_LEVELS = (64, 32, 16, 8, 4, 2, 1)
HGRN_GROUP = 8
VMEM_LIMIT = 48 * 1024 * 1024
LOG2_E = 1.4426950408889634
FOX_Q_SCALE = HEAD_DIM ** -0.5 * LOG2_E


def _cparams(sem):
    return pltpu.CompilerParams(dimension_semantics=sem, vmem_limit_bytes=VMEM_LIMIT)


def _silu(x):
    return x * jax.nn.sigmoid(x)


def _iota(shape, dim):
    return lax.broadcasted_iota(jnp.int32, shape, dim)


def _ada_kernel(c_ref, w_ref, b_ref, o_ref):
    sc = _silu(c_ref[...]).astype(BF16)
    o_ref[0] = jnp.dot(sc, w_ref[0].astype(BF16), preferred_element_type=F32) + b_ref[0]


def _ada_call(c, w_ada, b_ada):
    depth, d, n = w_ada.shape
    bsz = c.shape[0]
    tn = 1536
    return pl.pallas_call(
        _ada_kernel,
        out_shape=jax.ShapeDtypeStruct((depth, bsz, n), F32),
        grid=(depth, n // tn),
        in_specs=[
            pl.BlockSpec((bsz, d), lambda l, j: (0, 0)),
            pl.BlockSpec((1, d, tn), lambda l, j: (l, 0, j)),
            pl.BlockSpec((1, 1, tn), lambda l, j: (l, 0, j)),
        ],
        out_specs=pl.BlockSpec((1, bsz, tn), lambda l, j: (l, 0, j)),
        compiler_params=_cparams(("arbitrary", "arbitrary")),
        name="ada_mod",
    )(c, w_ada, b_ada.reshape(depth, 1, n))


def _lb_kernel(x_ref, o_ref):
    x = x_ref[...]
    depth = x.shape[0]
    e = jnp.exp(x - jnp.max(x, axis=0, keepdims=True))
    p = e / jnp.sum(e, axis=0, keepdims=True)
    acc = jnp.zeros_like(p[0:1])
    o_ref[0:1, :] = acc
    for l in range(1, depth):
        acc = acc + p[l:l + 1]
        o_ref[l:l + 1, :] = acc


def _lb_call(lb_logits):
    return pl.pallas_call(
        _lb_kernel,
        out_shape=jax.ShapeDtypeStruct(lb_logits.shape, F32),
        name="hgrn_lower_bounds",
    )(lb_logits)


def _norm_mod(x, g, shift, scale):
    ms = jnp.mean(x * x, axis=-1, keepdims=True)
    return (x * lax.rsqrt(ms + EPS) * g) * (1.0 + scale) + shift


def _inproj_kernel(x_ref, g_ref, mod_ref, wq_ref, wh_ref, wf_ref, qkv_ref, hg_ref, pu_ref, ff_ref, *, n_hg):
    h = _norm_mod(x_ref[0], g_ref[...], mod_ref[0, 0:1, :], mod_ref[0, 1:2, :]).astype(BF16)
    n_qkv = qkv_ref.shape[-1]
    q_scale = jnp.where(_iota((1, n_qkv), 1) < n_qkv // 3, FOX_Q_SCALE, 1.0)
    qkv_ref[0] = (jnp.dot(h, wq_ref[...], preferred_element_type=F32) * q_scale).astype(BF16)
    hg_ref[0] = jnp.dot(h, wh_ref[:, 0:n_hg], preferred_element_type=F32)
    pu_ref[0] = jnp.dot(h, wh_ref[:, n_hg:], preferred_element_type=F32)
    ff_ref[0] = jnp.dot(h, wf_ref[...], preferred_element_type=F32)


def _inproj_call(x, g, mod, w_qkv, w_hp, w_ff, n_hg, tm):
    bsz, seq, d = x.shape
    n_qkv, n_ff = w_qkv.shape[1], w_ff.shape[1]
    n_pu = w_hp.shape[1] - n_hg
    return pl.pallas_call(
        functools.partial(_inproj_kernel, n_hg=n_hg),
        out_shape=(
            jax.ShapeDtypeStruct((bsz, seq, n_qkv), BF16),
            jax.ShapeDtypeStruct((bsz, seq, n_hg), F32),
            jax.ShapeDtypeStruct((bsz, seq, n_pu), F32),
            jax.ShapeDtypeStruct((bsz, seq, n_ff), F32),
        ),
        grid=(bsz, seq // tm),
        in_specs=[
            pl.BlockSpec((1, tm, d), lambda b, i: (b, i, 0)),
            pl.BlockSpec((1, d), lambda b, i: (0, 0)),
            pl.BlockSpec((1, N_MOD, d), lambda b, i: (b, 0, 0)),
            pl.BlockSpec((d, n_qkv), lambda b, i: (0, 0)),
            pl.BlockSpec((d, n_hg + n_pu), lambda b, i: (0, 0)),
            pl.BlockSpec((d, n_ff), lambda b, i: (0, 0)),
        ],
        out_specs=(
            pl.BlockSpec((1, tm, n_qkv), lambda b, i: (b, i, 0)),
            pl.BlockSpec((1, tm, n_hg), lambda b, i: (b, i, 0)),
            pl.BlockSpec((1, tm, n_pu), lambda b, i: (b, i, 0)),
            pl.BlockSpec((1, tm, n_ff), lambda b, i: (b, i, 0)),
        ),
        compiler_params=_cparams(("arbitrary", "arbitrary")),
        name="norm_inproj",
    )(x, g, mod, w_qkv, w_hp, w_ff)


def _fbias_kernel(ff_ref, bias_ref, fcol_ref, *, cb):
    seq = ff_ref.shape[1]
    tri = (_iota((cb, cb), 0) >= _iota((cb, cb), 1)).astype(F32)
    carry = jnp.zeros((1, LANES), F32)
    for blk in range(seq // cb):
        x = ff_ref[0, blk * cb:(blk + 1) * cb, :] + bias_ref[...]
        log_f = jnp.minimum(x, 0.0) - jnp.log1p(jnp.exp(-jnp.abs(x)))
        cs = jnp.dot(tri, log_f, precision=HIGHEST, preferred_element_type=F32) + carry
        fcol_ref[0, blk * cb:(blk + 1) * cb, :] = cs * LOG2_E
        carry = cs[cb - 1:cb, :]


def _fbias_call(ff, bias):
    bsz, seq, _ = ff.shape
    cb = min(256, seq)
    return pl.pallas_call(
        functools.partial(_fbias_kernel, cb=cb),
        out_shape=jax.ShapeDtypeStruct((bsz, seq, LANES), F32),
        grid=(bsz,),
        in_specs=[pl.BlockSpec((1, seq, LANES), lambda b: (b, 0, 0)),
                  pl.BlockSpec((1, LANES), lambda b: (0, 0))],
        out_specs=pl.BlockSpec((1, seq, LANES), lambda b: (b, 0, 0)),
        compiler_params=_cparams(("arbitrary",)),
        name="fox_forget_bias",
    )(ff, bias)


def _split3(f):
    hi = f.astype(BF16).astype(F32)
    rest = f - hi
    mid = rest.astype(BF16).astype(F32)
    return hi, mid, rest - mid


def _fox_kernel(q_ref, k_ref, v_ref, fc_ref, gn_ref, o_ref, k0_sc, k1_sc, v0_sc, v1_sc, *, tq):
    p = pl.program_id(1)
    seq = k_ref.shape[1]
    nb = seq // tq
    lane = _iota((1, LANES), 1)
    h0 = lane < HEAD_DIM
    data = (h0, jnp.logical_not(h0))
    base = (HEAD_DIM, 0)
    k_sc = (k0_sc, k1_sc)
    v_sc = (v0_sc, v1_sc)
    nt = (((1,), (1,)), ((), ()))
    heads = (0, 1)

    def head_f(fc, which):
        return jnp.sum(jnp.where(lane == 2 * p + which, fc, 0.0), axis=-1, keepdims=True)

    def with_bias_lanes(x, which, first3, last3):
        out = jnp.where(data[which], x, 0.0)
        for n in range(3):
            out = jnp.where(lane == base[which] + n, first3[n], out)
            out = jnp.where(lane == base[which] + 3 + n, last3[n], out)
        return out.astype(BF16)

    qs = {}
    for blk in range(nb):
        rows = slice(blk * tq, (blk + 1) * tq)
        fc = fc_ref[0, rows, :]
        qf = q_ref[0, rows, :].astype(F32)
        kf = k_ref[0, rows, :].astype(F32)
        vf = v_ref[0, rows, :].astype(F32)
        for which in heads:
            hi, mid, lo = _split3(head_f(fc, which))
            qs[blk, which] = with_bias_lanes(qf, which, (hi, mid, lo), (1.0, 1.0, 1.0))
            k_sc[which][rows, :] = with_bias_lanes(kf, which, (1.0, 1.0, 1.0), (-hi, -mid, -lo))
            v_sc[which][rows, :] = jnp.where(data[which], vf,
                                             jnp.where(lane == base[which], 1.0, 0.0)).astype(BF16)

    half = tq // 2
    top_mask = _iota((half, half), 0) >= _iota((half, half), 1)
    bottom_mask = _iota((half, tq), 1) <= _iota((half, tq), 0) + half
    m, acc = {}, {}
    for r in range(nb):
        k0 = r * tq
        tasks = {}
        for which in heads:
            tasks[r, which, 0] = (qs[r, which][:half], slice(k0, k0 + half), top_mask, (r, which), slice(0, half))
            tasks[r, which, 1] = (qs[r, which][half:], slice(k0, k0 + tq), bottom_mask, (r, which),
                                  slice(half, tq))
            for i in range(r + 1, nb):
                tasks[i, which] = (qs[i, which], slice(k0, k0 + tq), None, (i, which), slice(0, tq))
        s = {}
        for t, (q_rows, key_rows, mask, _, _) in tasks.items():
            s[t] = lax.dot_general(q_rows, k_sc[t[1]][key_rows, :], nt, preferred_element_type=F32)
            if mask is not None:
                s[t] = jnp.where(mask, s[t], -jnp.inf)
        n = {}
        for t, (_, _, _, prev, rows) in tasks.items():
            n[t] = jnp.max(s[t], axis=-1, keepdims=True)
            if r > 0:
                n[t] = jnp.maximum(m[prev][rows], n[t])
        pv = {}
        for t, (_, key_rows, _, _, _) in tasks.items():
            pr = jnp.exp2(s[t] - n[t]).astype(BF16)
            pv[t] = jnp.dot(pr, v_sc[t[1]][key_rows, :], preferred_element_type=F32)
        new_m, new_acc = {}, {}
        for t, (_, _, _, prev, rows) in tasks.items():
            new_acc[t] = pv[t] if r == 0 else jnp.exp2(m[prev][rows] - n[t]) * acc[prev][rows] + pv[t]
            new_m[t] = n[t]
        m, acc = new_m, new_acc

        for part in range(2):
            a0, a1 = acc[r, 0, part], acc[r, 1, part]
            l0 = jnp.sum(jnp.where(lane == base[0], a0, 0.0), axis=-1, keepdims=True)
            l1 = jnp.sum(jnp.where(lane == base[1], a1, 0.0), axis=-1, keepdims=True)
            o = jnp.where(h0, a0 / l0, a1 / l1)
            o2 = o * o
            ms0 = jnp.sum(jnp.where(h0, o2, 0.0), axis=-1, keepdims=True)
            ms1 = jnp.sum(jnp.where(h0, 0.0, o2), axis=-1, keepdims=True)
            ms = jnp.where(h0, ms0, ms1) * (1.0 / HEAD_DIM)
            rows = slice(k0 + part * half, k0 + (part + 1) * half)
            o_ref[0, rows, :] = (o * lax.rsqrt(ms + EPS) * gn_ref[0]).astype(BF16)


def _fox_call(qkv, fcol, gn, n_pairs, tq):
    bsz, seq, _ = qkv.shape
    return pl.pallas_call(
        functools.partial(_fox_kernel, tq=tq),
        out_shape=jax.ShapeDtypeStruct((bsz, seq, n_pairs * LANES), BF16),
        grid=(bsz, n_pairs),
        in_specs=[
            pl.BlockSpec((1, seq, LANES), lambda b, p: (b, 0, p)),
            pl.BlockSpec((1, seq, LANES), lambda b, p: (b, 0, n_pairs + p)),
            pl.BlockSpec((1, seq, LANES), lambda b, p: (b, 0, 2 * n_pairs + p)),
            pl.BlockSpec((1, seq, LANES), lambda b, p: (b, 0, 0)),
            pl.BlockSpec((1, 1, LANES), lambda b, p: (p, 0, 0)),
        ],
        out_specs=pl.BlockSpec((1, seq, LANES), lambda b, p: (b, 0, p)),
        scratch_shapes=[pltpu.VMEM((seq, LANES), BF16)] * 4,
        compiler_params=_cparams(("arbitrary", "arbitrary")),
        name="fox_attention",
    )(qkv, qkv, qkv, fcol, gn)


def _hgrn_kernel(hq_ref, hf_ref, hi_ref, hg_ref, lb_ref, gn_ref, wg_ref, wu_ref, wd_ref,
                 o_ref, zero_ref, wgb_ref, wub_ref, wdb_ref):
    ch = HGRN_CHUNK
    seq = hq_ref.shape[1]
    lane = _iota((1, LANES), 1)
    h0 = lane < HEAD_DIM
    r = _iota((ch, ch), 0)
    c = _iota((ch, ch), 1)
    tri = (r >= c).astype(BF16)
    same_head = (r < HEAD_DIM) == (c < HEAD_DIM)
    seg = same_head.astype(BF16)
    r2 = _iota((ch, 2 * ch), 0)
    c2 = _iota((ch, 2 * ch), 1) % ch
    level_masks = [(r2 // (2 * m) == c2 // (2 * m)) & (r2 % (2 * m) >= m) & (c2 % (2 * m) < m)
                   for m in HGRN_LEVELS]
    small_levels = [m for m in HGRN_LEVELS if m < SUBLANES]
    pick = jnp.concatenate([(c == (r // (2 * m)) * (2 * m) + (m - 1)) for m in small_levels], axis=0).astype(BF16)
    lb = lb_ref[0]
    nt = (((1,), (1,)), ((), ()))
    zero_b = jnp.zeros((ch, LANES), BF16)
    zero_ref[...] = jnp.zeros(zero_ref.shape, F32)
    wgb_ref[...] = wg_ref[0].astype(BF16)
    wub_ref[...] = wu_ref[0].astype(BF16)
    wdb_ref[...] = wd_ref[0].astype(BF16)

    def both_heads(x):
        return jnp.concatenate([jnp.where(h0, x, zero_b), jnp.where(h0, zero_b, x)], axis=0)

    def sum3(x):
        return x[:, 0:LANES] + x[:, LANES:2 * LANES] + x[:, 2 * LANES:]

    def parts3(x):
        return jnp.concatenate([part.astype(BF16) for part in _split3(x)], axis=1)

    def group(gi, state_t):
        n = HGRN_GROUP
        sls = [pl.ds(pl.multiple_of((gi * n + i) * ch, ch), ch) for i in range(n)]
        k, q, v, vb, b, picked = [], [], [], [], [], []
        for sl in sls:
            f = lb + (1.0 - lb) * jax.nn.sigmoid(hf_ref[0, sl, :])
            k.append(1.0 - f)
            q.append(_silu(hq_ref[0, sl, :]))
            v.append(hi_ref[0, sl, :])
            vb.append(v[-1].astype(BF16))
            b.append(sum3(jnp.dot(tri, parts3(jnp.log(f) * LOG2_E), preferred_element_type=F32)))
        for i in range(n):
            picked.append(jnp.dot(pick, parts3(b[i]), preferred_element_type=F32))

        sc = [jnp.zeros((ch, 2 * ch), F32) for _ in range(n)]
        for li, m in enumerate(HGRN_LEVELS):
            for i in range(n):
                if m >= SUBLANES:
                    ref_l = jnp.broadcast_to(b[i].reshape(ch // (2 * m), 2 * m, LANES)[:, m - 1:m, :],
                                             (ch // (2 * m), 2 * m, LANES)).reshape(ch, LANES)
                else:
                    at = small_levels.index(m) * ch
                    ref_l = sum3(picked[i][at:at + ch])
                e = jnp.exp2(-jnp.abs(b[i] - ref_l))
                s = lax.dot_general((q[i] * e).astype(BF16), both_heads((k[i] * e).astype(BF16)), nt,
                                    preferred_element_type=F32)
                sc[i] = jnp.where(level_masks[li], s, sc[i])

        o, upd_t, dec = [], [], []
        for i in range(n):
            oi = jnp.dot(sc[i].astype(BF16), both_heads(vb[i]), preferred_element_type=F32)
            o.append(oi + jnp.dot((q[i] * k[i]).astype(BF16), seg, preferred_element_type=F32) * v[i])
            b_last = b[i][ch - 1:ch, :]
            k_dec = (k[i] * jnp.exp2(b_last - b[i])).astype(BF16)
            upd = jnp.dot(v[i].T.astype(BF16), k_dec, preferred_element_type=F32)
            upd_t.append(jnp.where(same_head, upd, 0.0))
            dec.append(jnp.exp2(b_last))

        states = [state_t]
        for i in range(n):
            states.append(dec[i] * states[i] + upd_t[i])

        for i, sl in enumerate(sls):
            oi = o[i] + lax.dot_general((q[i] * jnp.exp2(b[i])).astype(BF16), states[i].astype(BF16), nt,
                                        preferred_element_type=F32)
            o2 = oi * oi
            ms0 = jnp.sum(jnp.where(h0, o2, 0.0), axis=-1, keepdims=True)
            ms1 = jnp.sum(jnp.where(h0, 0.0, o2), axis=-1, keepdims=True)
            ms = jnp.where(h0, ms0, ms1) * (1.0 / HEAD_DIM)
            out = oi * lax.rsqrt(ms + EPS) * gn_ref[0] * _silu(hg_ref[0, sl, :])
            o_ref[0, sl, :] = out.astype(BF16)
        return states[n]

    lax.fori_loop(0, seq // (HGRN_GROUP * ch), group, jnp.zeros((LANES, LANES), F32))


def _hgrn_call(hg4, lb, gn, n_pairs, zero_rows, zero_width, wg, wu, wd, layer):
    bsz, seq, _ = hg4.shape
    _, n_e, d, f = wg.shape
    steps = bsz * n_pairs
    assert steps >= n_e
    zb = next(k * MOE_TILE_ROWS for k in range(1, zero_rows // MOE_TILE_ROWS + 1)
              if zero_rows % (k * MOE_TILE_ROWS) == 0 and zero_rows // (k * MOE_TILE_ROWS) <= steps)
    last_zero_block = zero_rows // zb - 1

    def spec(off):
        return pl.BlockSpec((1, seq, LANES), lambda b, p: (b, 0, off + p))

    def expert_in(b, p):
        return (layer, jnp.minimum(b * n_pairs + p, n_e - 1), 0, 0)

    def expert_out(b, p):
        return (jnp.minimum(b * n_pairs + p, n_e - 1), 0, 0)

    return pl.pallas_call(
        _hgrn_kernel,
        out_shape=(jax.ShapeDtypeStruct((bsz, seq, n_pairs * LANES), BF16),
                   jax.ShapeDtypeStruct((zero_rows, zero_width), F32),
                   jax.ShapeDtypeStruct((n_e, d, f), BF16),
                   jax.ShapeDtypeStruct((n_e, d, f), BF16),
                   jax.ShapeDtypeStruct((n_e, f, d), BF16)),
        grid=(bsz, n_pairs),
        in_specs=[spec(0), spec(n_pairs), spec(2 * n_pairs), spec(3 * n_pairs),
                  pl.BlockSpec((1, 1, LANES), lambda b, p: (p, 0, 0)),
                  pl.BlockSpec((1, 1, LANES), lambda b, p: (p, 0, 0)),
                  pl.BlockSpec((1, 1, d, f), expert_in), pl.BlockSpec((1, 1, d, f), expert_in),
                  pl.BlockSpec((1, 1, f, d), expert_in)],
        out_specs=(pl.BlockSpec((1, seq, LANES), lambda b, p: (b, 0, p)),
                   pl.BlockSpec((zb, zero_width), lambda b, p: (jnp.minimum(b * n_pairs + p, last_zero_block), 0)),
                   pl.BlockSpec((1, d, f), expert_out), pl.BlockSpec((1, d, f), expert_out),
                   pl.BlockSpec((1, f, d), expert_out)),
        compiler_params=_cparams(("arbitrary", "arbitrary")),
        name="hgrn2",
    )(hg4, hg4, hg4, hg4, lb, gn, wg, wu, wd)


def _pool_kernel(u_ref, w_ref, s_ref, o_ref):
    u = u_ref[0]
    seq, n = u.shape
    t = _iota((seq, 1), 0)
    lane = _iota((1, n), 1)

    def shifted(x, k):
        return jnp.where(t >= k, pltpu.roll(x, k, axis=0), 0.0)

    sums = []
    s = u
    for w in POOL_WINDOWS:
        s = s + shifted(s, w // 2)
        sums.append(s)
    pos1 = (t + 1).astype(F32)
    group = len(POOL_WINDOWS) - 1
    mean = sums[group] / jnp.minimum(pos1, float(POOL_WINDOWS[group]))
    group_dim = n // len(POOL_WINDOWS)
    for gi in range(group - 1, -1, -1):
        mean = jnp.where(lane < (gi + 1) * group_dim,
                         sums[gi] / jnp.minimum(pos1, float(POOL_WINDOWS[gi])), mean)
    pooled = (mean - u).astype(BF16)
    o_ref[0] = (jnp.dot(pooled, w_ref[...], preferred_element_type=F32) * s_ref[...]).astype(BF16)


def _pool_call(pu, w_bd, scale):
    bsz, seq, n = pu.shape
    return pl.pallas_call(
        _pool_kernel,
        out_shape=jax.ShapeDtypeStruct((bsz, seq, n), BF16),
        grid=(bsz,),
        in_specs=[pl.BlockSpec((1, seq, n), lambda b: (b, 0, 0)),
                  pl.BlockSpec((n, n), lambda b: (0, 0)),
                  pl.BlockSpec((1, n), lambda b: (0, 0))],
        out_specs=pl.BlockSpec((1, seq, n), lambda b: (b, 0, 0)),
        compiler_params=_cparams(("arbitrary",)),
        name="multiscale_pool",
    )(pu, w_bd, scale)


def _route(logits):
    lane = _iota(logits.shape, 1)
    lane_f = lane.astype(F32)
    big = float(LANES)
    is_g = lane < N_GROUPS
    gl = jnp.where(is_g, logits, -jnp.inf)
    gmax = jnp.max(gl, axis=-1, keepdims=True)
    gsum = jnp.sum(jnp.where(is_g, jnp.exp(gl - gmax), 0.0), axis=-1, keepdims=True)
    p_group = 1.0 / gsum
    g_sel = jnp.min(jnp.where(gl == gmax, lane_f, big), axis=-1, keepdims=True)
    lo = ROUTER_EXPERT_LANE0 + EXPERTS_PER_GROUP * g_sel
    in_group = (lane_f >= lo) & (lane_f < lo + EXPERTS_PER_GROUP)
    el = jnp.where(in_group, logits, -jnp.inf)
    m1 = jnp.max(el, axis=-1, keepdims=True)
    i1 = jnp.min(jnp.where(el == m1, lane_f, big), axis=-1, keepdims=True)
    el2 = jnp.where(lane_f == i1, -jnp.inf, el)
    m2 = jnp.max(el2, axis=-1, keepdims=True)
    i2 = jnp.min(jnp.where(el2 == m2, lane_f, big), axis=-1, keepdims=True)
    e2 = jnp.exp(m2 - m1)
    w1 = p_group / (1.0 + e2)
    w2 = p_group * e2 / (1.0 + e2)
    a = jnp.minimum(i1, i2) - lo
    b = jnp.maximum(i1, i2) - lo
    pair = jnp.where(a == 0.0, b - 1.0, jnp.where(a == 1.0, 6.0 - b, float(N_PAIRS - 1)))
    top1_in_a = (i1 < i2) != (pair == float(N_PAIRS - 1))
    cls_lane = ROUTE_CLASS_LANE0 + g_sel * N_PAIRS + pair
    return (jnp.where(lane == 0, jnp.where(top1_in_a, w1, w2), 0.0)
            + jnp.where(lane == 1, jnp.where(top1_in_a, w2, w1), 0.0)
            + jnp.where(lane_f == cls_lane, 1.0, 0.0))


def _outproj_kernel(of_ref, oh_ref, op_ref, x_ref, w_ref, mod_ref, g_ref, wr_ref, br_ref,
                    xn_ref, h2_ref, *, n_fox, n_hgrn):
    d = x_ref.shape[-1]
    y = jnp.dot(of_ref[...], w_ref[0:n_fox, :], preferred_element_type=F32)
    y = y + jnp.dot(oh_ref[...], w_ref[n_fox:n_fox + n_hgrn, :], preferred_element_type=F32)
    y = y + jnp.dot(op_ref[...], w_ref[n_fox + n_hgrn:, :], preferred_element_type=F32)
    xn = x_ref[...] + (1.0 + mod_ref[0, 2:3, :]) * y
    xn_ref[...] = xn
    h2 = _norm_mod(xn, g_ref[...], mod_ref[0, 3:4, :], mod_ref[0, 4:5, :])
    h2_ref[:, 0:d] = h2
    h_hi = h2.astype(BF16)
    h_lo = (h2 - h_hi.astype(F32)).astype(BF16)
    logits = (jnp.dot(h_hi, wr_ref[0], preferred_element_type=F32)
              + jnp.dot(h_lo, wr_ref[0], preferred_element_type=F32)
              + jnp.dot(h_hi, wr_ref[1], preferred_element_type=F32)) + br_ref[...]
    h2_ref[:, d:] = _route(logits)


def _outproj_call(o_fox, o_hgrn, o_pool, x, w_out, mod, g2, w_router, b_router, tm, seq):
    t, d = x.shape
    n_fox, n_hgrn, n_pool = o_fox.shape[-1], o_hgrn.shape[-1], o_pool.shape[-1]
    per_seq = seq // tm
    return pl.pallas_call(
        functools.partial(_outproj_kernel, n_fox=n_fox, n_hgrn=n_hgrn),
        out_shape=(jax.ShapeDtypeStruct((t, d), F32),
                   jax.ShapeDtypeStruct((t, d + LANES), F32)),
        grid=(t // tm,),
        in_specs=[
            pl.BlockSpec((tm, n_fox), lambda i: (i, 0)),
            pl.BlockSpec((tm, n_hgrn), lambda i: (i, 0)),
            pl.BlockSpec((tm, n_pool), lambda i: (i, 0)),
            pl.BlockSpec((tm, d), lambda i: (i, 0)),
            pl.BlockSpec((d, d), lambda i: (0, 0)),
            pl.BlockSpec((1, N_MOD, d), lambda i: (i // per_seq, 0, 0)),
            pl.BlockSpec((1, d), lambda i: (0, 0)),
            pl.BlockSpec((2, d, LANES), lambda i: (0, 0, 0)),
            pl.BlockSpec((1, LANES), lambda i: (0, 0)),
        ],
        out_specs=(pl.BlockSpec((tm, d), lambda i: (i, 0)),
                   pl.BlockSpec((tm, d + LANES), lambda i: (i, 0))),
        compiler_params=_cparams(("arbitrary",)),
        name="outproj_router",
    )(o_fox, o_hgrn, o_pool, x, w_out, mod, g2, w_router, b_router)


def _sort_kernel(route_ref, pos_ref, meta_ref, *, tile_rows, cb):
    t = route_ref.shape[0]
    lane = _iota((1, LANES), 1)
    is_cls = (lane >= ROUTE_CLASS_LANE0) & (lane < ROUTE_CLASS_LANE0 + N_CLASSES)
    nblk = t // cb
    zero_row = jnp.zeros((1, LANES), F32)

    def onehot(i):
        return jnp.where(is_cls, route_ref[pl.ds(pl.multiple_of(i * cb, cb), cb), :], 0.0)

    counts = lax.fori_loop(0, nblk, lambda i, acc: acc + jnp.sum(onehot(i), axis=0, keepdims=True), zero_row)
    padded = jnp.floor((counts + float(tile_rows - 1)) * (1.0 / tile_rows)) * float(tile_rows)
    before = (_iota((LANES, LANES), 0) < _iota((LANES, LANES), 1)).astype(F32)
    offs = jnp.dot(jnp.broadcast_to(padded, (8, LANES)), before, precision=HIGHEST,
                   preferred_element_type=F32)[0:1]
    ends = offs + padded
    strict = (_iota((cb, cb), 0) > _iota((cb, cb), 1)).astype(BF16)
    ones8 = jnp.ones((8, LANES), F32)
    nt = (((1,), (1,)), ((), ()))

    def place(i, seen):
        oh = onehot(i)
        rank = jnp.dot(strict, oh.astype(BF16), preferred_element_type=F32) + seen
        dest = oh * (rank + offs)
        rows = lax.dot_general(ones8, dest, nt, precision=HIGHEST, preferred_element_type=F32)
        pos_ref[:, pl.ds(pl.multiple_of(i * cb, cb), cb)] = rows.astype(jnp.int32)
        return seen + jnp.sum(oh, axis=0, keepdims=True)

    lax.fori_loop(0, nblk, place, zero_row)
    tile_start = _iota((LANES, LANES), 0).astype(F32) * float(tile_rows)
    tile_cls = jnp.sum(jnp.where(is_cls & (ends <= tile_start), 1.0, 0.0), axis=-1, keepdims=True)
    n_used = jnp.sum(jnp.where(lane == ROUTE_CLASS_LANE0 + N_CLASSES - 1, ends, 0.0), axis=-1,
                     keepdims=True) * (1.0 / tile_rows)
    meta_ref[...] = jnp.where(lane == 0, tile_cls, jnp.where(lane == 1, n_used, 0.0))


def _sort_call(h2ext, d, tile_rows):
    t = h2ext.shape[0]
    cb = min(512, t)
    return pl.pallas_call(
        functools.partial(_sort_kernel, tile_rows=tile_rows, cb=cb),
        out_shape=(jax.ShapeDtypeStruct((8, t), jnp.int32),
                   jax.ShapeDtypeStruct((LANES, LANES), F32)),
        grid=(1,),
        in_specs=[pl.BlockSpec((t, LANES), lambda i: (0, d // LANES))],
        out_specs=(pl.BlockSpec((8, t), lambda i: (0, 0)),
                   pl.BlockSpec((LANES, LANES), lambda i: (0, 0))),
        compiler_params=_cparams(("arbitrary",)),
        name="route_sort",
    )(h2ext)


def _dispatch_kernel(pos_ref, h_ref, init_ref, out_ref, sem, *, tm):
    del init_ref
    base = pl.program_id(0) * tm
    for g in range(tm // SUBLANES):
        for u in range(SUBLANES):
            dst = pos_ref[base + g * SUBLANES + u]
            pltpu.make_async_copy(h_ref.at[g, pl.ds(u, 1), :], out_ref.at[pl.ds(dst, 1), :], sem).start()
    pltpu.make_async_copy(h_ref, h_ref, sem).wait()


def _dispatch_call(pos, h2ext, zeros, tm):
    t, w = h2ext.shape
    return pl.pallas_call(
        functools.partial(_dispatch_kernel, tm=tm),
        out_shape=jax.ShapeDtypeStruct(zeros.shape, F32),
        grid_spec=pltpu.PrefetchScalarGridSpec(
            num_scalar_prefetch=1,
            grid=(t // tm,),
            in_specs=[pl.BlockSpec((tm // SUBLANES, SUBLANES, w), lambda i, pos: (i, 0, 0)),
                      pl.BlockSpec(memory_space=pl.ANY)],
            out_specs=pl.BlockSpec(memory_space=pl.ANY),
            scratch_shapes=[pltpu.SemaphoreType.DMA],
        ),
        input_output_aliases={2: 0},
        compiler_params=_cparams(("arbitrary",)),
        name="moe_dispatch",
    )(pos, h2ext.reshape(t // SUBLANES, SUBLANES, w), zeros)


def _moe_kernel(ea_ref, eb_ref, nu_ref, h_ref, wga_ref, wua_ref, wda_ref, wgb_ref, wub_ref, wdb_ref, o_ref):
    del ea_ref, eb_ref
    d = o_ref.shape[-1]

    @pl.when(pl.program_id(0) < nu_ref[0])
    def _():
        h = h_ref[:, 0:d].astype(BF16)
        tail = h_ref[:, d:]
        lane = _iota((1, LANES), 1)
        w_a = jnp.sum(jnp.where(lane == 0, tail, 0.0), axis=-1, keepdims=True)
        w_b = jnp.sum(jnp.where(lane == 1, tail, 0.0), axis=-1, keepdims=True)

        def expert(wg_ref, wu_ref, wd_ref):
            hidden = _silu(jnp.dot(h, wg_ref[0], preferred_element_type=F32)) * jnp.dot(
                h, wu_ref[0], preferred_element_type=F32)
            return jnp.dot(hidden.astype(BF16), wd_ref[0], preferred_element_type=F32)

        o_ref[...] = w_a * expert(wga_ref, wua_ref, wda_ref) + w_b * expert(wgb_ref, wub_ref, wdb_ref)

    @pl.when(pl.program_id(0) >= nu_ref[0])
    def _():
        o_ref[...] = jnp.zeros_like(o_ref)


def _moe_call(ea, eb, n_used, h_sorted, wg, wu, wd, tile_rows):
    n_rows, w = h_sorted.shape
    _, d, f = wg.shape

    def rows(j, ea, eb, nu):
        return (jnp.maximum(jnp.minimum(j, nu[0] - 1), 0), 0)

    def wa(j, ea, eb, nu):
        return (ea[j], 0, 0)

    def wb(j, ea, eb, nu):
        return (eb[j], 0, 0)

    return pl.pallas_call(
        _moe_kernel,
        out_shape=jax.ShapeDtypeStruct((n_rows, d), F32),
        grid_spec=pltpu.PrefetchScalarGridSpec(
            num_scalar_prefetch=3,
            grid=(n_rows // tile_rows,),
            in_specs=[pl.BlockSpec((tile_rows, w), rows),
                      pl.BlockSpec((1, d, f), wa), pl.BlockSpec((1, d, f), wa), pl.BlockSpec((1, f, d), wa),
                      pl.BlockSpec((1, d, f), wb), pl.BlockSpec((1, d, f), wb), pl.BlockSpec((1, f, d), wb)],
            out_specs=pl.BlockSpec((tile_rows, d), lambda j, ea, eb, nu: (j, 0)),
        ),
        compiler_params=_cparams(("arbitrary",)),
        name="moe_experts",
    )(ea, eb, n_used, h_sorted, wg, wu, wd, wg, wu, wd)


def _combine_kernel(pos_ref, y_ref, x_ref, mod_ref, fg_ref, o_ref, buf, sem, *, tm):
    i = pl.program_id(0)

    def issue(tile, slot):
        def eight(g, carry):
            for u in range(SUBLANES):
                src = pos_ref[tile * tm + g * SUBLANES + u]
                pltpu.make_async_copy(y_ref.at[pl.ds(src, 1), :], buf.at[slot, g, pl.ds(u, 1), :],
                                      sem.at[slot]).start()
            return carry
        lax.fori_loop(0, tm // SUBLANES, eight, 0)

    @pl.when(i == 0)
    def _():
        issue(0, 0)

    @pl.when(i + 1 < pl.num_programs(0))
    def _():
        issue(i + 1, (i + 1) % 2)

    slot = i % 2
    pltpu.make_async_copy(buf.at[slot], buf.at[slot], sem.at[slot]).wait()
    out = x_ref[...] + (1.0 + mod_ref[0, 5:6, :]) * buf[slot].reshape(x_ref.shape)
    ms = jnp.mean(out * out, axis=-1, keepdims=True)
    o_ref[...] = out * lax.rsqrt(ms + EPS) * fg_ref[...]


def _combine_call(pos, y_sorted, x, mod, final_g, tm, seq):
    t, d = x.shape
    per_seq = seq // tm
    return pl.pallas_call(
        functools.partial(_combine_kernel, tm=tm),
        out_shape=jax.ShapeDtypeStruct((t, d), F32),
        grid_spec=pltpu.PrefetchScalarGridSpec(
            num_scalar_prefetch=1,
            grid=(t // tm,),
            in_specs=[pl.BlockSpec(memory_space=pl.ANY),
                      pl.BlockSpec((tm, d), lambda i, pos: (i, 0)),
                      pl.BlockSpec((1, N_MOD, d), lambda i, pos: (i // per_seq, 0, 0)),
                      pl.BlockSpec((1, d), lambda i, pos: (0, 0))],
            out_specs=pl.BlockSpec((tm, d), lambda i, pos: (i, 0)),
            scratch_shapes=[pltpu.VMEM((2, tm // SUBLANES, SUBLANES, d), F32), pltpu.SemaphoreType.DMA((2,))],
        ),
        compiler_params=_cparams(("arbitrary",)),
        name="moe_combine",
    )(pos, y_sorted, x, mod, final_g)


def _combine_inproj_kernel(pos_ref, y_ref, x2_ref, modp_ref, g_ref, mod_ref, wq_ref, wh_ref, wf_ref,
                           x_ref, qkv_ref, hg_ref, pu_ref, ff_ref, buf, sem, *, tm, n_hg):
    i = pl.program_id(0)
    last = pl.num_programs(0) - 1
    groups = tm // SUBLANES

    def row_copy(tile, slot, g, u):
        src = pos_ref[tile * tm + g * SUBLANES + u]
        return pltpu.make_async_copy(y_ref.at[pl.ds(src, 1), :], buf.at[slot, g, pl.ds(u, 1), :], sem.at[slot])

    def wait_slot(slot):
        pltpu.make_async_copy(buf.at[slot], buf.at[slot], sem.at[slot]).wait()

    @pl.when(i == 0)
    def _():
        def eight(g, carry):
            for u in range(SUBLANES):
                row_copy(0, 0, g, u).start()
            return carry
        lax.fori_loop(0, groups, eight, 0)

    slot = i % 2
    wait_slot(slot)
    x = x2_ref[...] + (1.0 + modp_ref[0, 5:6, :]) * buf[slot].reshape(x2_ref.shape)
    x_ref[...] = x
    h = _norm_mod(x, g_ref[...], mod_ref[0, 0:1, :], mod_ref[0, 1:2, :]).astype(BF16)

    nxt = jnp.minimum(i + 1, last)
    for g in range(groups):
        for u in range(SUBLANES):
            row_copy(nxt, 1 - slot, g, u).start()

    n_qkv = qkv_ref.shape[-1]
    q_scale = jnp.where(_iota((1, n_qkv), 1) < n_qkv // 3, FOX_Q_SCALE, 1.0)
    qkv_ref[...] = (jnp.dot(h, wq_ref[...], preferred_element_type=F32) * q_scale).astype(BF16)
    hg_ref[...] = jnp.dot(h, wh_ref[:, 0:n_hg], preferred_element_type=F32)
    pu_ref[...] = jnp.dot(h, wh_ref[:, n_hg:], preferred_element_type=F32)
    ff_ref[...] = jnp.dot(h, wf_ref[...], preferred_element_type=F32)

    @pl.when(i == last)
    def _():
        wait_slot(1 - slot)


def _combine_inproj_call(pos, y_sorted, x2, mod_prev, g, mod, w_qkv, w_hp, w_ff, n_hg, tm, seq):
    t, d = x2.shape
    per_seq = seq // tm
    n_qkv, n_ff = w_qkv.shape[1], w_ff.shape[1]
    n_pu = w_hp.shape[1] - n_hg

    def tile(i, pos):
        return (i, 0)

    def fixed(i, pos):
        return (0, 0)

    def per_batch(i, pos):
        return (i // per_seq, 0, 0)

    return pl.pallas_call(
        functools.partial(_combine_inproj_kernel, tm=tm, n_hg=n_hg),
        out_shape=(
            jax.ShapeDtypeStruct((t, d), F32),
            jax.ShapeDtypeStruct((t, n_qkv), BF16),
            jax.ShapeDtypeStruct((t, n_hg), F32),
            jax.ShapeDtypeStruct((t, n_pu), F32),
            jax.ShapeDtypeStruct((t, n_ff), F32),
        ),
        grid_spec=pltpu.PrefetchScalarGridSpec(
            num_scalar_prefetch=1,
            grid=(t // tm,),
            in_specs=[pl.BlockSpec(memory_space=pl.ANY),
                      pl.BlockSpec((tm, d), tile),
                      pl.BlockSpec((1, N_MOD, d), per_batch),
                      pl.BlockSpec((1, d), fixed),
                      pl.BlockSpec((1, N_MOD, d), per_batch),
                      pl.BlockSpec((d, n_qkv), fixed),
                      pl.BlockSpec((d, n_hg + n_pu), fixed),
                      pl.BlockSpec((d, n_ff), fixed)],
            out_specs=(pl.BlockSpec((tm, d), tile), pl.BlockSpec((tm, n_qkv), tile), pl.BlockSpec((tm, n_hg), tile),
                       pl.BlockSpec((tm, n_pu), tile), pl.BlockSpec((tm, n_ff), tile)),
            scratch_shapes=[pltpu.VMEM((2, tm // SUBLANES, SUBLANES, d), F32), pltpu.SemaphoreType.DMA((2,))],
        ),
        compiler_params=_cparams(("arbitrary",)),
        name="combine_norm_inproj",
    )(pos, y_sorted, x2, mod_prev, g, mod, w_qkv, w_hp, w_ff)


def _pad_lanes(a, n=LANES):
    return jnp.pad(a, [(0, 0)] * (a.ndim - 1) + [(0, n - a.shape[-1])])


def kernel(x, c, w_ada, b_ada, norm1_g, w_in, fox_f_bias, fox_norm_g, hgrn_lb_logits, hgrn_norm_g, pool_w, pool_scale, w_out, norm2_g, router_group_w, router_group_b, router_expert_w, router_expert_b, expert_w_gate, expert_w_up, expert_w_down, final_norm_g):
    bsz, seq, d = x.shape
    depth = w_ada.shape[0]
    fox_heads = fox_f_bias.shape[1]
    fox_dim = fox_heads * HEAD_DIM
    hgrn_dim = hgrn_lb_logits.shape[1]
    pool_dim = pool_scale.shape[1]
    n_fox_pairs = fox_dim // LANES
    n_hgrn_pairs = hgrn_dim // LANES
    t = bsz * seq
    tm = min(512, seq)
    tq = min(512, seq)
    n_tiles = t // MOE_TILE_ROWS + N_CLASSES
    assert n_tiles <= LANES and t % MOE_TILE_ROWS == 0

    o_ff = 3 * fox_dim
    w_qkv = w_in[:, :, :o_ff].astype(BF16)
    w_hp = w_in[:, :, o_ff + fox_heads:].astype(BF16)
    w_ff = _pad_lanes(w_in[:, :, o_ff:o_ff + fox_heads]).astype(BF16)
    f_bias = _pad_lanes(fox_f_bias)
    groups = pool_w.shape[1]
    pool_bd = jnp.einsum('lgcd,gh->lgchd', pool_w, jnp.eye(groups, dtype=pool_w.dtype)).reshape(
        depth, pool_dim, pool_dim).astype(BF16)
    w_router = _pad_lanes(jnp.concatenate([router_group_w, router_expert_w], axis=-1))
    w_router_hi = w_router.astype(BF16)
    w_router = jnp.stack([w_router_hi, (w_router - w_router_hi.astype(F32)).astype(BF16)], axis=1)
    b_router = _pad_lanes(jnp.concatenate([router_group_b, router_expert_b], axis=-1))
    w_out_b = w_out.astype(BF16)
    pair_a = jnp.array(PAIR_SLOT_A, jnp.int32)
    pair_b = jnp.array(PAIR_SLOT_B, jnp.int32)
    final_g = final_norm_g.reshape(1, d)

    mod_all = _ada_call(c, w_ada, b_ada).reshape(depth, bsz, N_MOD, d)
    lower = _lb_call(hgrn_lb_logits)

    pending = None
    for l in range(depth):
        mod = mod_all[l]
        if pending is None:
            qkv, hg4, pu, ff = _inproj_call(x, norm1_g[l:l + 1], mod, w_qkv[l], w_hp[l], w_ff[l], 4 * hgrn_dim, tm)
        else:
            x, qkv, hg4, pu, ff = _combine_inproj_call(*pending, norm1_g[l:l + 1], mod, w_qkv[l], w_hp[l], w_ff[l],
                                                       4 * hgrn_dim, tm, seq)
            x, qkv, hg4, pu, ff = (a.reshape(bsz, seq, -1) for a in (x, qkv, hg4, pu, ff))
        fcol = _fbias_call(ff, f_bias[l:l + 1])
        o_fox = _fox_call(qkv, fcol, fox_norm_g[l].reshape(n_fox_pairs, 1, LANES), n_fox_pairs, tq)
        o_hgrn, zeros, wg_l, wu_l, wd_l = _hgrn_call(
            hg4, lower[l].reshape(n_hgrn_pairs, 1, LANES), hgrn_norm_g[l].reshape(n_hgrn_pairs, 1, LANES),
            n_hgrn_pairs, n_tiles * MOE_TILE_ROWS, d + LANES, expert_w_gate, expert_w_up, expert_w_down, l)
        o_pool = _pool_call(pu, pool_bd[l], pool_scale[l:l + 1])
        x2, h2ext = _outproj_call(o_fox.reshape(t, -1), o_hgrn.reshape(t, -1), o_pool.reshape(t, -1),
                                  x.reshape(t, d), w_out_b[l], mod, norm2_g[l:l + 1],
                                  w_router[l], b_router[l:l + 1], tm, seq)
        pos8, meta = _sort_call(h2ext, d, MOE_TILE_ROWS)
        pos = pos8[0]
        n_used = meta[0, 1].astype(jnp.int32).reshape(1)
        tile = jnp.minimum(jnp.arange(n_tiles, dtype=jnp.int32), n_used[0] - 1)
        tile_cls = jnp.minimum(meta[:, 0].astype(jnp.int32)[tile], N_CLASSES - 1)
        e0 = (tile_cls // N_PAIRS) * EXPERTS_PER_GROUP
        ea = e0 + pair_a[tile_cls % N_PAIRS]
        eb = e0 + pair_b[tile_cls % N_PAIRS]
        h_sorted = _dispatch_call(pos, h2ext, zeros, tm)
        y_sorted = _moe_call(ea, eb, n_used, h_sorted, wg_l, wu_l, wd_l, MOE_TILE_ROWS)
        pending = (pos, y_sorted, x2, mod)
    return _combine_call(*pending, final_g, tm, seq).reshape(bsz, seq, d)
```

```python
import functools

import jax
import jax.numpy as jnp
from jax import lax
from jax.experimental import pallas as pl
from jax.experimental.pallas import tpu as pltpu

F32 = jnp.float32
BF16 = jnp.bfloat16
HIGHEST = lax.Precision.HIGHEST

HEAD_DIM = 64
LANES = 128
SUBLANES = 8
EPS = 1e-6
N_MOD = 6
POOL_WINDOWS = (2, 4, 8, 16)
N_GROUPS = 4
EXPERTS_PER_GROUP = 4
N_EXPERTS = N_GROUPS * EXPERTS_PER_GROUP
ROUTER_EXPERT_LANE0 = N_GROUPS
N_PAIRS = 6
N_CLASSES = N_GROUPS * N_PAIRS
PAIR_SLOT_A = (0, 0, 0, 1, 1, 3)
PAIR_SLOT_B = (1, 2, 3, 3, 2, 2)
ROUTE_CLASS_LANE0 = 32
MOE_TILE_ROWS = 256
HGRN_CHUNK = 128
HGRN_LEVELS = (64, 32, 16, 8, 4, 2, 1)
HGRN_GROUP = 8
VMEM_LIMIT = 48 * 1024 * 1024
LOG2_E = 1.4426950408889634
FOX_Q_SCALE = HEAD_DIM ** -0.5 * LOG2_E


def _cparams(sem):
    return pltpu.CompilerParams(dimension_semantics=sem, vmem_limit_bytes=VMEM_LIMIT)


def _silu(x):
    return x * jax.nn.sigmoid(x)


def _iota(shape, dim):
    return lax.broadcasted_iota(jnp.int32, shape, dim)


def _ada_kernel(c_ref, w_ref, b_ref, o_ref):
    sc = _silu(c_ref[...]).astype(BF16)
    o_ref[0] = jnp.dot(sc, w_ref[0].astype(BF16), preferred_element_type=F32) + b_ref[0]


def _ada_call(c, w_ada, b_ada):
    depth, d, n = w_ada.shape
    bsz = c.shape[0]
    tn = 1536
    return pl.pallas_call(
        _ada_kernel,
        out_shape=jax.ShapeDtypeStruct((depth, bsz, n), F32),
        grid=(depth, n // tn),
        in_specs=[
            pl.BlockSpec((bsz, d), lambda l, j: (0, 0)),
            pl.BlockSpec((1, d, tn), lambda l, j: (l, 0, j)),
            pl.BlockSpec((1, 1, tn), lambda l, j: (l, 0, j)),
        ],
        out_specs=pl.BlockSpec((1, bsz, tn), lambda l, j: (l, 0, j)),
        compiler_params=_cparams(("arbitrary", "arbitrary")),
        name="ada_mod",
    )(c, w_ada, b_ada.reshape(depth, 1, n))


def _lb_kernel(x_ref, o_ref):
    x = x_ref[...]
    depth = x.shape[0]
    e = jnp.exp(x - jnp.max(x, axis=0, keepdims=True))
    p = e / jnp.sum(e, axis=0, keepdims=True)
    acc = jnp.zeros_like(p[0:1])
    o_ref[0:1, :] = acc
    for l in range(1, depth):
        acc = acc + p[l:l + 1]
        o_ref[l:l + 1, :] = acc


def _lb_call(lb_logits):
    return pl.pallas_call(
        _lb_kernel,
        out_shape=jax.ShapeDtypeStruct(lb_logits.shape, F32),
        name="hgrn_lower_bounds",
    )(lb_logits)


def _norm_mod(x, g, shift, scale):
    ms = jnp.mean(x * x, axis=-1, keepdims=True)
    return (x * lax.rsqrt(ms + EPS) * g) * (1.0 + scale) + shift


def _inproj_kernel(x_ref, g_ref, mod_ref, wq_ref, wh_ref, wf_ref, qkv_ref, hg_ref, pu_ref, ff_ref, *, n_hg):
    h = _norm_mod(x_ref[0], g_ref[...], mod_ref[0, 0:1, :], mod_ref[0, 1:2, :]).astype(BF16)
    n_qkv = qkv_ref.shape[-1]
    q_scale = jnp.where(_iota((1, n_qkv), 1) < n_qkv // 3, FOX_Q_SCALE, 1.0)
    qkv_ref[0] = (jnp.dot(h, wq_ref[...], preferred_element_type=F32) * q_scale).astype(BF16)
    hg_ref[0] = jnp.dot(h, wh_ref[:, 0:n_hg], preferred_element_type=F32)
    pu_ref[0] = jnp.dot(h, wh_ref[:, n_hg:], preferred_element_type=F32)
    ff_ref[0] = jnp.dot(h, wf_ref[...], preferred_element_type=F32)


def _inproj_call(x, g, mod, w_qkv, w_hp, w_ff, n_hg, tm):
    bsz, seq, d = x.shape
    n_qkv, n_ff = w_qkv.shape[1], w_ff.shape[1]
    n_pu = w_hp.shape[1] - n_hg
    return pl.pallas_call(
        functools.partial(_inproj_kernel, n_hg=n_hg),
        out_shape=(
            jax.ShapeDtypeStruct((bsz, seq, n_qkv), BF16),
            jax.ShapeDtypeStruct((bsz, seq, n_hg), F32),
            jax.ShapeDtypeStruct((bsz, seq, n_pu), F32),
            jax.ShapeDtypeStruct((bsz, seq, n_ff), F32),
        ),
        grid=(bsz, seq // tm),
        in_specs=[
            pl.BlockSpec((1, tm, d), lambda b, i: (b, i, 0)),
            pl.BlockSpec((1, d), lambda b, i: (0, 0)),
            pl.BlockSpec((1, N_MOD, d), lambda b, i: (b, 0, 0)),
            pl.BlockSpec((d, n_qkv), lambda b, i: (0, 0)),
            pl.BlockSpec((d, n_hg + n_pu), lambda b, i: (0, 0)),
            pl.BlockSpec((d, n_ff), lambda b, i: (0, 0)),
        ],
        out_specs=(
            pl.BlockSpec((1, tm, n_qkv), lambda b, i: (b, i, 0)),
            pl.BlockSpec((1, tm, n_hg), lambda b, i: (b, i, 0)),
            pl.BlockSpec((1, tm, n_pu), lambda b, i: (b, i, 0)),
            pl.BlockSpec((1, tm, n_ff), lambda b, i: (b, i, 0)),
        ),
        compiler_params=_cparams(("arbitrary", "arbitrary")),
        name="norm_inproj",
    )(x, g, mod, w_qkv, w_hp, w_ff)


def _fbias_kernel(ff_ref, bias_ref, fcol_ref, *, cb):
    seq = ff_ref.shape[1]
    tri = (_iota((cb, cb), 0) >= _iota((cb, cb), 1)).astype(F32)
    carry = jnp.zeros((1, LANES), F32)
    for blk in range(seq // cb):
        x = ff_ref[0, blk * cb:(blk + 1) * cb, :] + bias_ref[...]
        log_f = jnp.minimum(x, 0.0) - jnp.log1p(jnp.exp(-jnp.abs(x)))
        cs = jnp.dot(tri, log_f, precision=HIGHEST, preferred_element_type=F32) + carry
        fcol_ref[0, blk * cb:(blk + 1) * cb, :] = cs * LOG2_E
        carry = cs[cb - 1:cb, :]


def _fbias_call(ff, bias):
    bsz, seq, _ = ff.shape
    cb = min(256, seq)
    return pl.pallas_call(
        functools.partial(_fbias_kernel, cb=cb),
        out_shape=jax.ShapeDtypeStruct((bsz, seq, LANES), F32),
        grid=(bsz,),
        in_specs=[pl.BlockSpec((1, seq, LANES), lambda b: (b, 0, 0)),
                  pl.BlockSpec((1, LANES), lambda b: (0, 0))],
        out_specs=pl.BlockSpec((1, seq, LANES), lambda b: (b, 0, 0)),
        compiler_params=_cparams(("arbitrary",)),
        name="fox_forget_bias",
    )(ff, bias)


def _split3(f):
    hi = f.astype(BF16).astype(F32)
    rest = f - hi
    mid = rest.astype(BF16).astype(F32)
    return hi, mid, rest - mid


def _fox_kernel(q_ref, k_ref, v_ref, fc_ref, gn_ref, o_ref, k0_sc, k1_sc, v0_sc, v1_sc, *, tq):
    p = pl.program_id(1)
    seq = k_ref.shape[1]
    nb = seq // tq
    lane = _iota((1, LANES), 1)
    h0 = lane < HEAD_DIM
    data = (h0, jnp.logical_not(h0))
    base = (HEAD_DIM, 0)
    k_sc = (k0_sc, k1_sc)
    v_sc = (v0_sc, v1_sc)
    nt = (((1,), (1,)), ((), ()))
    heads = (0, 1)

    def head_f(fc, which):
        return jnp.sum(jnp.where(lane == 2 * p + which, fc, 0.0), axis=-1, keepdims=True)

    def with_bias_lanes(x, which, first3, last3):
        out = jnp.where(data[which], x, 0.0)
        for n in range(3):
            out = jnp.where(lane == base[which] + n, first3[n], out)
            out = jnp.where(lane == base[which] + 3 + n, last3[n], out)
        return out.astype(BF16)

    qs = {}
    for blk in range(nb):
        rows = slice(blk * tq, (blk + 1) * tq)
        fc = fc_ref[0, rows, :]
        qf = q_ref[0, rows, :].astype(F32)
        kf = k_ref[0, rows, :].astype(F32)
        vf = v_ref[0, rows, :].astype(F32)
        for which in heads:
            hi, mid, lo = _split3(head_f(fc, which))
            qs[blk, which] = with_bias_lanes(qf, which, (hi, mid, lo), (1.0, 1.0, 1.0))
            k_sc[which][rows, :] = with_bias_lanes(kf, which, (1.0, 1.0, 1.0), (-hi, -mid, -lo))
            v_sc[which][rows, :] = jnp.where(data[which], vf,
                                             jnp.where(lane == base[which], 1.0, 0.0)).astype(BF16)

    half = tq // 2
    top_mask = _iota((half, half), 0) >= _iota((half, half), 1)
    bottom_mask = _iota((half, tq), 1) <= _iota((half, tq), 0) + half
    m, acc = {}, {}
    for r in range(nb):
        k0 = r * tq
        tasks = {}
        for which in heads:
            tasks[r, which, 0] = (qs[r, which][:half], slice(k0, k0 + half), top_mask, (r, which), slice(0, half))
            tasks[r, which, 1] = (qs[r, which][half:], slice(k0, k0 + tq), bottom_mask, (r, which),
                                  slice(half, tq))
            for i in range(r + 1, nb):
                tasks[i, which] = (qs[i, which], slice(k0, k0 + tq), None, (i, which), slice(0, tq))
        s = {}
        for t, (q_rows, key_rows, mask, _, _) in tasks.items():
            s[t] = lax.dot_general(q_rows, k_sc[t[1]][key_rows, :], nt, preferred_element_type=F32)
            if mask is not None:
                s[t] = jnp.where(mask, s[t], -jnp.inf)
        n = {}
        for t, (_, _, _, prev, rows) in tasks.items():
            n[t] = jnp.max(s[t], axis=-1, keepdims=True)
            if r > 0:
                n[t] = jnp.maximum(m[prev][rows], n[t])
        pv = {}
        for t, (_, key_rows, _, _, _) in tasks.items():
            pr = jnp.exp2(s[t] - n[t]).astype(BF16)
            pv[t] = jnp.dot(pr, v_sc[t[1]][key_rows, :], preferred_element_type=F32)
        new_m, new_acc = {}, {}
        for t, (_, _, _, prev, rows) in tasks.items():
            new_acc[t] = pv[t] if r == 0 else jnp.exp2(m[prev][rows] - n[t]) * acc[prev][rows] + pv[t]
            new_m[t] = n[t]
        m, acc = new_m, new_acc

        for part in range(2):
            a0, a1 = acc[r, 0, part], acc[r, 1, part]
            l0 = jnp.sum(jnp.where(lane == base[0], a0, 0.0), axis=-1, keepdims=True)
            l1 = jnp.sum(jnp.where(lane == base[1], a1, 0.0), axis=-1, keepdims=True)
            o = jnp.where(h0, a0 / l0, a1 / l1)
            o2 = o * o
            ms0 = jnp.sum(jnp.where(h0, o2, 0.0), axis=-1, keepdims=True)
            ms1 = jnp.sum(jnp.where(h0, 0.0, o2), axis=-1, keepdims=True)
            ms = jnp.where(h0, ms0, ms1) * (1.0 / HEAD_DIM)
            rows = slice(k0 + part * half, k0 + (part + 1) * half)
            o_ref[0, rows, :] = (o * lax.rsqrt(ms + EPS) * gn_ref[0]).astype(BF16)


def _fox_call(qkv, fcol, gn, n_pairs, tq):
    bsz, seq, _ = qkv.shape
    return pl.pallas_call(
        functools.partial(_fox_kernel, tq=tq),
        out_shape=jax.ShapeDtypeStruct((bsz, seq, n_pairs * LANES), BF16),
        grid=(bsz, n_pairs),
        in_specs=[
            pl.BlockSpec((1, seq, LANES), lambda b, p: (b, 0, p)),
            pl.BlockSpec((1, seq, LANES), lambda b, p: (b, 0, n_pairs + p)),
            pl.BlockSpec((1, seq, LANES), lambda b, p: (b, 0, 2 * n_pairs + p)),
            pl.BlockSpec((1, seq, LANES), lambda b, p: (b, 0, 0)),
            pl.BlockSpec((1, 1, LANES), lambda b, p: (p, 0, 0)),
        ],
        out_specs=pl.BlockSpec((1, seq, LANES), lambda b, p: (b, 0, p)),
        scratch_shapes=[pltpu.VMEM((seq, LANES), BF16)] * 4,
        compiler_params=_cparams(("arbitrary", "arbitrary")),
        name="fox_attention",
    )(qkv, qkv, qkv, fcol, gn)


def _hgrn_kernel(hq_ref, hf_ref, hi_ref, hg_ref, lb_ref, gn_ref, wg_ref, wu_ref, wd_ref,
                 o_ref, zero_ref, wgb_ref, wub_ref, wdb_ref):
    ch = HGRN_CHUNK
    seq = hq_ref.shape[1]
    lane = _iota((1, LANES), 1)
    h0 = lane < HEAD_DIM
    r = _iota((ch, ch), 0)
    c = _iota((ch, ch), 1)
    tri = (r >= c).astype(BF16)
    same_head = (r < HEAD_DIM) == (c < HEAD_DIM)
    seg = same_head.astype(BF16)
    r2 = _iota((ch, 2 * ch), 0)
    c2 = _iota((ch, 2 * ch), 1) % ch
    level_masks = [(r2 // (2 * m) == c2 // (2 * m)) & (r2 % (2 * m) >= m) & (c2 % (2 * m) < m)
                   for m in HGRN_LEVELS]
    small_levels = [m for m in HGRN_LEVELS if m < SUBLANES]
    pick = jnp.concatenate([(c == (r // (2 * m)) * (2 * m) + (m - 1)) for m in small_levels], axis=0).astype(BF16)
    lb = lb_ref[0]
    nt = (((1,), (1,)), ((), ()))
    zero_b = jnp.zeros((ch, LANES), BF16)
    zero_ref[...] = jnp.zeros(zero_ref.shape, F32)
    wgb_ref[...] = wg_ref[0].astype(BF16)
    wub_ref[...] = wu_ref[0].astype(BF16)
    wdb_ref[...] = wd_ref[0].astype(BF16)

    def both_heads(x):
        return jnp.concatenate([jnp.where(h0, x, zero_b), jnp.where(h0, zero_b, x)], axis=0)

    def sum3(x):
        return x[:, 0:LANES] + x[:, LANES:2 * LANES] + x[:, 2 * LANES:]

    def parts3(x):
        return jnp.concatenate([part.astype(BF16) for part in _split3(x)], axis=1)

    def group(gi, state_t):
        n = HGRN_GROUP
        sls = [pl.ds(pl.multiple_of((gi * n + i) * ch, ch), ch) for i in range(n)]
        k, q, v, vb, b, picked = [], [], [], [], [], []
        for sl in sls:
            f = lb + (1.0 - lb) * jax.nn.sigmoid(hf_ref[0, sl, :])
            k.append(1.0 - f)
            q.append(_silu(hq_ref[0, sl, :]))
            v.append(hi_ref[0, sl, :])
            vb.append(v[-1].astype(BF16))
            b.append(sum3(jnp.dot(tri, parts3(jnp.log(f) * LOG2_E), preferred_element_type=F32)))
        for i in range(n):
            picked.append(jnp.dot(pick, parts3(b[i]), preferred_element_type=F32))

        sc = [jnp.zeros((ch, 2 * ch), F32) for _ in range(n)]
        for li, m in enumerate(HGRN_LEVELS):
            for i in range(n):
                if m >= SUBLANES:
                    ref_l = jnp.broadcast_to(b[i].reshape(ch // (2 * m), 2 * m, LANES)[:, m - 1:m, :],
                                             (ch // (2 * m), 2 * m, LANES)).reshape(ch, LANES)
                else:
                    at = small_levels.index(m) * ch
                    ref_l = sum3(picked[i][at:at + ch])
                e = jnp.exp2(-jnp.abs(b[i] - ref_l))
                s = lax.dot_general((q[i] * e).astype(BF16), both_heads((k[i] * e).astype(BF16)), nt,
                                    preferred_element_type=F32)
                sc[i] = jnp.where(level_masks[li], s, sc[i])

        o, upd_t, dec = [], [], []
        for i in range(n):
            oi = jnp.dot(sc[i].astype(BF16), both_heads(vb[i]), preferred_element_type=F32)
            o.append(oi + jnp.dot((q[i] * k[i]).astype(BF16), seg, preferred_element_type=F32) * v[i])
            b_last = b[i][ch - 1:ch, :]
            k_dec = (k[i] * jnp.exp2(b_last - b[i])).astype(BF16)
            upd = jnp.dot(v[i].T.astype(BF16), k_dec, preferred_element_type=F32)
            upd_t.append(jnp.where(same_head, upd, 0.0))
            dec.append(jnp.exp2(b_last))

        states = [state_t]
        for i in range(n):
            states.append(dec[i] * states[i] + upd_t[i])

        for i, sl in enumerate(sls):
            oi = o[i] + lax.dot_general((q[i] * jnp.exp2(b[i])).astype(BF16), states[i].astype(BF16), nt,
                                        preferred_element_type=F32)
            o2 = oi * oi
            ms0 = jnp.sum(jnp.where(h0, o2, 0.0), axis=-1, keepdims=True)
            ms1 = jnp.sum(jnp.where(h0, 0.0, o2), axis=-1, keepdims=True)
            ms = jnp.where(h0, ms0, ms1) * (1.0 / HEAD_DIM)
            out = oi * lax.rsqrt(ms + EPS) * gn_ref[0] * _silu(hg_ref[0, sl, :])
            o_ref[0, sl, :] = out.astype(BF16)
        return states[n]

    lax.fori_loop(0, seq // (HGRN_GROUP * ch), group, jnp.zeros((LANES, LANES), F32))


def _hgrn_call(hg4, lb, gn, n_pairs, zero_rows, zero_width, wg, wu, wd, layer):
    bsz, seq, _ = hg4.shape
    _, n_e, d, f = wg.shape
    steps = bsz * n_pairs
    assert steps >= n_e
    zb = next(k * MOE_TILE_ROWS for k in range(1, zero_rows // MOE_TILE_ROWS + 1)
              if zero_rows % (k * MOE_TILE_ROWS) == 0 and zero_rows // (k * MOE_TILE_ROWS) <= steps)
    last_zero_block = zero_rows // zb - 1

    def spec(off):
        return pl.BlockSpec((1, seq, LANES), lambda b, p: (b, 0, off + p))

    def expert_in(b, p):
        return (layer, jnp.minimum(b * n_pairs + p, n_e - 1), 0, 0)

    def expert_out(b, p):
        return (jnp.minimum(b * n_pairs + p, n_e - 1), 0, 0)

    return pl.pallas_call(
        _hgrn_kernel,
        out_shape=(jax.ShapeDtypeStruct((bsz, seq, n_pairs * LANES), BF16),
                   jax.ShapeDtypeStruct((zero_rows, zero_width), F32),
                   jax.ShapeDtypeStruct((n_e, d, f), BF16),
                   jax.ShapeDtypeStruct((n_e, d, f), BF16),
                   jax.ShapeDtypeStruct((n_e, f, d), BF16)),
        grid=(bsz, n_pairs),
        in_specs=[spec(0), spec(n_pairs), spec(2 * n_pairs), spec(3 * n_pairs),
                  pl.BlockSpec((1, 1, LANES), lambda b, p: (p, 0, 0)),
                  pl.BlockSpec((1, 1, LANES), lambda b, p: (p, 0, 0)),
                  pl.BlockSpec((1, 1, d, f), expert_in), pl.BlockSpec((1, 1, d, f), expert_in),
                  pl.BlockSpec((1, 1, f, d), expert_in)],
        out_specs=(pl.BlockSpec((1, seq, LANES), lambda b, p: (b, 0, p)),
                   pl.BlockSpec((zb, zero_width), lambda b, p: (jnp.minimum(b * n_pairs + p, last_zero_block), 0)),
                   pl.BlockSpec((1, d, f), expert_out), pl.BlockSpec((1, d, f), expert_out),
                   pl.BlockSpec((1, f, d), expert_out)),
        compiler_params=_cparams(("arbitrary", "arbitrary")),
        name="hgrn2",
    )(hg4, hg4, hg4, hg4, lb, gn, wg, wu, wd)


def _pool_kernel(u_ref, w_ref, s_ref, o_ref):
    u = u_ref[0]
    seq, n = u.shape
    t = _iota((seq, 1), 0)
    lane = _iota((1, n), 1)

    def shifted(x, k):
        return jnp.where(t >= k, pltpu.roll(x, k, axis=0), 0.0)

    sums = []
    s = u
    for w in POOL_WINDOWS:
        s = s + shifted(s, w // 2)
        sums.append(s)
    pos1 = (t + 1).astype(F32)
    group = len(POOL_WINDOWS) - 1
    mean = sums[group] / jnp.minimum(pos1, float(POOL_WINDOWS[group]))
    group_dim = n // len(POOL_WINDOWS)
    for gi in range(group - 1, -1, -1):
        mean = jnp.where(lane < (gi + 1) * group_dim,
                         sums[gi] / jnp.minimum(pos1, float(POOL_WINDOWS[gi])), mean)
    pooled = (mean - u).astype(BF16)
    o_ref[0] = (jnp.dot(pooled, w_ref[...], preferred_element_type=F32) * s_ref[...]).astype(BF16)


def _pool_call(pu, w_bd, scale):
    bsz, seq, n = pu.shape
    return pl.pallas_call(
        _pool_kernel,
        out_shape=jax.ShapeDtypeStruct((bsz, seq, n), BF16),
        grid=(bsz,),
        in_specs=[pl.BlockSpec((1, seq, n), lambda b: (b, 0, 0)),
                  pl.BlockSpec((n, n), lambda b: (0, 0)),
                  pl.BlockSpec((1, n), lambda b: (0, 0))],
        out_specs=pl.BlockSpec((1, seq, n), lambda b: (b, 0, 0)),
        compiler_params=_cparams(("arbitrary",)),
        name="multiscale_pool",
    )(pu, w_bd, scale)


def _route(logits):
    lane = _iota(logits.shape, 1)
    lane_f = lane.astype(F32)
    big = float(LANES)
    is_g = lane < N_GROUPS
    gl = jnp.where(is_g, logits, -jnp.inf)
    gmax = jnp.max(gl, axis=-1, keepdims=True)
    gsum = jnp.sum(jnp.where(is_g, jnp.exp(gl - gmax), 0.0), axis=-1, keepdims=True)
    p_group = 1.0 / gsum
    g_sel = jnp.min(jnp.where(gl == gmax, lane_f, big), axis=-1, keepdims=True)
    lo = ROUTER_EXPERT_LANE0 + EXPERTS_PER_GROUP * g_sel
    in_group = (lane_f >= lo) & (lane_f < lo + EXPERTS_PER_GROUP)
    el = jnp.where(in_group, logits, -jnp.inf)
    m1 = jnp.max(el, axis=-1, keepdims=True)
    i1 = jnp.min(jnp.where(el == m1, lane_f, big), axis=-1, keepdims=True)
    el2 = jnp.where(lane_f == i1, -jnp.inf, el)
    m2 = jnp.max(el2, axis=-1, keepdims=True)
    i2 = jnp.min(jnp.where(el2 == m2, lane_f, big), axis=-1, keepdims=True)
    e2 = jnp.exp(m2 - m1)
    w1 = p_group / (1.0 + e2)
    w2 = p_group * e2 / (1.0 + e2)
    a = jnp.minimum(i1, i2) - lo
    b = jnp.maximum(i1, i2) - lo
    pair = jnp.where(a == 0.0, b - 1.0, jnp.where(a == 1.0, 6.0 - b, float(N_PAIRS - 1)))
    top1_in_a = (i1 < i2) != (pair == float(N_PAIRS - 1))
    cls_lane = ROUTE_CLASS_LANE0 + g_sel * N_PAIRS + pair
    return (jnp.where(lane == 0, jnp.where(top1_in_a, w1, w2), 0.0)
            + jnp.where(lane == 1, jnp.where(top1_in_a, w2, w1), 0.0)
            + jnp.where(lane_f == cls_lane, 1.0, 0.0))


def _outproj_kernel(of_ref, oh_ref, op_ref, x_ref, w_ref, mod_ref, g_ref, wr_ref, br_ref,
                    xn_ref, h2_ref, *, n_fox, n_hgrn):
    d = x_ref.shape[-1]
    y = jnp.dot(of_ref[...], w_ref[0:n_fox, :], preferred_element_type=F32)
    y = y + jnp.dot(oh_ref[...], w_ref[n_fox:n_fox + n_hgrn, :], preferred_element_type=F32)
    y = y + jnp.dot(op_ref[...], w_ref[n_fox + n_hgrn:, :], preferred_element_type=F32)
    xn = x_ref[...] + (1.0 + mod_ref[0, 2:3, :]) * y
    xn_ref[...] = xn
    h2 = _norm_mod(xn, g_ref[...], mod_ref[0, 3:4, :], mod_ref[0, 4:5, :])
    h2_ref[:, 0:d] = h2
    h_hi = h2.astype(BF16)
    h_lo = (h2 - h_hi.astype(F32)).astype(BF16)
    both = jnp.dot(h_hi, wr_ref[...], preferred_element_type=F32)
    logits = (both[:, 0:LANES] + both[:, LANES:]
              + jnp.dot(h_lo, wr_ref[:, 0:LANES], preferred_element_type=F32)) + br_ref[...]
    h2_ref[:, d:] = _route(logits)


def _outproj_call(o_fox, o_hgrn, o_pool, x, w_out, mod, g2, w_router, b_router, tm, seq):
    t, d = x.shape
    n_fox, n_hgrn, n_pool = o_fox.shape[-1], o_hgrn.shape[-1], o_pool.shape[-1]
    per_seq = seq // tm
    return pl.pallas_call(
        functools.partial(_outproj_kernel, n_fox=n_fox, n_hgrn=n_hgrn),
        out_shape=(jax.ShapeDtypeStruct((t, d), F32),
                   jax.ShapeDtypeStruct((t, d + LANES), F32)),
        grid=(t // tm,),
        in_specs=[
            pl.BlockSpec((tm, n_fox), lambda i: (i, 0)),
            pl.BlockSpec((tm, n_hgrn), lambda i: (i, 0)),
            pl.BlockSpec((tm, n_pool), lambda i: (i, 0)),
            pl.BlockSpec((tm, d), lambda i: (i, 0)),
            pl.BlockSpec((d, d), lambda i: (0, 0)),
            pl.BlockSpec((1, N_MOD, d), lambda i: (i // per_seq, 0, 0)),
            pl.BlockSpec((1, d), lambda i: (0, 0)),
            pl.BlockSpec((d, 2 * LANES), lambda i: (0, 0)),
            pl.BlockSpec((1, LANES), lambda i: (0, 0)),
        ],
        out_specs=(pl.BlockSpec((tm, d), lambda i: (i, 0)),
                   pl.BlockSpec((tm, d + LANES), lambda i: (i, 0))),
        compiler_params=_cparams(("arbitrary",)),
        name="outproj_router",
    )(o_fox, o_hgrn, o_pool, x, w_out, mod, g2, w_router, b_router)


def _sort_kernel(route_ref, pos_ref, meta_ref, *, tile_rows, cb):
    t = route_ref.shape[0]
    lane = _iota((1, LANES), 1)
    is_cls = (lane >= ROUTE_CLASS_LANE0) & (lane < ROUTE_CLASS_LANE0 + N_CLASSES)
    nblk = t // cb
    zero_row = jnp.zeros((1, LANES), F32)

    def onehot(i):
        return jnp.where(is_cls, route_ref[pl.ds(pl.multiple_of(i * cb, cb), cb), :], 0.0)

    counts = lax.fori_loop(0, nblk, lambda i, acc: acc + jnp.sum(onehot(i), axis=0, keepdims=True), zero_row)
    padded = jnp.floor((counts + float(tile_rows - 1)) * (1.0 / tile_rows)) * float(tile_rows)
    before = (_iota((LANES, LANES), 0) < _iota((LANES, LANES), 1)).astype(F32)
    offs = jnp.dot(jnp.broadcast_to(padded, (8, LANES)), before, precision=HIGHEST,
                   preferred_element_type=F32)[0:1]
    ends = offs + padded
    strict = (_iota((cb, cb), 0) > _iota((cb, cb), 1)).astype(BF16)
    ones8 = jnp.ones((8, LANES), F32)
    nt = (((1,), (1,)), ((), ()))

    def place(i, seen):
        oh = onehot(i)
        rank = jnp.dot(strict, oh.astype(BF16), preferred_element_type=F32) + seen
        dest = oh * (rank + offs)
        rows = lax.dot_general(ones8, dest, nt, precision=HIGHEST, preferred_element_type=F32)
        pos_ref[:, pl.ds(pl.multiple_of(i * cb, cb), cb)] = rows.astype(jnp.int32)
        return seen + jnp.sum(oh, axis=0, keepdims=True)

    lax.fori_loop(0, nblk, place, zero_row)
    tile_start = _iota((LANES, LANES), 0).astype(F32) * float(tile_rows)
    tile_cls = jnp.sum(jnp.where(is_cls & (ends <= tile_start), 1.0, 0.0), axis=-1, keepdims=True)
    n_used = jnp.sum(jnp.where(lane == ROUTE_CLASS_LANE0 + N_CLASSES - 1, ends, 0.0), axis=-1,
                     keepdims=True) * (1.0 / tile_rows)
    meta_ref[...] = jnp.where(lane == 0, tile_cls, jnp.where(lane == 1, n_used, 0.0))


def _sort_call(h2ext, d, tile_rows):
    t = h2ext.shape[0]
    cb = min(512, t)
    return pl.pallas_call(
        functools.partial(_sort_kernel, tile_rows=tile_rows, cb=cb),
        out_shape=(jax.ShapeDtypeStruct((8, t), jnp.int32),
                   jax.ShapeDtypeStruct((LANES, LANES), F32)),
        grid=(1,),
        in_specs=[pl.BlockSpec((t, LANES), lambda i: (0, d // LANES))],
        out_specs=(pl.BlockSpec((8, t), lambda i: (0, 0)),
                   pl.BlockSpec((LANES, LANES), lambda i: (0, 0))),
        compiler_params=_cparams(("arbitrary",)),
        name="route_sort",
    )(h2ext)


def _dispatch_kernel(pos_ref, h_ref, init_ref, out_ref, sem, *, tm):
    del init_ref
    base = pl.program_id(0) * tm
    for g in range(tm // SUBLANES):
        for u in range(SUBLANES):
            dst = pos_ref[base + g * SUBLANES + u]
            pltpu.make_async_copy(h_ref.at[g, pl.ds(u, 1), :], out_ref.at[pl.ds(dst, 1), :], sem).start()
    pltpu.make_async_copy(h_ref, h_ref, sem).wait()


def _dispatch_call(pos, h2ext, zeros, tm):
    t, w = h2ext.shape
    return pl.pallas_call(
        functools.partial(_dispatch_kernel, tm=tm),
        out_shape=jax.ShapeDtypeStruct(zeros.shape, F32),
        grid_spec=pltpu.PrefetchScalarGridSpec(
            num_scalar_prefetch=1,
            grid=(t // tm,),
            in_specs=[pl.BlockSpec((tm // SUBLANES, SUBLANES, w), lambda i, pos: (i, 0, 0)),
                      pl.BlockSpec(memory_space=pl.ANY)],
            out_specs=pl.BlockSpec(memory_space=pl.ANY),
            scratch_shapes=[pltpu.SemaphoreType.DMA],
        ),
        input_output_aliases={2: 0},
        compiler_params=_cparams(("arbitrary",)),
        name="moe_dispatch",
    )(pos, h2ext.reshape(t // SUBLANES, SUBLANES, w), zeros)


def _moe_kernel(ea_ref, eb_ref, nu_ref, h_ref, wga_ref, wua_ref, wda_ref, wgb_ref, wub_ref, wdb_ref, o_ref):
    del ea_ref, eb_ref
    d = o_ref.shape[-1]

    @pl.when(pl.program_id(0) < nu_ref[0])
    def _():
        h = h_ref[:, 0:d].astype(BF16)
        tail = h_ref[:, d:]
        lane = _iota((1, LANES), 1)
        w_a = jnp.sum(jnp.where(lane == 0, tail, 0.0), axis=-1, keepdims=True)
        w_b = jnp.sum(jnp.where(lane == 1, tail, 0.0), axis=-1, keepdims=True)

        def expert(wg_ref, wu_ref, wd_ref):
            hidden = _silu(jnp.dot(h, wg_ref[0], preferred_element_type=F32)) * jnp.dot(
                h, wu_ref[0], preferred_element_type=F32)
            return jnp.dot(hidden.astype(BF16), wd_ref[0], preferred_element_type=F32)

        o_ref[...] = w_a * expert(wga_ref, wua_ref, wda_ref) + w_b * expert(wgb_ref, wub_ref, wdb_ref)

    @pl.when(pl.program_id(0) >= nu_ref[0])
    def _():
        o_ref[...] = jnp.zeros_like(o_ref)


def _moe_call(ea, eb, n_used, h_sorted, wg, wu, wd, tile_rows):
    n_rows, w = h_sorted.shape
    _, d, f = wg.shape

    def rows(j, ea, eb, nu):
        return (jnp.maximum(jnp.minimum(j, nu[0] - 1), 0), 0)

    def wa(j, ea, eb, nu):
        return (ea[j], 0, 0)

    def wb(j, ea, eb, nu):
        return (eb[j], 0, 0)

    return pl.pallas_call(
        _moe_kernel,
        out_shape=jax.ShapeDtypeStruct((n_rows, d), F32),
        grid_spec=pltpu.PrefetchScalarGridSpec(
            num_scalar_prefetch=3,
            grid=(n_rows // tile_rows,),
            in_specs=[pl.BlockSpec((tile_rows, w), rows),
                      pl.BlockSpec((1, d, f), wa), pl.BlockSpec((1, d, f), wa), pl.BlockSpec((1, f, d), wa),
                      pl.BlockSpec((1, d, f), wb), pl.BlockSpec((1, d, f), wb), pl.BlockSpec((1, f, d), wb)],
            out_specs=pl.BlockSpec((tile_rows, d), lambda j, ea, eb, nu: (j, 0)),
        ),
        compiler_params=_cparams(("arbitrary",)),
        name="moe_experts",
    )(ea, eb, n_used, h_sorted, wg, wu, wd, wg, wu, wd)


def _combine_kernel(pos_ref, y_ref, x_ref, mod_ref, fg_ref, o_ref, buf, sem, *, tm):
    i = pl.program_id(0)

    def issue(tile, slot):
        for g in range(tm // SUBLANES):
            for u in range(SUBLANES):
                src = pos_ref[tile * tm + g * SUBLANES + u]
                pltpu.make_async_copy(y_ref.at[pl.ds(src, 1), :], buf.at[slot, g, pl.ds(u, 1), :],
                                      sem.at[slot]).start()

    @pl.when(i == 0)
    def _():
        issue(0, 0)

    @pl.when(i + 1 < pl.num_programs(0))
    def _():
        issue(i + 1, (i + 1) % 2)

    slot = i % 2
    pltpu.make_async_copy(buf.at[slot], buf.at[slot], sem.at[slot]).wait()
    out = x_ref[...] + (1.0 + mod_ref[0, 5:6, :]) * buf[slot].reshape(x_ref.shape)
    ms = jnp.mean(out * out, axis=-1, keepdims=True)
    o_ref[...] = out * lax.rsqrt(ms + EPS) * fg_ref[...]


def _combine_call(pos, y_sorted, x, mod, final_g, tm, seq):
    t, d = x.shape
    per_seq = seq // tm
    return pl.pallas_call(
        functools.partial(_combine_kernel, tm=tm),
        out_shape=jax.ShapeDtypeStruct((t, d), F32),
        grid_spec=pltpu.PrefetchScalarGridSpec(
            num_scalar_prefetch=1,
            grid=(t // tm,),
            in_specs=[pl.BlockSpec(memory_space=pl.ANY),
                      pl.BlockSpec((tm, d), lambda i, pos: (i, 0)),
                      pl.BlockSpec((1, N_MOD, d), lambda i, pos: (i // per_seq, 0, 0)),
                      pl.BlockSpec((1, d), lambda i, pos: (0, 0))],
            out_specs=pl.BlockSpec((tm, d), lambda i, pos: (i, 0)),
            scratch_shapes=[pltpu.VMEM((2, tm // SUBLANES, SUBLANES, d), F32), pltpu.SemaphoreType.DMA((2,))],
        ),
        compiler_params=_cparams(("arbitrary",)),
        name="moe_combine",
    )(pos, y_sorted, x, mod, final_g)


def _combine_inproj_kernel(pos_ref, y_ref, x2_ref, modp_ref, g_ref, mod_ref, wq_ref, wh_ref, wf_ref,
                           x_ref, qkv_ref, hg_ref, pu_ref, ff_ref, buf, sem, *, tm, n_hg):
    i = pl.program_id(0)
    last = pl.num_programs(0) - 1
    groups = tm // SUBLANES

    def row_copy(tile, slot, g, u):
        src = pos_ref[tile * tm + g * SUBLANES + u]
        return pltpu.make_async_copy(y_ref.at[pl.ds(src, 1), :], buf.at[slot, g, pl.ds(u, 1), :], sem.at[slot])

    def wait_slot(slot):
        pltpu.make_async_copy(buf.at[slot], buf.at[slot], sem.at[slot]).wait()

    @pl.when(i == 0)
    def _():
        def eight(g, carry):
            for u in range(SUBLANES):
                row_copy(0, 0, g, u).start()
            return carry
        lax.fori_loop(0, groups, eight, 0)

    slot = i % 2
    wait_slot(slot)
    x = x2_ref[...] + (1.0 + modp_ref[0, 5:6, :]) * buf[slot].reshape(x2_ref.shape)
    x_ref[...] = x
    h = _norm_mod(x, g_ref[...], mod_ref[0, 0:1, :], mod_ref[0, 1:2, :]).astype(BF16)

    nxt = jnp.minimum(i + 1, last)
    for g in range(groups):
        for u in range(SUBLANES):
            row_copy(nxt, 1 - slot, g, u).start()

    n_qkv = qkv_ref.shape[-1]
    q_scale = jnp.where(_iota((1, n_qkv), 1) < n_qkv // 3, FOX_Q_SCALE, 1.0)
    qkv_ref[...] = (jnp.dot(h, wq_ref[...], preferred_element_type=F32) * q_scale).astype(BF16)
    hg_ref[...] = jnp.dot(h, wh_ref[:, 0:n_hg], preferred_element_type=F32)
    pu_ref[...] = jnp.dot(h, wh_ref[:, n_hg:], preferred_element_type=F32)
    ff_ref[...] = jnp.dot(h, wf_ref[...], preferred_element_type=F32)

    @pl.when(i == last)
    def _():
        wait_slot(1 - slot)


def _combine_inproj_call(pos, y_sorted, x2, mod_prev, g, mod, w_qkv, w_hp, w_ff, n_hg, tm, seq):
    t, d = x2.shape
    per_seq = seq // tm
    n_qkv, n_ff = w_qkv.shape[1], w_ff.shape[1]
    n_pu = w_hp.shape[1] - n_hg

    def tile(i, pos):
        return (i, 0)

    def fixed(i, pos):
        return (0, 0)

    def per_batch(i, pos):
        return (i // per_seq, 0, 0)

    return pl.pallas_call(
        functools.partial(_combine_inproj_kernel, tm=tm, n_hg=n_hg),
        out_shape=(
            jax.ShapeDtypeStruct((t, d), F32),
            jax.ShapeDtypeStruct((t, n_qkv), BF16),
            jax.ShapeDtypeStruct((t, n_hg), F32),
            jax.ShapeDtypeStruct((t, n_pu), F32),
            jax.ShapeDtypeStruct((t, n_ff), F32),
        ),
        grid_spec=pltpu.PrefetchScalarGridSpec(
            num_scalar_prefetch=1,
            grid=(t // tm,),
            in_specs=[pl.BlockSpec(memory_space=pl.ANY),
                      pl.BlockSpec((tm, d), tile),
                      pl.BlockSpec((1, N_MOD, d), per_batch),
                      pl.BlockSpec((1, d), fixed),
                      pl.BlockSpec((1, N_MOD, d), per_batch),
                      pl.BlockSpec((d, n_qkv), fixed),
                      pl.BlockSpec((d, n_hg + n_pu), fixed),
                      pl.BlockSpec((d, n_ff), fixed)],
            out_specs=(pl.BlockSpec((tm, d), tile), pl.BlockSpec((tm, n_qkv), tile), pl.BlockSpec((tm, n_hg), tile),
                       pl.BlockSpec((tm, n_pu), tile), pl.BlockSpec((tm, n_ff), tile)),
            scratch_shapes=[pltpu.VMEM((2, tm // SUBLANES, SUBLANES, d), F32), pltpu.SemaphoreType.DMA((2,))],
        ),
        compiler_params=_cparams(("arbitrary",)),
        name="combine_norm_inproj",
    )(pos, y_sorted, x2, mod_prev, g, mod, w_qkv, w_hp, w_ff)


def _pad_lanes(a, n=LANES):
    return jnp.pad(a, [(0, 0)] * (a.ndim - 1) + [(0, n - a.shape[-1])])


def kernel(x, c, w_ada, b_ada, norm1_g, w_in, fox_f_bias, fox_norm_g, hgrn_lb_logits, hgrn_norm_g, pool_w, pool_scale, w_out, norm2_g, router_group_w, router_group_b, router_expert_w, router_expert_b, expert_w_gate, expert_w_up, expert_w_down, final_norm_g):
    bsz, seq, d = x.shape
    depth = w_ada.shape[0]
    fox_heads = fox_f_bias.shape[1]
    fox_dim = fox_heads * HEAD_DIM
    hgrn_dim = hgrn_lb_logits.shape[1]
    pool_dim = pool_scale.shape[1]
    n_fox_pairs = fox_dim // LANES
    n_hgrn_pairs = hgrn_dim // LANES
    t = bsz * seq
    tm = min(512, seq)
    tq = min(512, seq)
    n_tiles = t // MOE_TILE_ROWS + N_CLASSES
    assert n_tiles <= LANES and t % MOE_TILE_ROWS == 0

    o_ff = 3 * fox_dim
    w_qkv = w_in[:, :, :o_ff].astype(BF16)
    w_hp = w_in[:, :, o_ff + fox_heads:].astype(BF16)
    w_ff = _pad_lanes(w_in[:, :, o_ff:o_ff + fox_heads]).astype(BF16)
    f_bias = _pad_lanes(fox_f_bias)
    groups = pool_w.shape[1]
    pool_bd = jnp.einsum('lgcd,gh->lgchd', pool_w, jnp.eye(groups, dtype=pool_w.dtype)).reshape(
        depth, pool_dim, pool_dim).astype(BF16)
    w_router = _pad_lanes(jnp.concatenate([router_group_w, router_expert_w], axis=-1))
    w_router_hi = w_router.astype(BF16)
    w_router = jnp.concatenate([w_router_hi, (w_router - w_router_hi.astype(F32)).astype(BF16)],
                               axis=-1)
    b_router = _pad_lanes(jnp.concatenate([router_group_b, router_expert_b], axis=-1))
    w_out_b = w_out.astype(BF16)
    pair_a = jnp.array(PAIR_SLOT_A, jnp.int32)
    pair_b = jnp.array(PAIR_SLOT_B, jnp.int32)
    final_g = final_norm_g.reshape(1, d)

    mod_all = _ada_call(c, w_ada, b_ada).reshape(depth, bsz, N_MOD, d)
    lower = _lb_call(hgrn_lb_logits)

    pending = None
    for l in range(depth):
        mod = mod_all[l]
        if pending is None:
            qkv, hg4, pu, ff = _inproj_call(x, norm1_g[l:l + 1], mod, w_qkv[l], w_hp[l], w_ff[l], 4 * hgrn_dim, tm)
        else:
            x, qkv, hg4, pu, ff = _combine_inproj_call(*pending, norm1_g[l:l + 1], mod, w_qkv[l], w_hp[l], w_ff[l],
                                                       4 * hgrn_dim, tm, seq)
            x, qkv, hg4, pu, ff = (a.reshape(bsz, seq, -1) for a in (x, qkv, hg4, pu, ff))
        fcol = _fbias_call(ff, f_bias[l:l + 1])
        o_fox = _fox_call(qkv, fcol, fox_norm_g[l].reshape(n_fox_pairs, 1, LANES), n_fox_pairs, tq)
        o_hgrn, zeros, wg_l, wu_l, wd_l = _hgrn_call(
            hg4, lower[l].reshape(n_hgrn_pairs, 1, LANES), hgrn_norm_g[l].reshape(n_hgrn_pairs, 1, LANES),
            n_hgrn_pairs, n_tiles * MOE_TILE_ROWS, d + LANES, expert_w_gate, expert_w_up, expert_w_down, l)
        o_pool = _pool_call(pu, pool_bd[l], pool_scale[l:l + 1])
        x2, h2ext = _outproj_call(o_fox.reshape(t, -1), o_hgrn.reshape(t, -1), o_pool.reshape(t, -1),
                                  x.reshape(t, d), w_out_b[l], mod, norm2_g[l:l + 1],
                                  w_router[l], b_router[l:l + 1], tm, seq)
        pos8, meta = _sort_call(h2ext, d, MOE_TILE_ROWS)
        pos = pos8[0]
        n_used = meta[0, 1].astype(jnp.int32).reshape(1)
        tile = jnp.minimum(jnp.arange(n_tiles, dtype=jnp.int32), n_used[0] - 1)
        tile_cls = jnp.minimum(meta[:, 0].astype(jnp.int32)[tile], N_CLASSES - 1)
        e0 = (tile_cls // N_PAIRS) * EXPERTS_PER_GROUP
        ea = e0 + pair_a[tile_cls % N_PAIRS]
        eb = e0 + pair_b[tile_cls % N_PAIRS]
        h_sorted = _dispatch_call(pos, h2ext, zeros, tm)
        y_sorted = _moe_call(ea, eb, n_used, h_sorted, wg_l, wu_l, wd_l, MOE_TILE_ROWS)
        pending = (pos, y_sorted, x2, mod)
    return _combine_call(*pending, final_g, tm, seq).reshape(bsz, seq, d)
```
